```python
import math
import jax, jax.numpy as jnp
from jax import lax
import numpy as np

D_MODEL = 1024
BATCH = 8
SEQ = 2048
DEPTH = 2
DEC_BATCH = 128
DEC_SEQ = 1
PAST_LEN = 2048
PAGE_SIZE = 128

HEAD_DIM = 64
BRANCH_WIDTH = D_MODEL // 2
N_BRANCH = 3
FOX_HEADS = BRANCH_WIDTH // HEAD_DIM
FOX_WIDTH = FOX_HEADS * HEAD_DIM
GMLP_WIDTH = BRANCH_WIDTH
GMLP_GROUPS = 8
GMLP_GROUP_DIM = GMLP_WIDTH // GMLP_GROUPS
CHUNK = 128
NSA_HEADS = BRANCH_WIDTH // HEAD_DIM
NSA_KV_GROUPS = 2
NSA_REP = NSA_HEADS // NSA_KV_GROUPS
NSA_WIDTH = NSA_HEADS * HEAD_DIM
NSA_KV_WIDTH = NSA_KV_GROUPS * HEAD_DIM
CMP_LEN = 32
CMP_STRIDE = 16
SEL_BLOCK = 64
N_SEL = 8
WINDOW = 512
Q_BLOCK = 128
D_FF = ((8 * D_MODEL // 3 + 127) // 128) * 128
D_IN = 3 * FOX_WIDTH + FOX_HEADS + 2 * GMLP_WIDTH + NSA_WIDTH + 6 * NSA_KV_WIDTH + 3 * NSA_HEADS + N_BRANCH * D_MODEL
FORCE_BONUS = 1.0e4
NEG = -1.0e30
EPS = 1e-6

kernel_name = 'fox_gmlp_nsa_macaron_hybrid_step'


def _split_points():
    sizes = [3 * FOX_WIDTH, FOX_HEADS, 2 * GMLP_WIDTH, NSA_WIDTH, 6 * NSA_KV_WIDTH, 3 * NSA_HEADS, N_BRANCH * D_MODEL]
    return [int(v) for v in np.cumsum(sizes)[:-1]]


def rmsnorm(x, g):
    xf = x.astype(jnp.float32)
    y = xf * lax.rsqrt(jnp.mean(xf * xf, axis=-1, keepdims=True) + EPS)
    return (y * g.astype(jnp.float32)).astype(x.dtype)


def swiglu(x, w_gu, w_down):
    g, u = jnp.split(x @ w_gu, 2, axis=-1)
    return (jax.nn.silu(g) * u) @ w_down


def alibi_slopes():
    return jnp.asarray(2.0 ** (-8.0 * np.arange(1, NSA_HEADS + 1) / NSA_HEADS), dtype=jnp.float32)


def masked_softmax(s, mask):
    p = jax.nn.softmax(jnp.where(mask, s, NEG), axis=-1)
    return jnp.where(mask, p, 0.0)


def seq_blocks(a):
    b, s = a.shape[:2]
    return jnp.moveaxis(a.reshape((b, s // Q_BLOCK, Q_BLOCK) + a.shape[2:]), 1, 0)


def unblock(a):
    a = jnp.moveaxis(a, 0, 1)
    return a.reshape((a.shape[0], a.shape[1] * a.shape[2]) + a.shape[3:])


def gather_pages(pool, page_table):
    g = pool[page_table]
    return g.reshape((g.shape[0], g.shape[1] * g.shape[2]) + g.shape[3:])


def fox_attend(q, c_q, q_pos, k, v, c_k, k_pos):
    s = jnp.einsum('bqhd,bkhd->bhqk', q, k).astype(jnp.float32) * (HEAD_DIM ** -0.5)
    s = s + (jnp.swapaxes(c_q, 1, 2)[..., :, None] - jnp.swapaxes(c_k, 1, 2)[..., None, :])
    mask = k_pos[None, :] <= q_pos[:, None]
    p = masked_softmax(s, mask)
    return jnp.einsum('bhqk,bkhd->bqhd', p.astype(v.dtype), v)


def fox_prompt(q, k, v, logf):
    s_len = q.shape[1]
    c = jnp.cumsum(logf, axis=1)
    pos = jnp.arange(s_len)

    def body(xs):
        qb, cb, pb = xs
        return fox_attend(qb, cb, pb, k, v, c, pos)

    return unblock(lax.map(body, (seq_blocks(q), seq_blocks(c), pos.reshape(-1, Q_BLOCK))))


def gmlp_mix(u, v, w_s, b_s):
    b, t = u.shape[:2]
    n = min(t, CHUNK)
    nc = t // n
    ws = jnp.tril(w_s[:, :n, :n])
    vb = v.reshape(b, nc, n, GMLP_GROUPS, GMLP_GROUP_DIM)
    mixed = jnp.einsum('gts,bcsgd->bctgd', ws, vb) + jnp.swapaxes(b_s[:, :n], 0, 1)[None, None, :, :, None]
    return u * mixed.reshape(b, t, GMLP_WIDTH)


def nsa_compress(rows, pe, w1, w2):
    t = rows.shape[1]
    n_cmp = (t - CMP_LEN) // CMP_STRIDE + 1
    idx = jnp.arange(n_cmp)[:, None] * CMP_STRIDE + jnp.arange(CMP_LEN)[None, :]
    blk = rows[:, idx] + pe[None, None, :, None, :]
    blk = jnp.moveaxis(blk, 3, 2).reshape(rows.shape[0], n_cmp, NSA_KV_GROUPS, CMP_LEN * HEAD_DIM)
    return jax.nn.silu(blk @ w1) @ w2


def nsa_cmp_attend(q, q_pos, kc, vc, slopes):
    b, tq = q.shape[:2]
    qg = q.reshape(b, tq, NSA_KV_GROUPS, NSA_REP, HEAD_DIM)
    n_cmp = kc.shape[1]
    end = jnp.arange(n_cmp) * CMP_STRIDE + CMP_LEN - 1
    s = jnp.einsum('bqgrd,bngd->bqgrn', qg, kc).astype(jnp.float32) * (HEAD_DIM ** -0.5)
    dist = (q_pos[:, None] - end[None, :]).astype(jnp.float32)
    s = s - slopes.reshape(NSA_KV_GROUPS, NSA_REP)[None, None, :, :, None] * dist[None, :, None, None, :]
    mask = (end[None, :] <= q_pos[:, None])[None, :, None, None, :]
    p = masked_softmax(s, mask)
    o = jnp.einsum('bqgrn,bngd->bqgrd', p.astype(vc.dtype), vc).reshape(b, tq, NSA_HEADS, HEAD_DIM)
    return o, p.sum(axis=3)


def nsa_select(imp, q_pos, n_blk):
    per = SEL_BLOCK // CMP_STRIDE
    n_cmp = imp.shape[-1]
    imp = jnp.pad(imp, ((0, 0), (0, 0), (0, 0), (0, n_blk * per - n_cmp)))
    score = imp.reshape(imp.shape[:3] + (n_blk, per)).sum(-1)
    blk = jnp.arange(n_blk)[None, :]
    cur = (q_pos // SEL_BLOCK)[:, None]
    forced = (blk == 0) | (blk == cur) | (blk == cur - 1)
    future = blk > cur
    score = jnp.where(forced[None, :, None, :], score + FORCE_BONUS, score)
    score = jnp.where(future[None, :, None, :], NEG, score)
    vals, idx = lax.top_k(score, min(N_SEL, n_blk))
    return idx, vals > 0.5 * NEG


def nsa_sel_attend(q, q_pos, ks, vs, idx, valid, slopes):
    b, tq = q.shape[:2]
    tok = (idx[..., None] * SEL_BLOCK + jnp.arange(SEL_BLOCK)).reshape(b, tq, NSA_KV_GROUPS, -1)
    ok = jnp.repeat(valid, SEL_BLOCK, axis=-1) & (tok <= q_pos[None, :, None, None])
    bi = jnp.arange(b)[:, None, None, None]
    gi = jnp.arange(NSA_KV_GROUPS)[None, None, :, None]
    kg = ks[bi, tok, gi]
    vg = vs[bi, tok, gi]
    qg = q.reshape(b, tq, NSA_KV_GROUPS, NSA_REP, HEAD_DIM)
    s = jnp.einsum('bqgrd,bqgnd->bqgrn', qg, kg).astype(jnp.float32) * (HEAD_DIM ** -0.5)
    dist = (q_pos[None, :, None, None] - tok).astype(jnp.float32)
    s = s - slopes.reshape(NSA_KV_GROUPS, NSA_REP)[None, None, :, :, None] * dist[:, :, :, None, :]
    p = masked_softmax(s, ok[:, :, :, None, :])
    return jnp.einsum('bqgrn,bqgnd->bqgrd', p.astype(vg.dtype), vg).reshape(b, tq, NSA_HEADS, HEAD_DIM)


def nsa_win_attend(q, q_pos, kw, vw, k_pos, slopes):
    b, tq = q.shape[:2]
    qg = q.reshape(b, tq, NSA_KV_GROUPS, NSA_REP, HEAD_DIM)
    s = jnp.einsum('bqgrd,bkgd->bqgrk', qg, kw).astype(jnp.float32) * (HEAD_DIM ** -0.5)
    d = q_pos[:, None] - k_pos[None, :]
    mask = (d >= 0) & (d < WINDOW) & (k_pos[None, :] >= 0)
    s = s - slopes.reshape(NSA_KV_GROUPS, NSA_REP)[None, None, :, :, None] * d.astype(jnp.float32)[None, :, None, None, :]
    p = masked_softmax(s, mask[None, :, None, None, :])
    return jnp.einsum('bqgrk,bkgd->bqgrd', p.astype(vw.dtype), vw).reshape(b, tq, NSA_HEADS, HEAD_DIM)


def win_prompt(q, kw, vw, slopes):
    b, s_len = q.shape[:2]
    nb = s_len // Q_BLOCK
    nw = WINDOW // Q_BLOCK
    bidx = jnp.arange(nb)[:, None] + jnp.arange(nw + 1)[None, :]

    def band(a):
        ap = jnp.pad(a, ((0, 0), (WINDOW, 0), (0, 0), (0, 0))).reshape((b, nb + nw, Q_BLOCK) + a.shape[2:])
        ab = ap[:, bidx].reshape((b, nb, (nw + 1) * Q_BLOCK) + a.shape[2:])
        return jnp.moveaxis(ab, 1, 0)

    k_pos = jnp.arange(nb)[:, None] * Q_BLOCK - WINDOW + jnp.arange((nw + 1) * Q_BLOCK)[None, :]
    q_pos = jnp.arange(s_len).reshape(nb, Q_BLOCK)

    def body(xs):
        qb, pb, kb, vb, kp = xs
        return nsa_win_attend(qb, pb, kb, vb, kp, slopes)

    return unblock(lax.map(body, (seq_blocks(q), q_pos, band(kw), band(vw), k_pos)))


def token_mixer(h, lw, past):
    b, t = h.shape[:2]
    z = h @ lw['w_in']
    fox_qkv, fox_f, gm_uv, nsa_q, nsa_kv, nsa_g, merge_g = jnp.split(z, _split_points(), axis=-1)
    slopes = alibi_slopes()

    fqkv = fox_qkv.reshape(b, t, 3, FOX_HEADS, HEAD_DIM)
    fq = rmsnorm(fqkv[:, :, 0], lw['fox_qk_gain'][0])
    fk = rmsnorm(fqkv[:, :, 1], lw['fox_qk_gain'][1])
    fv = fqkv[:, :, 2]
    logf = jax.nn.log_sigmoid((fox_f + lw['fox_b_f']).astype(jnp.float32))
    new_fox_kv = jnp.stack([fk, fv], axis=2)

    u, v = jnp.split(jax.nn.gelu(gm_uv), 2, axis=-1)
    v = rmsnorm(v, lw['gmlp_v_gain'])
    o_gm = gmlp_mix(u, v, lw['gmlp_w_s'], lw['gmlp_b_s'])
    gm_state = v[:, t - min(CHUNK, t):]

    nq = rmsnorm(nsa_q.reshape(b, t, NSA_HEADS, HEAD_DIM), lw['nsa_q_gain'])
    kv6 = nsa_kv.reshape(b, t, 6, NSA_KV_GROUPS, HEAD_DIM)
    kg = lw['nsa_k_gain']
    new_nsa_kv = jnp.stack([kv6[:, :, 0], kv6[:, :, 1], rmsnorm(kv6[:, :, 2], kg[1]), kv6[:, :, 3]], axis=2)
    new_win = jnp.stack([rmsnorm(kv6[:, :, 4], kg[2]), kv6[:, :, 5]], axis=2)

    if past is None:
        start = 0
        q_pos = jnp.arange(t)
        o_fox = fox_prompt(fq, fk, fv, logf)
        nsa_all = new_nsa_kv
        o_win = win_prompt(nq, new_win[:, :, 0], new_win[:, :, 1], slopes)
        win_state = new_win[:, t - min(WINDOW, t):]
    else:
        start = past['fox_kv'].shape[1]
        q_pos = start + jnp.arange(t)
        fox_all = jnp.concatenate([past['fox_kv'], new_fox_kv], axis=1)
        c = jnp.cumsum(jnp.concatenate([past['fox_logf'].astype(jnp.float32), logf], axis=1), axis=1)
        o_fox = fox_attend(fq, c[:, start:], q_pos, fox_all[:, :, 0], fox_all[:, :, 1], c, jnp.arange(start + t))
        nsa_all = jnp.concatenate([past['nsa_kv'], new_nsa_kv], axis=1)
        w_buf = past['win'].shape[1]
        w_all = jnp.concatenate([past['win'], new_win], axis=1)
        k_pos = start - w_buf + jnp.arange(w_buf + t)
        o_win = nsa_win_attend(nq, q_pos, w_all[:, :, 0], w_all[:, :, 1], k_pos, slopes)
        win_state = w_all[:, t:]

    t_all = start + t
    kc = rmsnorm(nsa_compress(nsa_all[:, :, 0], lw['nsa_cmp_pe'][0], lw['nsa_cmp_w1'][0], lw['nsa_cmp_w2'][0]), kg[0])
    vc = nsa_compress(nsa_all[:, :, 1], lw['nsa_cmp_pe'][1], lw['nsa_cmp_w1'][1], lw['nsa_cmp_w2'][1])
    o_cmp, imp = nsa_cmp_attend(nq, q_pos, kc, vc, slopes)

    n_blk = -(-t_all // SEL_BLOCK)
    idx, valid = nsa_select(imp, q_pos, n_blk)
    pad = n_blk * SEL_BLOCK - t_all
    ks = jnp.pad(nsa_all[:, :, 2], ((0, 0), (0, pad), (0, 0), (0, 0)))
    vs = jnp.pad(nsa_all[:, :, 3], ((0, 0), (0, pad), (0, 0), (0, 0)))
    if past is None:
        def sel_body(xs):
            qb, ib, vb, pb = xs
            return nsa_sel_attend(qb, pb, ks, vs, ib, vb, slopes)
        o_sel = unblock(lax.map(sel_body, (seq_blocks(nq), seq_blocks(idx), seq_blocks(valid), q_pos.reshape(-1, Q_BLOCK))))
    else:
        o_sel = nsa_sel_attend(nq, q_pos, ks, vs, idx, valid, slopes)

    ng = jax.nn.sigmoid(nsa_g.reshape(b, t, 3, NSA_HEADS))[..., None]
    o_nsa = ng[:, :, 0] * o_cmp + ng[:, :, 1] * o_sel + ng[:, :, 2] * o_win

    branches = jnp.stack([o_fox.reshape(b, t, FOX_WIDTH), o_gm, o_nsa.reshape(b, t, NSA_WIDTH)], axis=2)
    proj = jnp.einsum('btiw,iwd->btid', branches, lw['w_branch'])
    gates = jax.nn.sigmoid(merge_g.reshape(b, t, N_BRANCH, D_MODEL))
    y = (gates * proj).sum(axis=2) @ lw['w_out']
    return y, (new_fox_kv, logf, new_nsa_kv, win_state, gm_state)


def layer(x, lw, past):
    x = x + 0.5 * swiglu(rmsnorm(x, lw['ffn1_norm']), lw['ffn1_w_gu'], lw['ffn1_w_down'])
    y, st = token_mixer(rmsnorm(x, lw['mix_norm']), lw, past)
    x = x + y
    x = x + 0.5 * swiglu(rmsnorm(x, lw['ffn2_norm']), lw['ffn2_w_gu'], lw['ffn2_w_down'])
    return x, st


def setup_inputs(seed: int = 0) -> dict:
    key = jax.random.key(seed)
    ks = jax.random.split(key, 32)
    n_pages = PAST_LEN // PAGE_SIZE
    n_used = DEC_BATCH * n_pages
    n_pool = n_used + n_used // 4
    w_buf = min(WINDOW, PAST_LEN)
    f32 = jnp.float32

    def nrm(k, shape, scale=1.0):
        return jax.random.normal(k, shape, f32) * scale

    page_table = jax.random.permutation(ks[2], n_pool)[:n_used].reshape(DEC_BATCH, n_pages).astype(jnp.int32)
    return {
        'x_prompt': nrm(ks[0], (BATCH, SEQ, D_MODEL)),
        'x_sample': nrm(ks[1], (DEC_BATCH, DEC_SEQ, D_MODEL)),
        'page_table': page_table,
        'cache_fox_kv': nrm(ks[3], (DEPTH, n_pool, PAGE_SIZE, 2, FOX_HEADS, HEAD_DIM)),
        'cache_fox_logf': jax.nn.log_sigmoid(3.0 + nrm(ks[4], (DEPTH, n_pool, PAGE_SIZE, FOX_HEADS))),
        'cache_nsa_kv': nrm(ks[5], (DEPTH, n_pool, PAGE_SIZE, 4, NSA_KV_GROUPS, HEAD_DIM)),
        'state_nsa_win': nrm(ks[6], (DEPTH, DEC_BATCH, w_buf, 2, NSA_KV_GROUPS, HEAD_DIM)),
        'ffn1_norm': 1.0 + nrm(ks[7], (DEPTH, D_MODEL), 0.05),
        'ffn1_w_gu': nrm(ks[8], (DEPTH, D_MODEL, 2 * D_FF), D_MODEL ** -0.5),
        'ffn1_w_down': nrm(ks[9], (DEPTH, D_FF, D_MODEL), D_FF ** -0.5),
        'mix_norm': 1.0 + nrm(ks[10], (DEPTH, D_MODEL), 0.05),
        'w_in': nrm(ks[11], (DEPTH, D_MODEL, D_IN), D_MODEL ** -0.5),
        'fox_b_f': 3.0 + nrm(ks[12], (DEPTH, FOX_HEADS), 0.3),
        'fox_qk_gain': 1.0 + nrm(ks[13], (DEPTH, 2, HEAD_DIM), 0.05),
        'gmlp_v_gain': 1.0 + nrm(ks[14], (DEPTH, GMLP_WIDTH), 0.05),
        'gmlp_w_s': nrm(ks[15], (DEPTH, GMLP_GROUPS, CHUNK, CHUNK), CHUNK ** -0.5),
        'gmlp_b_s': 1.0 + nrm(ks[16], (DEPTH, GMLP_GROUPS, CHUNK), 0.1),
        'nsa_q_gain': 1.0 + nrm(ks[17], (DEPTH, HEAD_DIM), 0.05),
        'nsa_k_gain': 1.0 + nrm(ks[18], (DEPTH, 3, HEAD_DIM), 0.05),
        'nsa_cmp_pe': nrm(ks[19], (DEPTH, 2, CMP_LEN, HEAD_DIM), 0.1),
        'nsa_cmp_w1': nrm(ks[20], (DEPTH, 2, CMP_LEN * HEAD_DIM, HEAD_DIM), (CMP_LEN * HEAD_DIM) ** -0.5),
        'nsa_cmp_w2': nrm(ks[21], (DEPTH, 2, HEAD_DIM, HEAD_DIM), HEAD_DIM ** -0.5),
        'w_branch': nrm(ks[22], (DEPTH, N_BRANCH, BRANCH_WIDTH, D_MODEL), BRANCH_WIDTH ** -0.5),
        'w_out': nrm(ks[23], (DEPTH, D_MODEL, D_MODEL), D_MODEL ** -0.5),
        'ffn2_norm': 1.0 + nrm(ks[24], (DEPTH, D_MODEL), 0.05),
        'ffn2_w_gu': nrm(ks[25], (DEPTH, D_MODEL, 2 * D_FF), D_MODEL ** -0.5),
        'ffn2_w_down': nrm(ks[26], (DEPTH, D_FF, D_MODEL), D_FF ** -0.5),
    }


def reference(x_prompt, x_sample, page_table, cache_fox_kv, cache_fox_logf, cache_nsa_kv, state_nsa_win,
              ffn1_norm, ffn1_w_gu, ffn1_w_down, mix_norm, w_in, fox_b_f, fox_qk_gain, gmlp_v_gain,
              gmlp_w_s, gmlp_b_s, nsa_q_gain, nsa_k_gain, nsa_cmp_pe, nsa_cmp_w1, nsa_cmp_w2,
              w_branch, w_out, ffn2_norm, ffn2_w_gu, ffn2_w_down):
    xp = x_prompt
    xs = x_sample
    st_p = [[], [], [], [], []]
    st_s = [[], [], [], [], []]
    for l in range(DEPTH):
        lw = {
            'ffn1_norm': ffn1_norm[l], 'ffn1_w_gu': ffn1_w_gu[l], 'ffn1_w_down': ffn1_w_down[l],
            'mix_norm': mix_norm[l], 'w_in': w_in[l], 'fox_b_f': fox_b_f[l], 'fox_qk_gain': fox_qk_gain[l],
            'gmlp_v_gain': gmlp_v_gain[l], 'gmlp_w_s': gmlp_w_s[l], 'gmlp_b_s': gmlp_b_s[l],
            'nsa_q_gain': nsa_q_gain[l], 'nsa_k_gain': nsa_k_gain[l], 'nsa_cmp_pe': nsa_cmp_pe[l],
            'nsa_cmp_w1': nsa_cmp_w1[l], 'nsa_cmp_w2': nsa_cmp_w2[l], 'w_branch': w_branch[l], 'w_out': w_out[l],
            'ffn2_norm': ffn2_norm[l], 'ffn2_w_gu': ffn2_w_gu[l], 'ffn2_w_down': ffn2_w_down[l],
        }
        past = {
            'fox_kv': gather_pages(cache_fox_kv[l], page_table),
            'fox_logf': gather_pages(cache_fox_logf[l], page_table),
            'nsa_kv': gather_pages(cache_nsa_kv[l], page_table),
            'win': state_nsa_win[l],
        }
        xp, sp = layer(xp, lw, None)
        xs, ss = layer(xs, lw, past)
        for i in range(5):
            st_p[i].append(sp[i])
            st_s[i].append(ss[i])
    fox_kv_p = jnp.stack(st_p[0])
    fox_kv_s = jnp.stack(st_s[0])
    fox_logf_p = jnp.stack(st_p[1])
    fox_logf_s = jnp.stack(st_s[1])
    nsa_kv_p = jnp.stack(st_p[2])
    nsa_kv_s = jnp.stack(st_s[2])
    win_p = jnp.stack(st_p[3])
    win_s = jnp.stack(st_s[3])
    gmlp_v_p = jnp.stack(st_p[4])
    gmlp_v_s = jnp.stack(st_s[4])
    return (xp, xs, fox_kv_p, fox_kv_s, fox_logf_p, fox_logf_s, nsa_kv_p, nsa_kv_s, win_p, win_s, gmlp_v_p, gmlp_v_s)
```

```python
import functools

import jax
import jax.numpy as jnp
import numpy as np
from jax import lax
from jax.experimental import pallas as pl
from jax.experimental.pallas import tpu as pltpu

F32 = jnp.float32
BF16 = jnp.bfloat16

HEAD_DIM = 64
LANES = 128
PAGE = 128
CHUNK = 128
Q_BLOCK = 128
FOX_BLOCK = 256
CMP_LEN = 32
CMP_STRIDE = 16
SEL_BLOCK = 64
N_SEL = 8
WINDOW = 512
N_GROUPS = 2
N_REP = 4
N_HEADS = 8
FORCE_BONUS = 1.0e4
NEG = -1.0e30
EPS = 1e-6
SCALE = HEAD_DIM ** -0.5
VMEM_LIMIT = 56 * 1024 * 1024
HI = lax.Precision.HIGHEST


def _cparams(*sem):
    return pltpu.CompilerParams(dimension_semantics=sem, vmem_limit_bytes=VMEM_LIMIT)


def _dot(a, b, precision=None):
    return jnp.dot(a, b, preferred_element_type=F32, precision=precision)


def _dot_nt(a, b):
    return lax.dot_general(a, b, (((1,), (1,)), ((), ())), preferred_element_type=F32)


def _shr(x, pow2):
    return jnp.right_shift(x, int(pow2).bit_length() - 1)


def _rms_rows(x, g):
    ms = jnp.mean(x * x, axis=-1, keepdims=True)
    return x * lax.rsqrt(ms + EPS) * g


def _headnorm_pair(zb, gain):
    lo = lax.broadcasted_iota(jnp.int32, zb.shape, 1) < HEAD_DIM
    sq = zb * zb
    s_lo = jnp.sum(jnp.where(lo, sq, 0.0), axis=-1, keepdims=True)
    s_hi = jnp.sum(jnp.where(lo, 0.0, sq), axis=-1, keepdims=True)
    ms = jnp.where(lo, s_lo, s_hi) * (1.0 / HEAD_DIM)
    return zb * lax.rsqrt(ms + EPS) * gain


def _silu(x):
    return x * jax.nn.sigmoid(x)


def _gelu_tanh(x):
    return 0.5 * x * (1.0 + jnp.tanh(np.sqrt(2.0 / np.pi) * (x + 0.044715 * (x * x * x))))


def _log_sigmoid(x):
    return jnp.minimum(x, 0.0) - jnp.log1p(jnp.exp(-jnp.abs(x)))


def _token_tile(t):
    for tm in (512, 384, 256, 128):
        if t % tm == 0:
            return tm
    raise ValueError(f"token count {t} is not a multiple of 128")


def _const_spec(shape):
    nd = len(shape)
    return pl.BlockSpec(shape, lambda *_: (0,) * nd, pipeline_mode=pl.Buffered(1))


def _ffn_body(x_ref, g_ref, wgu_ref, wd_ref, o_ref, *, d_ff, fc):
    x = x_ref[...]
    h = _rms_rows(x, g_ref[...]).astype(BF16)
    acc = jnp.zeros_like(x)
    for c in range(d_ff // fc):
        g = _dot(h, wgu_ref[:, c * fc:(c + 1) * fc])
        u = _dot(h, wgu_ref[:, d_ff + c * fc:d_ff + (c + 1) * fc])
        a = (_silu(g) * u).astype(BF16)
        acc = acc + _dot(a, wd_ref[c * fc:(c + 1) * fc, :])
    o_ref[...] = x + 0.5 * acc


def _ffn(x, g, wgu, wd):
    t, d = x.shape
    d_ff = wd.shape[0]
    tm = _token_tile(t)
    fc = 256 if d_ff % 256 == 0 else 128
    return pl.pallas_call(
        functools.partial(_ffn_body, d_ff=d_ff, fc=fc),
        grid=(t // tm,),
        in_specs=[pl.BlockSpec((tm, d), lambda i: (i, 0)),
                  _const_spec((1, d)),
                  _const_spec((d, 2 * d_ff)),
                  _const_spec((d_ff, d))],
        out_specs=pl.BlockSpec((tm, d), lambda i: (i, 0)),
        out_shape=jax.ShapeDtypeStruct((t, d), F32),
        compiler_params=_cparams("parallel"),
        name="ffn",
    )(x, g, wgu, wd)


C_FOX = 0
C_GM = 1536
C_NQ = 2560
C_NKV = 3584
C_SMALL = 4352
C_MG = 4480
C_END = 7552


def _proj_body(x_ref, g_ref, w_ref, gq_ref, gk_ref, gv_ref, gnq_ref, gk1_ref, gk2_ref, bf_ref,
               fq_ref, foxkv_ref, foxkvb_ref, gmu_ref, gmv_ref, nq_ref, nsakv_ref, win_ref, nsab_ref,
               small_ref, mg_ref):
    h = _rms_rows(x_ref[...], g_ref[...]).astype(BF16)

    z = _dot(h, w_ref[:, C_FOX:C_GM])
    for j in range(4):
        sl = slice(j * LANES, (j + 1) * LANES)
        fq_ref[:, sl] = _headnorm_pair(z[:, sl], gq_ref[...]).astype(BF16)
        kb = _headnorm_pair(z[:, 512 + j * LANES:512 + (j + 1) * LANES], gk_ref[...])
        foxkv_ref[:, sl] = kb
        foxkvb_ref[:, sl] = kb.astype(BF16)
    v = z[:, 1024:1536]
    foxkv_ref[:, 512:1024] = v
    foxkvb_ref[:, 512:1024] = v.astype(BF16)

    a = _gelu_tanh(_dot(h, w_ref[:, C_GM:C_NQ]))
    gmu_ref[...] = a[:, 0:512]
    gmv_ref[...] = _rms_rows(a[:, 512:1024], gv_ref[...])

    z = _dot(h, w_ref[:, C_NQ:C_NKV])
    for hh in range(N_HEADS):
        sl = slice(hh * LANES, (hh + 1) * LANES)
        zb = z[:, sl]
        ms = jnp.sum(zb * zb, axis=-1, keepdims=True) * (1.0 / HEAD_DIM)
        nq_ref[:, sl] = (zb * lax.rsqrt(ms + EPS) * gnq_ref[:, sl]).astype(BF16)

    z = _dot(h, w_ref[:, C_NKV:C_SMALL])
    selk = _headnorm_pair(z[:, 256:384], gk1_ref[...])
    selv = z[:, 384:512]
    wink = _headnorm_pair(z[:, 512:640], gk2_ref[...])
    winv = z[:, 640:768]
    nsakv_ref[:, 0:256] = z[:, 0:256]
    nsakv_ref[:, 256:384] = selk
    nsakv_ref[:, 384:512] = selv
    win_ref[:, 0:128] = wink
    win_ref[:, 128:256] = winv
    nsab_ref[:, 0:128] = selk.astype(BF16)
    nsab_ref[:, 128:256] = selv.astype(BF16)
    nsab_ref[:, 256:384] = wink.astype(BF16)
    nsab_ref[:, 384:512] = winv.astype(BF16)

    z = _dot(h, w_ref[:, C_SMALL:C_MG])
    lane = lax.broadcasted_iota(jnp.int32, z.shape, 1)
    small_ref[...] = jnp.where(lane < N_HEADS, _log_sigmoid(z + bf_ref[...]), jax.nn.sigmoid(z))

    mg_ref[...] = jax.nn.sigmoid(_dot(h, w_ref[:, C_MG:C_END]))


def _proj(x, g, w, gq, gk, gv, gnq, gk1, gk2, bf):
    t, d = x.shape
    tm = _token_tile(t)
    widths = [(512, BF16), (1024, F32), (1024, BF16), (512, F32), (512, F32), (1024, BF16), (512, F32),
              (256, F32), (512, BF16), (128, F32), (3072, F32)]
    small = [gq, gk, gv, gnq, gk1, gk2, bf]
    return pl.pallas_call(
        _proj_body,
        grid=(t // tm,),
        in_specs=[pl.BlockSpec((tm, d), lambda i: (i, 0)), _const_spec((1, d)), _const_spec(w.shape)]
                 + [_const_spec(a.shape) for a in small],
        out_specs=[pl.BlockSpec((tm, n), lambda i: (i, 0)) for n, _ in widths],
        out_shape=[jax.ShapeDtypeStruct((t, n), dt) for n, dt in widths],
        compiler_params=_cparams("parallel"),
        name="in_proj",
    )(x, g, w, *small)


def _cumsum_body(x_ref, o_ref, *, blk):
    s = x_ref.shape[0]
    r = lax.broadcasted_iota(jnp.int32, (blk, blk), 0)
    c = lax.broadcasted_iota(jnp.int32, (blk, blk), 1)
    tri = (r >= c).astype(F32)
    carry = jnp.zeros((1, x_ref.shape[1]), F32)
    for i in range(s // blk):
        cs = _dot(tri, x_ref[i * blk:(i + 1) * blk, :], precision=HI) + carry
        o_ref[i * blk:(i + 1) * blk, :] = cs
        carry = cs[blk - 1:blk, :]


def _cumsum(x):
    b, s, n = x.shape
    return pl.pallas_call(
        functools.partial(_cumsum_body, blk=min(256, s)),
        grid=(b,),
        in_specs=[pl.BlockSpec((None, s, n), lambda i: (i, 0, 0))],
        out_specs=pl.BlockSpec((None, s, n), lambda i: (i, 0, 0)),
        out_shape=jax.ShapeDtypeStruct((b, s, n), F32),
        compiler_params=_cparams("parallel"),
        name="logf_cumsum",
    )(x)


def _online_update(s, v, m_ref, l_ref, acc_ref, col):
    m_old = m_ref[:, col:col + 1]
    m_new = jnp.maximum(m_old, jnp.max(s, axis=-1, keepdims=True))
    alpha = jnp.exp(m_old - m_new)
    p = jnp.exp(s - m_new)
    l_ref[:, col:col + 1] = alpha * l_ref[:, col:col + 1] + jnp.sum(p, axis=-1, keepdims=True)
    m_ref[:, col:col + 1] = m_new
    return alpha, _dot(p.astype(BF16), v)


def _fox_body(q_ref, k_ref, v_ref, cq_ref, ck_ref, o_ref, m_ref, l_ref, acc_ref, *, tq):
    qi = pl.program_id(2)
    ki = pl.program_id(3)

    @pl.when(ki == 0)
    def _():
        m_ref[...] = jnp.full(m_ref.shape, NEG, F32)
        l_ref[...] = jnp.zeros(l_ref.shape, F32)
        acc_ref[...] = jnp.zeros(acc_ref.shape, F32)

    @pl.when(ki <= qi)
    def _():
        q = q_ref[...]
        k = k_ref[...]
        v = v_ref[...]
        lo = lax.broadcasted_iota(jnp.int32, q.shape, 1) < HEAD_DIM
        qpos = qi * tq + lax.broadcasted_iota(jnp.int32, (tq, tq), 0)
        kpos = ki * tq + lax.broadcasted_iota(jnp.int32, (tq, tq), 1)
        mask = kpos <= qpos
        zero = jnp.zeros_like(q)
        outs = []
        for hh in range(2):
            qh = jnp.where(lo, q, zero) if hh == 0 else jnp.where(lo, zero, q)
            s = _dot_nt(qh, k) * SCALE + (cq_ref[:, hh:hh + 1] - ck_ref[hh:hh + 1, :])
            s = jnp.where(mask, s, NEG)
            outs.append(_online_update(s, v, m_ref, l_ref, acc_ref, hh))
        lo_f = lax.broadcasted_iota(jnp.int32, acc_ref.shape, 1) < HEAD_DIM
        alpha = jnp.where(lo_f, outs[0][0], outs[1][0])
        acc_ref[...] = alpha * acc_ref[...] + jnp.where(lo_f, outs[0][1], outs[1][1])

    @pl.when(ki == pl.num_programs(3) - 1)
    def _():
        lo_f = lax.broadcasted_iota(jnp.int32, acc_ref.shape, 1) < HEAD_DIM
        l = jnp.where(lo_f, l_ref[:, 0:1], l_ref[:, 1:2])
        o_ref[...] = (acc_ref[...] / l).astype(o_ref.dtype)


def _fox_prompt(fq, foxkvb, cq, ck):
    b, s, _ = fq.shape
    tq = min(FOX_BLOCK, s)
    nb = s // tq
    return pl.pallas_call(
        functools.partial(_fox_body, tq=tq),
        grid=(b, 4, nb, nb),
        in_specs=[pl.BlockSpec((None, tq, LANES), lambda bi, p, qi, ki: (bi, qi, p)),
                  pl.BlockSpec((None, tq, LANES), lambda bi, p, qi, ki: (bi, jnp.minimum(ki, qi), p)),
                  pl.BlockSpec((None, tq, LANES), lambda bi, p, qi, ki: (bi, jnp.minimum(ki, qi), 4 + p)),
                  pl.BlockSpec((None, None, tq, 2), lambda bi, p, qi, ki: (bi, p, qi, 0)),
                  pl.BlockSpec((None, None, 2, tq), lambda bi, p, qi, ki: (bi, p, 0, jnp.minimum(ki, qi)))],
        out_specs=pl.BlockSpec((None, tq, LANES), lambda bi, p, qi, ki: (bi, qi, p)),
        out_shape=jax.ShapeDtypeStruct((b, s, 512), BF16),
        scratch_shapes=[pltpu.VMEM((tq, LANES), F32), pltpu.VMEM((tq, LANES), F32),
                        pltpu.VMEM((tq, LANES), F32)],
        compiler_params=_cparams("parallel", "parallel", "parallel", "arbitrary"),
        name="fox_prompt",
    )(fq, foxkvb, foxkvb, cq, ck)


def _gmlp_body(u_ref, v_ref, ws_ref, b_ref, o_ref):
    n = ws_ref.shape[1]
    r = lax.broadcasted_iota(jnp.int32, (n, n), 0)
    c = lax.broadcasted_iota(jnp.int32, (n, n), 1)
    lo = lax.broadcasted_iota(jnp.int32, (n, LANES), 1) < HEAD_DIM
    for j in range(4):
        sl = slice(j * LANES, (j + 1) * LANES)
        vp = v_ref[:, sl].astype(BF16)
        w0 = jnp.where(r >= c, ws_ref[2 * j], 0.0).astype(BF16)
        w1 = jnp.where(r >= c, ws_ref[2 * j + 1], 0.0).astype(BF16)
        mixed = jnp.where(lo, _dot(w0, vp), _dot(w1, vp)) + b_ref[:, sl]
        o_ref[:, sl] = (u_ref[:, sl] * mixed).astype(o_ref.dtype)


def _gmlp_prompt(gmu, gmv, ws, bexp, n_chunks):
    return pl.pallas_call(
        _gmlp_body,
        grid=(n_chunks,),
        in_specs=[pl.BlockSpec((CHUNK, 512), lambda i: (i, 0)),
                  pl.BlockSpec((CHUNK, 512), lambda i: (i, 0)),
                  _const_spec(ws.shape), _const_spec(bexp.shape)],
        out_specs=pl.BlockSpec((CHUNK, 512), lambda i: (i, 0)),
        out_shape=jax.ShapeDtypeStruct((n_chunks * CHUNK, 512), BF16),
        compiler_params=_cparams("parallel"),
        name="gmlp_prompt",
    )(gmu, gmv, ws, bexp)


def _gmlp_first_body(u_ref, v_ref, w_ref, b_ref, o_ref):
    o_ref[...] = (u_ref[...] * (v_ref[...] * w_ref[...] + b_ref[...])).astype(o_ref.dtype)


def _gmlp_sample(gmu, gmv, wrow, brow, row0, n):
    blk = row0 // n
    return pl.pallas_call(
        _gmlp_first_body,
        grid=(1,),
        in_specs=[pl.BlockSpec((n, 512), lambda i: (blk, 0)), pl.BlockSpec((n, 512), lambda i: (blk, 0)),
                  _const_spec((1, 512)), _const_spec((1, 512))],
        out_specs=pl.BlockSpec((n, 512), lambda i: (0, 0)),
        out_shape=jax.ShapeDtypeStruct((n, 512), BF16),
        compiler_params=_cparams("arbitrary"),
        name="gmlp_sample",
    )(gmu, gmv, wrow, brow)


def _compress(xk_ref, xv_ref, wa_ref, wb_ref, pea_ref, peb_ref, w2_ref, kg0_ref):
    nhb = xk_ref.shape[0] // CMP_STRIDE
    a = jnp.zeros((nhb, 256), F32)
    b = jnp.zeros((nhb, 256), F32)
    for l in range(CMP_STRIDE):
        xl = jnp.concatenate([xk_ref[pl.ds(l, nhb, stride=CMP_STRIDE), :],
                              xv_ref[pl.ds(l, nhb, stride=CMP_STRIDE), :]], axis=1)
        a = a + _dot((xl + pea_ref[l]).astype(BF16), wa_ref[l])
        b = b + _dot((xl + peb_ref[l]).astype(BF16), wb_ref[l])
    pre = a + pltpu.roll(b, nhb - 1, 0)
    out = _dot(_silu(pre).astype(BF16), w2_ref[...])
    kc = _headnorm_pair(out[:, 0:LANES], kg0_ref[...])
    return kc, out[:, LANES:2 * LANES]


def _compress_body(xk_ref, xv_ref, wa_ref, wb_ref, pea_ref, peb_ref, w2_ref, kg0_ref, o_ref):
    kc, vc = _compress(xk_ref, xv_ref, wa_ref, wb_ref, pea_ref, peb_ref, w2_ref, kg0_ref)
    o_ref[:, 0:LANES] = kc.astype(BF16)
    o_ref[:, LANES:2 * LANES] = vc.astype(BF16)


def _compress_prompt(nsakv, cw):
    b, s, _ = nsakv.shape
    nhb = s // CMP_STRIDE
    return pl.pallas_call(
        _compress_body,
        grid=(b,),
        in_specs=[pl.BlockSpec((None, s, LANES), lambda i: (i, 0, 0)),
                  pl.BlockSpec((None, s, LANES), lambda i: (i, 0, 1))] + [_const_spec(a.shape) for a in cw],
        out_specs=pl.BlockSpec((None, nhb, 256), lambda i: (i, 0, 0)),
        out_shape=jax.ShapeDtypeStruct((b, nhb, 256), BF16),
        compiler_params=_cparams("parallel"),
        name="nsa_compress",
    )(nsakv, nsakv, *cw)


def _slope_col(head):
    out = jnp.zeros(head.shape, F32)
    for hh in range(N_HEADS):
        out = jnp.where(head == hh, 2.0 ** (-(hh + 1)), out)
    return out


def _cmp_softmax(s, mask):
    s = jnp.where(mask, s, NEG)
    m = jnp.max(s, axis=-1, keepdims=True)
    e = jnp.where(mask, jnp.exp(s - m), 0.0)
    l = jnp.sum(e, axis=-1, keepdims=True)
    return e / jnp.where(l > 0.0, l, 1.0)


def _select_blocks(imp, cur, n_blk):
    m, n = imp.shape
    pool = (_shr(lax.broadcasted_iota(jnp.int32, (n, LANES), 0), SEL_BLOCK // CMP_STRIDE)
            == lax.broadcasted_iota(jnp.int32, (n, LANES), 1)).astype(F32)
    score = _dot(imp, pool, precision=HI)
    j = lax.broadcasted_iota(jnp.int32, (m, LANES), 1)
    forced = (j == 0) | (j == cur) | (j == cur - 1)
    score = jnp.where(forced, score + FORCE_BONUS, score)
    score = jnp.where(j > cur, NEG, score)
    rank = jnp.zeros((m, LANES), F32)
    for i in range(n_blk):
        ci = score[:, i:i + 1]
        ahead = (ci > score) | ((ci == score) & (j > i))
        rank = rank + ahead.astype(F32)
    return (rank < float(min(N_SEL, n_blk))) & (score > 0.5 * NEG)


def _expand_sel(sel_bf, t0):
    jj = lax.broadcasted_iota(jnp.int32, (LANES, LANES), 0)
    tt = t0 + lax.broadcasted_iota(jnp.int32, (LANES, LANES), 1)
    e = (_shr(tt, SEL_BLOCK) == jj).astype(BF16)
    return _dot(sel_bf, e)


def _nsa_body(nq_ref, cmp_ref, kv_ref, gate_ref, o_ref, m_ref, l_ref, acc_ref, *, n_blk):
    qi = pl.program_id(1)
    rows = N_REP * Q_BLOCK
    row = lax.broadcasted_iota(jnp.int32, (rows, 1), 0)
    qpos = qi * Q_BLOCK + (row & (Q_BLOCK - 1))
    qpos_q = qi * Q_BLOCK + lax.broadcasted_iota(jnp.int32, (Q_BLOCK, 1), 0)
    lane = lax.broadcasted_iota(jnp.int32, (1, LANES), 1)
    nhb = cmp_ref.shape[0]
    end = lax.broadcasted_iota(jnp.int32, (1, nhb), 1) * CMP_STRIDE + (CMP_LEN - 1)

    def reset():
        m_ref[...] = jnp.full(m_ref.shape, NEG, F32)
        l_ref[...] = jnp.zeros(l_ref.shape, F32)
        acc_ref[...] = jnp.zeros(acc_ref.shape, F32)

    def step(s, v):
        alpha, pv = _online_update(s, v, m_ref, l_ref, acc_ref, 0)
        acc_ref[...] = alpha * acc_ref[...] + pv

    for g in range(N_GROUPS):
        q = jnp.concatenate([nq_ref[:, (N_REP * g + r) * LANES:(N_REP * g + r + 1) * LANES]
                             for r in range(N_REP)], axis=0)
        slope = _slope_col(N_REP * g + _shr(row, Q_BLOCK))

        s = _dot_nt(q, cmp_ref[:, 0:LANES]) * SCALE - slope * (qpos - end).astype(F32)
        p = _cmp_softmax(s, end <= qpos)
        o_cmp = _dot(p.astype(BF16), cmp_ref[:, LANES:2 * LANES])
        imp = p[0:Q_BLOCK]
        for r in range(1, N_REP):
            imp = imp + p[r * Q_BLOCK:(r + 1) * Q_BLOCK]
        sel = _select_blocks(imp, _shr(qpos_q, SEL_BLOCK), n_blk).astype(BF16)

        reset()

        def sel_step(kb, carry):
            t0 = pl.multiple_of(kb * Q_BLOCK, Q_BLOCK)
            k = kv_ref[pl.ds(t0, Q_BLOCK), 0:LANES]
            v = kv_ref[pl.ds(t0, Q_BLOCK), LANES:2 * LANES]
            kpos = t0 + lane
            s = _dot_nt(q, k) * SCALE - slope * (qpos - kpos).astype(F32)
            mk = _expand_sel(sel, t0)
            mk = (jnp.concatenate([mk] * N_REP, axis=0) > 0.5) & (kpos <= qpos)
            step(jnp.where(mk, s, NEG), v)
            return carry

        lax.fori_loop(0, qi + 1, sel_step, 0)
        o_sel = acc_ref[...] / l_ref[:, 0:1]

        reset()

        def win_step(kb, carry):
            t0 = pl.multiple_of(kb * Q_BLOCK, Q_BLOCK)
            k = kv_ref[pl.ds(t0, Q_BLOCK), 2 * LANES:3 * LANES]
            v = kv_ref[pl.ds(t0, Q_BLOCK), 3 * LANES:4 * LANES]
            d = qpos - (t0 + lane)
            s = _dot_nt(q, k) * SCALE - slope * d.astype(F32)
            step(jnp.where((d >= 0) & (d < WINDOW), s, NEG), v)
            return carry

        lax.fori_loop(jnp.maximum(qi - WINDOW // Q_BLOCK, 0), qi + 1, win_step, 0)
        o_win = acc_ref[...] / l_ref[:, 0:1]

        in_group = _shr(lane, HEAD_DIM) == g
        for r in range(N_REP):
            hh = N_REP * g + r
            rs = slice(r * Q_BLOCK, (r + 1) * Q_BLOCK)
            o = (gate_ref[:, 8 + hh:9 + hh] * o_cmp[rs] + gate_ref[:, 16 + hh:17 + hh] * o_sel[rs]
                 + gate_ref[:, 24 + hh:25 + hh] * o_win[rs])
            o_ref[:, hh * LANES:(hh + 1) * LANES] = jnp.where(in_group, o, 0.0).astype(o_ref.dtype)


def _nsa_prompt(nq, cmpkv, nsab, small):
    b, s, _ = nq.shape
    nhb = cmpkv.shape[1]
    rows = N_REP * Q_BLOCK
    return pl.pallas_call(
        functools.partial(_nsa_body, n_blk=s // SEL_BLOCK),
        grid=(b, s // Q_BLOCK),
        in_specs=[pl.BlockSpec((None, Q_BLOCK, 1024), lambda bi, qi: (bi, qi, 0)),
                  pl.BlockSpec((None, nhb, 256), lambda bi, qi: (bi, 0, 0)),
                  pl.BlockSpec((None, s, 512), lambda bi, qi: (bi, 0, 0)),
                  pl.BlockSpec((None, Q_BLOCK, LANES), lambda bi, qi: (bi, qi, 0))],
        out_specs=pl.BlockSpec((None, Q_BLOCK, 1024), lambda bi, qi: (bi, qi, 0)),
        out_shape=jax.ShapeDtypeStruct((b, s, 1024), BF16),
        scratch_shapes=[pltpu.VMEM((rows, LANES), F32), pltpu.VMEM((rows, LANES), F32),
                        pltpu.VMEM((rows, LANES), F32)],
        compiler_params=_cparams("parallel", "arbitrary"),
        name="nsa_prompt",
    )(nq, cmpkv, nsab, small)


def _merge_body(x_ref, of_ref, og_ref, on_ref, mg_ref, wb0_ref, wb1_ref, wb2_ref, wo_ref, o_ref):
    d = x_ref.shape[1]
    y = mg_ref[:, 0:d] * _dot(of_ref[...], wb0_ref[...])
    y = y + mg_ref[:, d:2 * d] * _dot(og_ref[...], wb1_ref[...])
    y = y + mg_ref[:, 2 * d:3 * d] * _dot(on_ref[...], wb2_ref[...])
    o_ref[...] = x_ref[...] + _dot(y.astype(BF16), wo_ref[...])


def _merge(x, o_fox, o_gm, o_nsa, mg, wb0, wb1, wb2, wo):
    t, d = x.shape
    tm = _token_tile(t)
    row = lambda n: pl.BlockSpec((tm, n), lambda i: (i, 0))
    return pl.pallas_call(
        _merge_body,
        grid=(t // tm,),
        in_specs=[row(d), row(512), row(512), row(1024), row(3 * d),
                  _const_spec(wb0.shape), _const_spec(wb1.shape), _const_spec(wb2.shape), _const_spec(wo.shape)],
        out_specs=row(d),
        out_shape=jax.ShapeDtypeStruct((t, d), F32),
        compiler_params=_cparams("parallel"),
        name="merge",
    )(x, o_fox, o_gm, o_nsa, mg, wb0, wb1, wb2, wo)


def _fox_decode_body(pt_ref, *refs, n_pages):
    kv = refs[:n_pages]
    lf = refs[n_pages:2 * n_pages]
    q_ref, new_ref, lfn_ref, o_ref, s_ref = refs[2 * n_pages:]
    w = N_HEADS * HEAD_DIM
    row = lax.broadcasted_iota(jnp.int32, (N_HEADS, w), 0)
    lane = lax.broadcasted_iota(jnp.int32, (N_HEADS, w), 1)
    diag = _shr(lane, HEAD_DIM) == row
    qbd = jnp.where(diag, jnp.broadcast_to(q_ref[...].astype(F32), (N_HEADS, w)), 0.0).astype(BF16)
    upper = (lax.broadcasted_iota(jnp.int32, (PAGE, PAGE), 0)
             > lax.broadcasted_iota(jnp.int32, (PAGE, PAGE), 1)).astype(F32)

    later = lfn_ref[...]
    for i in reversed(range(n_pages)):
        s = _dot_nt(qbd, kv[i][:, 0:w].astype(BF16)) * SCALE
        s_ref[:, i * PAGE:(i + 1) * PAGE] = s + _dot(lf[i][...], upper, precision=HI) + later
        later = later + jnp.sum(lf[i][...], axis=-1, keepdims=True)

    k_new = new_ref[:, 0:w].astype(BF16).astype(F32)
    v_new = new_ref[:, w:2 * w].astype(BF16).astype(F32)
    s_new = jnp.sum(qbd.astype(F32) * k_new, axis=-1, keepdims=True) * SCALE
    s_all = s_ref[...]
    m = jnp.maximum(jnp.max(s_all, axis=-1, keepdims=True), s_new)
    p_new = jnp.exp(s_new - m)
    l = p_new
    o = p_new.astype(BF16).astype(F32) * v_new
    for i in range(n_pages):
        p = jnp.exp(s_ref[:, i * PAGE:(i + 1) * PAGE] - m)
        l = l + jnp.sum(p, axis=-1, keepdims=True)
        o = o + _dot(p.astype(BF16), kv[i][:, w:2 * w].astype(BF16))
    o_ref[...] = jnp.sum(jnp.where(diag, o / l, 0.0), axis=0, keepdims=True).astype(o_ref.dtype)


def _fox_decode(page_table, cache_kv, cache_lft, fq_s, foxkv_s, lf_s):
    db, n_pages = page_table.shape

    def page_spec(i, shape):
        return pl.BlockSpec((None,) + shape, lambda b, pt, i=i: (pt[b, i], 0, 0))

    grid_spec = pltpu.PrefetchScalarGridSpec(
        num_scalar_prefetch=1,
        grid=(db,),
        in_specs=[page_spec(i, (PAGE, 1024)) for i in range(n_pages)]
                 + [page_spec(i, (N_HEADS, PAGE)) for i in range(n_pages)]
                 + [pl.BlockSpec((None, 1, 512), lambda b, pt: (b, 0, 0)),
                    pl.BlockSpec((None, 1, 1024), lambda b, pt: (b, 0, 0)),
                    pl.BlockSpec((None, N_HEADS, 1), lambda b, pt: (b, 0, 0))],
        out_specs=pl.BlockSpec((None, 1, 512), lambda b, pt: (b, 0, 0)),
        scratch_shapes=[pltpu.VMEM((N_HEADS, n_pages * PAGE), F32)],
    )
    return pl.pallas_call(
        functools.partial(_fox_decode_body, n_pages=n_pages),
        grid_spec=grid_spec,
        out_shape=jax.ShapeDtypeStruct((db, 1, 512), BF16),
        compiler_params=_cparams("arbitrary"),
        name="fox_decode",
    )(page_table, *([cache_kv] * n_pages), *([cache_lft] * n_pages), fq_s, foxkv_s, lf_s)


def _nsa_decode_body(pt_ref, *refs, n_pages):
    pg = refs[:n_pages]
    (win_ref, q_ref, new_ref, gate_ref, wa_ref, wb_ref, pea_ref, peb_ref, w2_ref, kg0_ref,
     o_ref, x_ref, s_ref) = refs[n_pages:]
    past = n_pages * PAGE
    wbuf = win_ref.shape[0]
    cur = past // SEL_BLOCK
    n_blk = cur + 1

    for i in range(n_pages):
        x_ref[0, i * PAGE:(i + 1) * PAGE, :] = pg[i][:, 0:LANES]
        x_ref[1, i * PAGE:(i + 1) * PAGE, :] = pg[i][:, LANES:2 * LANES]
    kc, vc = _compress(x_ref.at[0], x_ref.at[1], wa_ref, wb_ref, pea_ref, peb_ref, w2_ref, kg0_ref)
    nhb = kc.shape[0]

    q = q_ref[...]
    qf = q.astype(F32)
    row = lax.broadcasted_iota(jnp.int32, (N_HEADS, 1), 0)
    slope = _slope_col(row)
    lane = lax.broadcasted_iota(jnp.int32, (1, LANES), 1)
    g0 = row < N_REP

    end = lax.broadcasted_iota(jnp.int32, (1, nhb), 1) * CMP_STRIDE + (CMP_LEN - 1)
    s = _dot_nt(q, kc.astype(BF16)) * SCALE - slope * (past - end).astype(F32)
    p = _cmp_softmax(s, jnp.broadcast_to(end <= past, s.shape))
    o_cmp = _dot(p.astype(BF16), vc.astype(BF16))
    imp0 = jnp.sum(jnp.where(g0, p, 0.0), axis=0, keepdims=True)
    imp1 = jnp.sum(jnp.where(g0, 0.0, p), axis=0, keepdims=True)
    imp = jnp.where(g0, imp0, imp1)
    sel = _select_blocks(imp, jnp.full((N_HEADS, 1), cur, jnp.int32), n_blk)
    sel_bf = sel.astype(BF16)

    def new_dot(k_row):
        return jnp.sum(qf * k_row.astype(F32), axis=-1, keepdims=True) * SCALE

    for i in range(n_pages):
        kpos = i * PAGE + lane
        s = _dot_nt(q, pg[i][:, 256:384].astype(BF16)) * SCALE - slope * (past - kpos).astype(F32)
        s_ref[:, i * PAGE:(i + 1) * PAGE] = jnp.where(_expand_sel(sel_bf, i * PAGE) > 0.5, s, NEG)
    new_ok = jnp.sum(jnp.where(lane == cur, sel.astype(F32), 0.0), axis=-1, keepdims=True) > 0.5
    s_new = jnp.where(new_ok, new_dot(new_ref[:, 0:LANES]), NEG)
    m = jnp.maximum(jnp.max(s_ref[...], axis=-1, keepdims=True), s_new)
    p_new = jnp.where(new_ok, jnp.exp(s_new - m), 0.0)
    l = p_new
    o_sel = p_new.astype(BF16).astype(F32) * new_ref[:, LANES:2 * LANES].astype(F32)
    for i in range(n_pages):
        sp = s_ref[:, i * PAGE:(i + 1) * PAGE]
        p = jnp.where(sp > 0.5 * NEG, jnp.exp(sp - m), 0.0)
        l = l + jnp.sum(p, axis=-1, keepdims=True)
        o_sel = o_sel + _dot(p.astype(BF16), pg[i][:, 384:512].astype(BF16))
    o_sel = o_sel / jnp.where(l > 0.0, l, 1.0)

    d = wbuf - lax.broadcasted_iota(jnp.int32, (1, wbuf), 1)
    s = _dot_nt(q, win_ref[:, 0:LANES].astype(BF16)) * SCALE - slope * d.astype(F32)
    s = jnp.where(d < WINDOW, s, NEG)
    s_new = new_dot(new_ref[:, 2 * LANES:3 * LANES])
    m = jnp.maximum(jnp.max(s, axis=-1, keepdims=True), s_new)
    p = jnp.exp(s - m)
    p_new = jnp.exp(s_new - m)
    l = jnp.sum(p, axis=-1, keepdims=True) + p_new
    o_win = (_dot(p.astype(BF16), win_ref[:, LANES:2 * LANES].astype(BF16))
             + p_new.astype(BF16).astype(F32) * new_ref[:, 3 * LANES:4 * LANES].astype(F32)) / l

    o = gate_ref[:, 0:1] * o_cmp + gate_ref[:, 1:2] * o_sel + gate_ref[:, 2:3] * o_win
    in_group = _shr(lane, HEAD_DIM) == _shr(row, N_REP)
    o_ref[...] = jnp.where(in_group, o, 0.0).astype(o_ref.dtype)


def _nsa_decode(page_table, cache_nsa, win_state, nq_s, nsab_s, gates_s, cw):
    db, n_pages = page_table.shape
    wbuf = win_state.shape[1]
    grid_spec = pltpu.PrefetchScalarGridSpec(
        num_scalar_prefetch=1,
        grid=(db,),
        in_specs=[pl.BlockSpec((None, PAGE, 512), lambda b, pt, i=i: (pt[b, i], 0, 0)) for i in range(n_pages)]
                 + [pl.BlockSpec((None, wbuf, 256), lambda b, pt: (b, 0, 0)),
                    pl.BlockSpec((None, N_HEADS, LANES), lambda b, pt: (b, 0, 0)),
                    pl.BlockSpec((None, 1, 512), lambda b, pt: (b, 0, 0)),
                    pl.BlockSpec((None, N_HEADS, 3), lambda b, pt: (b, 0, 0))]
                 + [pl.BlockSpec(a.shape, lambda b, pt, nd=a.ndim: (0,) * nd) for a in cw],
        out_specs=pl.BlockSpec((None, N_HEADS, LANES), lambda b, pt: (b, 0, 0)),
        scratch_shapes=[pltpu.VMEM((2, n_pages * PAGE, LANES), F32), pltpu.VMEM((N_HEADS, n_pages * PAGE), F32)],
    )
    return pl.pallas_call(
        functools.partial(_nsa_decode_body, n_pages=n_pages),
        grid_spec=grid_spec,
        out_shape=jax.ShapeDtypeStruct((db, N_HEADS, LANES), BF16),
        compiler_params=_cparams("arbitrary"),
        name="nsa_decode",
    )(page_table, *([cache_nsa] * n_pages), win_state, nq_s, nsab_s, gates_s, *cw)


def _prep_w_in(w):
    d = w.shape[0]
    o_f, o_gm, o_nq, o_nkv, o_ng, o_mg = 1536, 1544, 2568, 3080, 3848, 3872
    wq = w[:, o_nq:o_nkv].reshape(d, N_HEADS, 1, HEAD_DIM)
    place = (np.arange(N_HEADS)[:, None] // N_REP == np.arange(N_GROUPS)[None, :]).astype(np.float32)
    wq = (wq * place[None, :, :, None]).reshape(d, N_HEADS * LANES)
    small = jnp.concatenate([w[:, o_f:o_gm], w[:, o_ng:o_mg], jnp.zeros((d, LANES - 32), w.dtype)], axis=1)
    return jnp.concatenate([w[:, 0:o_f], w[:, o_gm:o_nq], wq, w[:, o_nkv:o_ng], small, w[:, o_mg:]],
                           axis=1).astype(BF16)


def _pad_heads(g):
    place = (np.arange(N_HEADS)[:, None] // N_REP == np.arange(N_GROUPS)[None, :]).astype(np.float32)
    return (place[:, :, None] * g[None, None, :]).reshape(1, N_HEADS * LANES)


def _prep_compress(w1, w2, pe, kg0):
    eye = np.eye(N_GROUPS, dtype=np.float32)
    w1r = w1.reshape(2, 2, CMP_STRIDE, HEAD_DIM, HEAD_DIM)
    wl = jnp.einsum('chldk,cC,gG->hlcgdCGk', w1r, np.eye(2, dtype=np.float32), eye)
    wl = wl.reshape(2, CMP_STRIDE, 256, 256).astype(BF16)
    w2bd = jnp.einsum('cdk,cC,gG->cgdCGk', w2, np.eye(2, dtype=np.float32), eye).reshape(256, 256).astype(BF16)
    pex = jnp.broadcast_to(pe.transpose(1, 0, 2)[:, :, None, :], (CMP_LEN, 2, N_GROUPS, HEAD_DIM))
    pex = pex.reshape(2, CMP_STRIDE, 1, 256)
    return (wl[0], wl[1], pex[0], pex[1], w2bd, jnp.tile(kg0, 2)[None, :])


def _layer(x, lw, page_table, caches, dims):
    b, s, db = dims
    tp = b * s
    d = x.shape[1]

    x = _ffn(x, lw['ffn1_norm'], lw['ffn1_w_gu'], lw['ffn1_w_down'])

    (fq, foxkv, foxkvb, gmu, gmv, nq, nsakv, win, nsab, small, mg) = _proj(
        x, lw['mix_norm'], lw['w_in'], lw['gq'], lw['gk'], lw['gv'], lw['gnq'], lw['gk1'], lw['gk2'], lw['bf'])

    c = _cumsum(small[:tp].reshape(b, s, LANES))[:, :, :N_HEADS].reshape(b, s, 4, 2)
    cq = c.transpose(0, 2, 1, 3)
    ck = c.transpose(0, 2, 3, 1)
    o_fox_p = _fox_prompt(fq[:tp].reshape(b, s, 512), foxkvb[:tp].reshape(b, s, 1024), cq, ck)
    o_gm_p = _gmlp_prompt(gmu, gmv, lw['gmlp_w_s'], lw['gmlp_bexp'], tp // CHUNK)
    cmpkv = _compress_prompt(nsakv[:tp].reshape(b, s, 512), lw['cw'])
    o_nsa_p = _nsa_prompt(nq[:tp].reshape(b, s, 1024), cmpkv, nsab[:tp].reshape(b, s, 512),
                          small[:tp].reshape(b, s, LANES))

    cache_fox_kv, cache_fox_lft, cache_nsa, win_state = caches
    o_fox_s = _fox_decode(page_table, cache_fox_kv, cache_fox_lft, fq[tp:].reshape(db, 1, 512),
                          foxkv[tp:].reshape(db, 1, 1024), small[tp:, :N_HEADS].reshape(db, N_HEADS, 1))
    o_gm_s = _gmlp_sample(gmu, gmv, lw['gmlp_w00'], lw['gmlp_b0'], tp, db)
    gates_s = small[tp:, 8:32].reshape(db, 3, N_HEADS).transpose(0, 2, 1)
    o_nsa_s = _nsa_decode(page_table, cache_nsa, win_state, nq[tp:].reshape(db, N_HEADS, LANES),
                          nsab[tp:].reshape(db, 1, 512), gates_s, lw['cw'])

    o_fox = jnp.concatenate([o_fox_p.reshape(tp, 512), o_fox_s.reshape(db, 512)], axis=0)
    o_gm = jnp.concatenate([o_gm_p, o_gm_s], axis=0)
    o_nsa = jnp.concatenate([o_nsa_p.reshape(tp, 1024), o_nsa_s.reshape(db, 1024)], axis=0)
    x = _merge(x, o_fox, o_gm, o_nsa, mg, lw['wb0'], lw['wb1'], lw['wb2'], lw['w_out'])

    x = _ffn(x, lw['ffn2_norm'], lw['ffn2_w_gu'], lw['ffn2_w_down'])

    wb = win_state.shape[1]
    state = dict(
        fox_kv_p=foxkv[:tp].reshape(b, s, 2, N_HEADS, HEAD_DIM),
        fox_kv_s=foxkv[tp:].reshape(db, 1, 2, N_HEADS, HEAD_DIM),
        fox_logf_p=small[:tp, :N_HEADS].reshape(b, s, N_HEADS),
        fox_logf_s=small[tp:, :N_HEADS].reshape(db, 1, N_HEADS),
        nsa_kv_p=nsakv[:tp].reshape(b, s, 4, N_GROUPS, HEAD_DIM),
        nsa_kv_s=nsakv[tp:].reshape(db, 1, 4, N_GROUPS, HEAD_DIM),
        win_p=win[:tp].reshape(b, s, 2, N_GROUPS, HEAD_DIM)[:, s - min(WINDOW, s):],
        win_s=jnp.concatenate([win_state[:, 1:].reshape(db, wb - 1, 2, N_GROUPS, HEAD_DIM),
                               win[tp:].reshape(db, 1, 2, N_GROUPS, HEAD_DIM)], axis=1),
        gmlp_v_p=gmv[:tp].reshape(b, s, 512)[:, s - min(CHUNK, s):],
        gmlp_v_s=gmv[tp:].reshape(db, 1, 512),
    )
    return x, state


def kernel(x_prompt, x_sample, page_table, cache_fox_kv, cache_fox_logf, cache_nsa_kv, state_nsa_win, ffn1_norm, ffn1_w_gu, ffn1_w_down, mix_norm, w_in, fox_b_f, fox_qk_gain, gmlp_v_gain, gmlp_w_s, gmlp_b_s, nsa_q_gain, nsa_k_gain, nsa_cmp_pe, nsa_cmp_w1, nsa_cmp_w2, w_branch, w_out, ffn2_norm, ffn2_w_gu, ffn2_w_down):
    b, s, d = x_prompt.shape
    db = x_sample.shape[0]
    depth = w_in.shape[0]
    n_pool = cache_fox_kv.shape[1]
    tp = b * s
    x = jnp.concatenate([x_prompt.reshape(tp, d), x_sample.reshape(db, d)], axis=0)

    states = []
    for l in range(depth):
        nsa_rows = (np.arange(N_HEADS * LANES) % LANES) // HEAD_DIM == (np.arange(N_HEADS * LANES) // LANES) // N_REP
        wb2 = jnp.zeros((N_HEADS * LANES, d), F32).at[np.nonzero(nsa_rows)[0]].set(w_branch[l, 2])
        lw = dict(
            ffn1_norm=ffn1_norm[l][None, :], ffn1_w_gu=ffn1_w_gu[l].astype(BF16), ffn1_w_down=ffn1_w_down[l].astype(BF16),
            ffn2_norm=ffn2_norm[l][None, :], ffn2_w_gu=ffn2_w_gu[l].astype(BF16), ffn2_w_down=ffn2_w_down[l].astype(BF16),
            mix_norm=mix_norm[l][None, :], w_in=_prep_w_in(w_in[l]),
            gq=jnp.tile(fox_qk_gain[l, 0], 2)[None, :], gk=jnp.tile(fox_qk_gain[l, 1], 2)[None, :],
            gv=gmlp_v_gain[l][None, :], gnq=_pad_heads(nsa_q_gain[l]),
            gk1=jnp.tile(nsa_k_gain[l, 1], 2)[None, :], gk2=jnp.tile(nsa_k_gain[l, 2], 2)[None, :],
            bf=jnp.concatenate([fox_b_f[l], jnp.zeros((LANES - N_HEADS,), F32)])[None, :],
            gmlp_w_s=gmlp_w_s[l], gmlp_bexp=jnp.repeat(gmlp_b_s[l].T, HEAD_DIM, axis=1),
            gmlp_w00=jnp.repeat(gmlp_w_s[l, :, 0, 0], HEAD_DIM)[None, :],
            gmlp_b0=jnp.repeat(gmlp_b_s[l, :, 0], HEAD_DIM)[None, :],
            cw=_prep_compress(nsa_cmp_w1[l], nsa_cmp_w2[l], nsa_cmp_pe[l], nsa_k_gain[l, 0]),
            wb0=w_branch[l, 0].astype(BF16), wb1=w_branch[l, 1].astype(BF16), wb2=wb2.astype(BF16),
            w_out=w_out[l].astype(BF16),
        )
        caches = (cache_fox_kv[l].reshape(n_pool, PAGE, 2 * N_HEADS * HEAD_DIM),
                  cache_fox_logf[l].transpose(0, 2, 1),
                  cache_nsa_kv[l].reshape(n_pool, PAGE, 4 * N_GROUPS * HEAD_DIM),
                  state_nsa_win[l].reshape(db, state_nsa_win.shape[2], 2 * N_GROUPS * HEAD_DIM))
        x, st = _layer(x, lw, page_table, caches, (b, s, db))
        states.append(st)

    def stack(name):
        return jnp.stack([st[name] for st in states])

    return (x[:tp].reshape(b, s, d), x[tp:].reshape(db, 1, d),
            stack('fox_kv_p'), stack('fox_kv_s'), stack('fox_logf_p'), stack('fox_logf_s'),
            stack('nsa_kv_p'), stack('nsa_kv_s'), stack('win_p'), stack('win_s'),
            stack('gmlp_v_p'), stack('gmlp_v_s'))
```

```python
import functools

import jax
import jax.numpy as jnp
import numpy as np
from jax import lax
from jax.experimental import pallas as pl
from jax.experimental.pallas import tpu as pltpu

F32 = jnp.float32
BF16 = jnp.bfloat16

HEAD_DIM = 64
LANES = 128
PAGE = 128
CHUNK = 128
Q_BLOCK = 128
FOX_TQ = 256
ATT_TK = 512
CMP_LEN = 32
CMP_STRIDE = 16
SEL_BLOCK = 64
N_SEL = 8
WINDOW = 512
N_GROUPS = 2
N_REP = 4
N_HEADS = 8
FORCE_BONUS = 1.0e4
NEG = -1.0e30
EPS = 1e-6
SCALE = HEAD_DIM ** -0.5
VMEM_LIMIT = 56 * 1024 * 1024
HI = lax.Precision.HIGHEST


def _cparams(*sem):
    return pltpu.CompilerParams(dimension_semantics=sem, vmem_limit_bytes=VMEM_LIMIT)


def _dot(a, b, precision=None):
    return jnp.dot(a, b, preferred_element_type=F32, precision=precision)


def _dot_nt(a, b):
    return lax.dot_general(a, b, (((1,), (1,)), ((), ())), preferred_element_type=F32)


def _shr(x, pow2):
    return jnp.right_shift(x, int(pow2).bit_length() - 1)


def _rms_rows(x, g):
    ms = jnp.mean(x * x, axis=-1, keepdims=True)
    return x * lax.rsqrt(ms + EPS) * g


def _headnorm_pair(zb, gain):
    lo = lax.broadcasted_iota(jnp.int32, zb.shape, 1) < HEAD_DIM
    sq = zb * zb
    s_lo = jnp.sum(jnp.where(lo, sq, 0.0), axis=-1, keepdims=True)
    s_hi = jnp.sum(jnp.where(lo, 0.0, sq), axis=-1, keepdims=True)
    ms = jnp.where(lo, s_lo, s_hi) * (1.0 / HEAD_DIM)
    return zb * lax.rsqrt(ms + EPS) * gain


def _headnorm_cols(zt, gain_col):
    ms = jnp.mean(zt * zt, axis=0, keepdims=True)
    return zt * lax.rsqrt(ms + EPS) * gain_col


def _silu(x):
    return x * jax.nn.sigmoid(x)


def _gelu_tanh(x):
    return 0.5 * x * (1.0 + jnp.tanh(np.sqrt(2.0 / np.pi) * (x + 0.044715 * (x * x * x))))


def _log_sigmoid(x):
    return jnp.minimum(x, 0.0) - jnp.log1p(jnp.exp(-jnp.abs(x)))


def _token_tile(t):
    for tm in (512, 384, 256, 128):
        if t % tm == 0:
            return tm
    raise ValueError(f"token count {t} is not a multiple of 128")


def _const_spec(shape):
    nd = len(shape)
    return pl.BlockSpec(shape, lambda *_: (0,) * nd, pipeline_mode=pl.Buffered(1))


def _ffn_body(x_ref, g_ref, wgu_ref, wd_ref, o_ref, *, d_ff, fc):
    x = x_ref[...]
    h = _rms_rows(x, g_ref[...]).astype(BF16)
    acc = jnp.zeros_like(x)
    for c in range(d_ff // fc):
        g = _dot(h, wgu_ref[:, c * fc:(c + 1) * fc])
        u = _dot(h, wgu_ref[:, d_ff + c * fc:d_ff + (c + 1) * fc])
        a = (_silu(g) * u).astype(BF16)
        acc = acc + _dot(a, wd_ref[c * fc:(c + 1) * fc, :])
    o_ref[...] = x + 0.5 * acc


def _ffn(x, g, wgu, wd):
    t, d = x.shape
    d_ff = wd.shape[0]
    tm = _token_tile(t)
    fc = 256 if d_ff % 256 == 0 else 128
    return pl.pallas_call(
        functools.partial(_ffn_body, d_ff=d_ff, fc=fc),
        grid=(t // tm,),
        in_specs=[pl.BlockSpec((tm, d), lambda i: (i, 0)),
                  _const_spec((1, d)),
                  _const_spec((d, 2 * d_ff)),
                  _const_spec((d_ff, d))],
        out_specs=pl.BlockSpec((tm, d), lambda i: (i, 0)),
        out_shape=jax.ShapeDtypeStruct((t, d), F32),
        compiler_params=_cparams("parallel"),
        name="ffn",
    )(x, g, wgu, wd)


C_FOX = 0
C_GM = 1536
C_NQ = 2560
C_NKV = 3584
C_SMALL = 4352
C_END = 4480
R_NKV = 1024
R_F = 1792
R_END = 1808


def _proj_body(x_ref, g_ref, w_ref, wt_ref, gq_ref, gk_ref, gkc_ref, gv_ref, gnq_ref, gk1_ref, gk2_ref,
               gk1c_ref, gk2c_ref, bfc_ref, *out_refs, sample):
    (fq_ref, foxt_ref, foxtb_ref, gmu_ref, gmv_ref, nq_ref, nsat_ref, wint_ref, nsatb_ref, cmpraw_ref,
     lft_ref, gate_ref) = out_refs[:12]
    h = _rms_rows(x_ref[...], g_ref[...]).astype(BF16)

    z = _dot(h, w_ref[:, 0:512])
    for j in range(4):
        sl = slice(j * LANES, (j + 1) * LANES)
        fq_ref[:, sl] = (_headnorm_pair(z[:, sl], gq_ref[...]) * SCALE).astype(BF16)
    zt = _dot_nt(wt_ref[0:R_NKV, :], h)
    for hh in range(N_HEADS):
        rs = slice(hh * HEAD_DIM, (hh + 1) * HEAD_DIM)
        kt = _headnorm_cols(zt[rs], gkc_ref[...])
        foxt_ref[rs, :] = kt
        foxtb_ref[rs, :] = kt.astype(BF16)
    foxt_ref[512:1024, :] = zt[512:1024]
    foxtb_ref[512:1024, :] = zt[512:1024].astype(BF16)

    a = _gelu_tanh(_dot(h, w_ref[:, C_GM:C_NQ]))
    gmu_ref[...] = a[:, 0:512]
    gmv_ref[...] = _rms_rows(a[:, 512:1024], gv_ref[...])

    z = _dot(h, w_ref[:, C_NQ:C_NKV])
    for hh in range(N_HEADS):
        sl = slice(hh * LANES, (hh + 1) * LANES)
        zb = z[:, sl]
        ms = jnp.sum(zb * zb, axis=-1, keepdims=True) * (1.0 / HEAD_DIM)
        nq_ref[:, sl] = (zb * lax.rsqrt(ms + EPS) * (gnq_ref[:, sl] * SCALE)).astype(BF16)

    zt = _dot_nt(wt_ref[R_NKV:R_F, :], h)
    nsat_ref[0:256, :] = zt[0:256]
    nsat_ref[384:512, :] = zt[384:512]
    nsatb_ref[128:256, :] = zt[384:512].astype(BF16)
    wint_ref[128:256, :] = zt[640:768]
    nsatb_ref[384:512, :] = zt[640:768].astype(BF16)
    for g in range(N_GROUPS):
        sk = _headnorm_cols(zt[256 + g * HEAD_DIM:256 + (g + 1) * HEAD_DIM], gk1c_ref[...])
        nsat_ref[256 + g * HEAD_DIM:256 + (g + 1) * HEAD_DIM, :] = sk
        nsatb_ref[g * HEAD_DIM:(g + 1) * HEAD_DIM, :] = sk.astype(BF16)
        wk = _headnorm_cols(zt[512 + g * HEAD_DIM:512 + (g + 1) * HEAD_DIM], gk2c_ref[...])
        wint_ref[g * HEAD_DIM:(g + 1) * HEAD_DIM, :] = wk
        nsatb_ref[256 + g * HEAD_DIM:256 + (g + 1) * HEAD_DIM, :] = wk.astype(BF16)

    cmpraw_ref[...] = _dot(h, w_ref[:, C_NKV:C_NKV + 256])

    zt = _dot_nt(wt_ref[R_F:R_END, :], h)
    lft_ref[...] = _log_sigmoid(zt[0:N_HEADS] + bfc_ref[...])
    gate_ref[...] = jax.nn.sigmoid(_dot(h, w_ref[:, C_SMALL:C_END]))

    if sample:
        foxs_ref, nsabs_ref = out_refs[12:]
        z = _dot(h, w_ref[:, 512:C_GM])
        for j in range(4):
            sl = slice(j * LANES, (j + 1) * LANES)
            foxs_ref[:, sl] = _headnorm_pair(z[:, sl], gk_ref[...])
        foxs_ref[:, 512:1024] = z[:, 512:1024]
        z = _dot(h, w_ref[:, C_NKV + 256:C_SMALL])
        nsabs_ref[:, 0:128] = _headnorm_pair(z[:, 0:128], gk1_ref[...]).astype(BF16)
        nsabs_ref[:, 128:256] = z[:, 128:256].astype(BF16)
        nsabs_ref[:, 256:384] = _headnorm_pair(z[:, 256:384], gk2_ref[...]).astype(BF16)
        nsabs_ref[:, 384:512] = z[:, 384:512].astype(BF16)


def _proj(x, pw, b, s, sample):
    t, d = x.shape
    tm = _token_tile(s)
    nst = s // tm
    tok = lambda n, dt: (pl.BlockSpec((tm, n), lambda i: (i, 0)), jax.ShapeDtypeStruct((t, n), dt))
    feat = lambda n, dt: (pl.BlockSpec((None, n, tm), lambda i: (i // nst, 0, i % nst)),
                          jax.ShapeDtypeStruct((b, n, s), dt))
    outs = [tok(512, BF16), feat(1024, F32), feat(1024, BF16), tok(512, F32), tok(512, F32), tok(1024, BF16),
            feat(512, F32), feat(256, F32), feat(512, BF16), tok(256, F32), feat(N_HEADS, F32), tok(LANES, F32)]
    if sample:
        outs += [tok(1024, F32), tok(512, BF16)]
    return pl.pallas_call(
        functools.partial(_proj_body, sample=sample),
        grid=(t // tm,),
        in_specs=[pl.BlockSpec((tm, d), lambda i: (i, 0))] + [_const_spec(a.shape) for a in pw],
        out_specs=[o[0] for o in outs],
        out_shape=[o[1] for o in outs],
        compiler_params=_cparams("parallel"),
        name="in_proj_sample" if sample else "in_proj",
    )(x, *pw)


def _cumsum_body(x_ref, o_ref, *, blk):
    s = x_ref.shape[1]
    r = lax.broadcasted_iota(jnp.int32, (blk, blk), 0)
    c = lax.broadcasted_iota(jnp.int32, (blk, blk), 1)
    tri = (r <= c).astype(F32)
    carry = jnp.zeros((x_ref.shape[0], 1), F32)
    for i in range(s // blk):
        cs = _dot(x_ref[:, i * blk:(i + 1) * blk], tri, precision=HI) + carry
        o_ref[:, i * blk:(i + 1) * blk] = cs
        carry = cs[:, blk - 1:blk]


def _cumsum(x):
    b, n, s = x.shape
    return pl.pallas_call(
        functools.partial(_cumsum_body, blk=min(256, s)),
        grid=(b,),
        in_specs=[pl.BlockSpec((None, n, s), lambda i: (i, 0, 0))],
        out_specs=pl.BlockSpec((None, n, s), lambda i: (i, 0, 0)),
        out_shape=jax.ShapeDtypeStruct((b, n, s), F32),
        compiler_params=_cparams("parallel"),
        name="logf_cumsum",
    )(x)


def _fox_body(q_ref, k_ref, v_ref, c_ref, o_ref, *, tq, tk):
    qi = pl.program_id(2)
    q = q_ref[...]
    lo = lax.broadcasted_iota(jnp.int32, (tq, LANES), 1) < HEAD_DIM
    zero = jnp.zeros_like(q)
    qh = (jnp.where(lo, q, zero), jnp.where(lo, zero, q))
    qpos = qi * tq + lax.broadcasted_iota(jnp.int32, (tq, 1), 0)

    def block(kb, carry, masked):
        t0 = pl.multiple_of(kb * tk, tk)
        k = k_ref[:, pl.ds(t0, tk)]
        v = v_ref[:, pl.ds(t0, tk)]
        acc = carry[4]
        stats, alphas, pvs = [], [], []
        for hh in range(2):
            m_old, l_old = carry[2 * hh], carry[2 * hh + 1]
            s = _dot(qh[hh], k) - c_ref[hh:hh + 1, pl.ds(t0, tk)]
            if masked:
                kpos = t0 + lax.broadcasted_iota(jnp.int32, (1, tk), 1)
                s = jnp.where(kpos <= qpos, s, NEG)
            m_new = jnp.maximum(m_old, jnp.max(s, axis=-1, keepdims=True))
            alpha = jnp.exp(m_old - m_new)
            p = jnp.exp(s - m_new)
            stats += [m_new, alpha * l_old + jnp.sum(p, axis=-1, keepdims=True)]
            alphas.append(alpha)
            pvs.append(_dot_nt(p.astype(BF16), v))
        acc = jnp.where(lo, alphas[0], alphas[1]) * acc + jnp.where(lo, pvs[0], pvs[1])
        return tuple(stats) + (acc,)

    n_kb = _shr(qi * tq + tq + tk - 1, tk)
    col = lambda v: jnp.full((tq, 1), v, F32)
    carry = (col(NEG), col(0.0), col(NEG), col(0.0), jnp.zeros((tq, LANES), F32))
    carry = lax.fori_loop(0, n_kb - 1, lambda kb, c: block(kb, c, False), carry)
    carry = block(n_kb - 1, carry, True)
    o_ref[...] = (carry[4] / jnp.where(lo, carry[1], carry[3])).astype(o_ref.dtype)


def _fox_prompt(fq, foxtb, c):
    b, s, _ = fq.shape
    tq = min(FOX_TQ, s)
    tk = min(ATT_TK, s)
    return pl.pallas_call(
        functools.partial(_fox_body, tq=tq, tk=tk),
        grid=(b, 4, s // tq),
        in_specs=[pl.BlockSpec((None, tq, LANES), lambda bi, p, qi: (bi, qi, p)),
                  pl.BlockSpec((None, LANES, s), lambda bi, p, qi: (bi, p, 0)),
                  pl.BlockSpec((None, LANES, s), lambda bi, p, qi: (bi, 4 + p, 0)),
                  pl.BlockSpec((None, None, 2, s), lambda bi, p, qi: (bi, p, 0, 0))],
        out_specs=pl.BlockSpec((None, tq, LANES), lambda bi, p, qi: (bi, qi, p)),
        out_shape=jax.ShapeDtypeStruct((b, s, 512), BF16),
        compiler_params=_cparams("parallel", "parallel", "arbitrary"),
        name="fox_prompt",
    )(fq, foxtb, foxtb, c)


def _gmlp_body(u_ref, v_ref, ws_ref, b_ref, o_ref):
    n = ws_ref.shape[1]
    r = lax.broadcasted_iota(jnp.int32, (n, n), 0)
    c = lax.broadcasted_iota(jnp.int32, (n, n), 1)
    lo = lax.broadcasted_iota(jnp.int32, (n, LANES), 1) < HEAD_DIM
    for j in range(4):
        sl = slice(j * LANES, (j + 1) * LANES)
        vp = v_ref[:, sl].astype(BF16)
        w0 = jnp.where(r >= c, ws_ref[2 * j], 0.0).astype(BF16)
        w1 = jnp.where(r >= c, ws_ref[2 * j + 1], 0.0).astype(BF16)
        mixed = jnp.where(lo, _dot(w0, vp), _dot(w1, vp)) + b_ref[:, sl]
        o_ref[:, sl] = (u_ref[:, sl] * mixed).astype(o_ref.dtype)


def _gmlp_prompt(gmu, gmv, ws, bexp):
    t = gmu.shape[0]
    return pl.pallas_call(
        _gmlp_body,
        grid=(t // CHUNK,),
        in_specs=[pl.BlockSpec((CHUNK, 512), lambda i: (i, 0)),
                  pl.BlockSpec((CHUNK, 512), lambda i: (i, 0)),
                  _const_spec(ws.shape), _const_spec(bexp.shape)],
        out_specs=pl.BlockSpec((CHUNK, 512), lambda i: (i, 0)),
        out_shape=jax.ShapeDtypeStruct((t, 512), BF16),
        compiler_params=_cparams("parallel"),
        name="gmlp_prompt",
    )(gmu, gmv, ws, bexp)


def _gmlp_first_body(u_ref, v_ref, w_ref, b_ref, o_ref):
    o_ref[...] = (u_ref[...] * (v_ref[...] * w_ref[...] + b_ref[...])).astype(o_ref.dtype)


def _gmlp_sample(gmu, gmv, wrow, brow):
    n = gmu.shape[0]
    full = pl.BlockSpec((n, 512), lambda i: (0, 0))
    return pl.pallas_call(
        _gmlp_first_body,
        grid=(1,),
        in_specs=[full, full, _const_spec((1, 512)), _const_spec((1, 512))],
        out_specs=full,
        out_shape=jax.ShapeDtypeStruct((n, 512), BF16),
        compiler_params=_cparams("arbitrary"),
        name="gmlp_sample",
    )(gmu, gmv, wrow, brow)


def _compress(xk_ref, xv_ref, wa_ref, wb_ref, pea_ref, peb_ref, w2_ref, kg0_ref):
    nhb = xk_ref.shape[0] // CMP_STRIDE
    a = jnp.zeros((nhb, 256), F32)
    b = jnp.zeros((nhb, 256), F32)
    for l in range(CMP_STRIDE):
        xl = jnp.concatenate([xk_ref[pl.ds(l, nhb, stride=CMP_STRIDE), :],
                              xv_ref[pl.ds(l, nhb, stride=CMP_STRIDE), :]], axis=1)
        a = a + _dot((xl + pea_ref[l]).astype(BF16), wa_ref[l])
        b = b + _dot((xl + peb_ref[l]).astype(BF16), wb_ref[l])
    pre = a + pltpu.roll(b, nhb - 1, 0)
    out = _dot(_silu(pre).astype(BF16), w2_ref[...])
    kc = _headnorm_pair(out[:, 0:LANES], kg0_ref[...])
    return kc, out[:, LANES:2 * LANES]


def _compress_body(xk_ref, xv_ref, wa_ref, wb_ref, pea_ref, peb_ref, w2_ref, kg0_ref, o_ref):
    kc, vc = _compress(xk_ref, xv_ref, wa_ref, wb_ref, pea_ref, peb_ref, w2_ref, kg0_ref)
    o_ref[:, 0:LANES] = kc.astype(BF16)
    o_ref[:, LANES:2 * LANES] = vc.astype(BF16)


def _compress_prompt(cmpraw, cw):
    b, s, _ = cmpraw.shape
    nhb = s // CMP_STRIDE
    return pl.pallas_call(
        _compress_body,
        grid=(b,),
        in_specs=[pl.BlockSpec((None, s, LANES), lambda i: (i, 0, 0)),
                  pl.BlockSpec((None, s, LANES), lambda i: (i, 0, 1))] + [_const_spec(a.shape) for a in cw],
        out_specs=pl.BlockSpec((None, nhb, 256), lambda i: (i, 0, 0)),
        out_shape=jax.ShapeDtypeStruct((b, nhb, 256), BF16),
        compiler_params=_cparams("parallel"),
        name="nsa_compress",
    )(cmpraw, cmpraw, *cw)


def _slope_col(head):
    out = jnp.zeros(head.shape, F32)
    for hh in range(N_HEADS):
        out = jnp.where(head == hh, 2.0 ** (-(hh + 1)), out)
    return out


def _cmp_softmax(s, mask):
    s = jnp.where(mask, s, NEG)
    m = jnp.max(s, axis=-1, keepdims=True)
    e = jnp.where(mask, jnp.exp(s - m), 0.0)
    l = jnp.sum(e, axis=-1, keepdims=True)
    return e / jnp.where(l > 0.0, l, 1.0)


def _select_blocks(imp, cur, n_blk):
    m, n = imp.shape
    pool = (_shr(lax.broadcasted_iota(jnp.int32, (n, LANES), 0), SEL_BLOCK // CMP_STRIDE)
            == lax.broadcasted_iota(jnp.int32, (n, LANES), 1)).astype(F32)
    score = _dot(imp, pool, precision=HI)
    j = lax.broadcasted_iota(jnp.int32, (m, LANES), 1)
    forced = (j == 0) | (j == cur) | (j == cur - 1)
    score = jnp.where(forced, score + FORCE_BONUS, score)
    score = jnp.where(j > cur, NEG, score)
    rank = jnp.zeros((m, LANES), F32)
    for i in range(n_blk):
        ci = score[:, i:i + 1]
        ahead = (ci > score) | ((ci == score) & (j > i))
        rank = rank + ahead.astype(F32)
    return (rank < float(min(N_SEL, n_blk))) & (score > 0.5 * NEG)


def _expand_sel(sel_bf, t0, n):
    jj = lax.broadcasted_iota(jnp.int32, (LANES, n), 0)
    tt = t0 + lax.broadcasted_iota(jnp.int32, (LANES, n), 1)
    e = (_shr(tt, SEL_BLOCK) == jj).astype(BF16)
    return _dot(sel_bf, e)


def _nsa_body(nq_ref, cmp_ref, kv_ref, gate_ref, o_ref, m_ref, l_ref, acc_ref, *, n_blk, tk, wlen):
    qi = pl.program_id(1)
    s_len = kv_ref.shape[1]
    rows = N_REP * Q_BLOCK
    row = lax.broadcasted_iota(jnp.int32, (rows, 1), 0)
    qpos = qi * Q_BLOCK + (row & (Q_BLOCK - 1))
    qpos_q = qi * Q_BLOCK + lax.broadcasted_iota(jnp.int32, (Q_BLOCK, 1), 0)
    lane = lax.broadcasted_iota(jnp.int32, (1, LANES), 1)
    nhb = cmp_ref.shape[0]
    end = lax.broadcasted_iota(jnp.int32, (1, nhb), 1) * CMP_STRIDE + (CMP_LEN - 1)

    for g in range(N_GROUPS):
        q = jnp.concatenate([nq_ref[:, (N_REP * g + r) * LANES:(N_REP * g + r + 1) * LANES]
                             for r in range(N_REP)], axis=0)
        slope = _slope_col(N_REP * g + _shr(row, Q_BLOCK))

        s = _dot_nt(q, cmp_ref[:, 0:LANES]) - slope * (qpos - end).astype(F32)
        p = _cmp_softmax(s, end <= qpos)
        o_cmp = _dot(p.astype(BF16), cmp_ref[:, LANES:2 * LANES])
        imp = p[0:Q_BLOCK]
        for r in range(1, N_REP):
            imp = imp + p[r * Q_BLOCK:(r + 1) * Q_BLOCK]
        sel = _select_blocks(imp, _shr(qpos_q, SEL_BLOCK), n_blk).astype(BF16)

        m_ref[...] = jnp.full(m_ref.shape, NEG, F32)
        l_ref[...] = jnp.zeros(l_ref.shape, F32)
        acc_ref[...] = jnp.zeros(acc_ref.shape, F32)

        def sel_step(kb, carry):
            t0 = pl.multiple_of(kb * tk, tk)
            k = kv_ref[0:LANES, pl.ds(t0, tk)]
            v = kv_ref[LANES:2 * LANES, pl.ds(t0, tk)]
            kpos = t0 + lax.broadcasted_iota(jnp.int32, (1, tk), 1)
            s = _dot(q, k) - slope * (qpos - kpos).astype(F32)
            mk = _expand_sel(sel, t0, tk)
            mk = (jnp.concatenate([mk] * N_REP, axis=0) > 0.5) & (kpos <= qpos)
            s = jnp.where(mk, s, NEG)
            m_old = m_ref[:, 0:1]
            m_new = jnp.maximum(m_old, jnp.max(s, axis=-1, keepdims=True))
            alpha = jnp.exp(m_old - m_new)
            p = jnp.exp(s - m_new)
            l_ref[:, 0:1] = alpha * l_ref[:, 0:1] + jnp.sum(p, axis=-1, keepdims=True)
            m_ref[:, 0:1] = m_new
            acc_ref[...] = alpha * acc_ref[...] + _dot_nt(p.astype(BF16), v)
            return carry

        lax.fori_loop(0, _shr(qi * Q_BLOCK + Q_BLOCK + tk - 1, tk), sel_step, 0)
        o_sel = acc_ref[...] / l_ref[:, 0:1]

        t0 = jnp.clip((qi - WINDOW // Q_BLOCK) * Q_BLOCK, 0, s_len - wlen)
        t0 = pl.multiple_of(t0, Q_BLOCK)
        k = kv_ref[2 * LANES:3 * LANES, pl.ds(t0, wlen)]
        v = kv_ref[3 * LANES:4 * LANES, pl.ds(t0, wlen)]
        d = qpos - (t0 + lax.broadcasted_iota(jnp.int32, (1, wlen), 1))
        s = jnp.where((d >= 0) & (d < WINDOW), _dot(q, k) - slope * d.astype(F32), NEG)
        p = jnp.exp(s - jnp.max(s, axis=-1, keepdims=True))
        o_win = _dot_nt(p.astype(BF16), v) / jnp.sum(p, axis=-1, keepdims=True)

        in_group = _shr(lane, HEAD_DIM) == g
        for r in range(N_REP):
            hh = N_REP * g + r
            rs = slice(r * Q_BLOCK, (r + 1) * Q_BLOCK)
            o = (gate_ref[:, 8 + hh:9 + hh] * o_cmp[rs] + gate_ref[:, 16 + hh:17 + hh] * o_sel[rs]
                 + gate_ref[:, 24 + hh:25 + hh] * o_win[rs])
            o_ref[:, hh * LANES:(hh + 1) * LANES] = jnp.where(in_group, o, 0.0).astype(o_ref.dtype)


def _nsa_prompt(nq, cmpkv, nsatb, gates):
    b, s, _ = nq.shape
    nhb = cmpkv.shape[1]
    rows = N_REP * Q_BLOCK
    return pl.pallas_call(
        functools.partial(_nsa_body, n_blk=s // SEL_BLOCK, tk=min(ATT_TK, s), wlen=min(WINDOW + Q_BLOCK, s)),
        grid=(b, s // Q_BLOCK),
        in_specs=[pl.BlockSpec((None, Q_BLOCK, 1024), lambda bi, qi: (bi, qi, 0)),
                  pl.BlockSpec((None, nhb, 256), lambda bi, qi: (bi, 0, 0)),
                  pl.BlockSpec((None, 512, s), lambda bi, qi: (bi, 0, 0)),
                  pl.BlockSpec((None, Q_BLOCK, LANES), lambda bi, qi: (bi, qi, 0))],
        out_specs=pl.BlockSpec((None, Q_BLOCK, 1024), lambda bi, qi: (bi, qi, 0)),
        out_shape=jax.ShapeDtypeStruct((b, s, 1024), BF16),
        scratch_shapes=[pltpu.VMEM((rows, LANES), F32), pltpu.VMEM((rows, LANES), F32),
                        pltpu.VMEM((rows, LANES), F32)],
        compiler_params=_cparams("parallel", "arbitrary"),
        name="nsa_prompt",
    )(nq, cmpkv, nsatb, gates)


def _merge_body(x_ref, g_ref, of_ref, og_ref, on_ref, wg_ref, wb0_ref, wb1_ref, wb2_ref, wo_ref, o_ref):
    x = x_ref[...]
    d = x.shape[1]
    h = _rms_rows(x, g_ref[...]).astype(BF16)
    y = jax.nn.sigmoid(_dot(h, wg_ref[:, 0:d])) * _dot(of_ref[...], wb0_ref[...])
    y = y + jax.nn.sigmoid(_dot(h, wg_ref[:, d:2 * d])) * _dot(og_ref[...], wb1_ref[...])
    y = y + jax.nn.sigmoid(_dot(h, wg_ref[:, 2 * d:3 * d])) * _dot(on_ref[...], wb2_ref[...])
    o_ref[...] = x + _dot(y.astype(BF16), wo_ref[...])


def _merge(x, g, o_fox, o_gm, o_nsa, wg, wb0, wb1, wb2, wo):
    t, d = x.shape
    tm = _token_tile(t)
    row = lambda n: pl.BlockSpec((tm, n), lambda i: (i, 0))
    consts = [wg, wb0, wb1, wb2, wo]
    return pl.pallas_call(
        _merge_body,
        grid=(t // tm,),
        in_specs=[row(d), _const_spec((1, d)), row(512), row(512), row(1024)] + [_const_spec(a.shape) for a in consts],
        out_specs=row(d),
        out_shape=jax.ShapeDtypeStruct((t, d), F32),
        compiler_params=_cparams("parallel"),
        name="merge",
    )(x, g, o_fox, o_gm, o_nsa, *consts)


def _column(x, idx):
    lane = lax.broadcasted_iota(jnp.int32, x.shape, 1)
    return jnp.sum(jnp.where(lane == idx, x, 0.0), axis=1, keepdims=True)


def _fox_decode_body(pt_ref, *refs, n_pages):
    kv = refs[:n_pages]
    lf = refs[n_pages:2 * n_pages]
    q_ref, new_ref, lfn_ref, o_ref, s_ref = refs[2 * n_pages:]
    w = N_HEADS * HEAD_DIM
    row = lax.broadcasted_iota(jnp.int32, (N_HEADS, w), 0)
    lane = lax.broadcasted_iota(jnp.int32, (N_HEADS, w), 1)
    diag = _shr(lane, HEAD_DIM) == row
    qbd = jnp.where(diag, jnp.broadcast_to(q_ref[...].astype(F32), (N_HEADS, w)), 0.0).astype(BF16)
    upper = (lax.broadcasted_iota(jnp.int32, (PAGE, PAGE), 0)
             > lax.broadcasted_iota(jnp.int32, (PAGE, PAGE), 1)).astype(F32)

    later = _column(lfn_ref[...], pl.program_id(0))
    for i in reversed(range(n_pages)):
        s = _dot(qbd, kv[i][0].astype(BF16))
        s_ref[:, i * PAGE:(i + 1) * PAGE] = s + _dot(lf[i][...], upper, precision=HI) + later
        later = later + jnp.sum(lf[i][...], axis=-1, keepdims=True)

    k_new = new_ref[:, 0:w].astype(BF16).astype(F32)
    v_new = new_ref[:, w:2 * w].astype(BF16).astype(F32)
    s_new = jnp.sum(qbd.astype(F32) * k_new, axis=-1, keepdims=True)
    m = jnp.maximum(jnp.max(s_ref[...], axis=-1, keepdims=True), s_new)
    p_new = jnp.exp(s_new - m)
    l = p_new
    o = p_new.astype(BF16).astype(F32) * v_new
    for i in range(n_pages):
        p = jnp.exp(s_ref[:, i * PAGE:(i + 1) * PAGE] - m)
        l = l + jnp.sum(p, axis=-1, keepdims=True)
        o = o + _dot_nt(p.astype(BF16), kv[i][1].astype(BF16))
    o_ref[...] = jnp.sum(jnp.where(diag, o / l, 0.0), axis=0, keepdims=True).astype(o_ref.dtype)


def _fox_decode(page_table, cache_kv, cache_lf, layer, fq_s, foxkv_s, lft_s):
    db, n_pages = page_table.shape
    w = N_HEADS * HEAD_DIM
    grid_spec = pltpu.PrefetchScalarGridSpec(
        num_scalar_prefetch=1,
        grid=(db,),
        in_specs=[pl.BlockSpec((None, None, 2, w, PAGE), lambda b, pt, i=i: (layer, pt[b, i], 0, 0, 0))
                  for i in range(n_pages)]
                 + [pl.BlockSpec((None, None, N_HEADS, PAGE), lambda b, pt, i=i: (layer, pt[b, i], 0, 0))
                    for i in range(n_pages)]
                 + [pl.BlockSpec((None, 1, w), lambda b, pt: (b, 0, 0)),
                    pl.BlockSpec((None, 1, 2 * w), lambda b, pt: (b, 0, 0)),
                    pl.BlockSpec(lft_s.shape, lambda b, pt: (0, 0))],
        out_specs=pl.BlockSpec((None, 1, w), lambda b, pt: (b, 0, 0)),
        scratch_shapes=[pltpu.VMEM((N_HEADS, n_pages * PAGE), F32)],
    )
    return pl.pallas_call(
        functools.partial(_fox_decode_body, n_pages=n_pages),
        grid_spec=grid_spec,
        out_shape=jax.ShapeDtypeStruct((db, 1, w), BF16),
        compiler_params=_cparams("arbitrary"),
        name="fox_decode",
    )(page_table, *([cache_kv] * n_pages), *([cache_lf] * n_pages), fq_s, foxkv_s, lft_s)


def _nsa_decode_body(pt_ref, *refs, n_pages):
    pg = refs[:n_pages]
    (win_ref, q_ref, new_ref, gate_ref, wnew_ref, wa_ref, wb_ref, pea_ref, peb_ref, w2_ref, kg0_ref,
     o_ref, wout_ref, x_ref, s_ref) = refs[n_pages:]
    past = n_pages * PAGE
    wbuf = win_ref.shape[2]
    cur = past // SEL_BLOCK
    n_blk = cur + 1
    bidx = pl.program_id(0)

    for i in range(n_pages):
        x_ref[0, i * PAGE:(i + 1) * PAGE, :] = pg[i][0].T
        x_ref[1, i * PAGE:(i + 1) * PAGE, :] = pg[i][1].T
    kc, vc = _compress(x_ref.at[0], x_ref.at[1], wa_ref, wb_ref, pea_ref, peb_ref, w2_ref, kg0_ref)
    nhb = kc.shape[0]

    q = q_ref[...]
    qf = q.astype(F32)
    row = lax.broadcasted_iota(jnp.int32, (N_HEADS, 1), 0)
    slope = _slope_col(row)
    lane = lax.broadcasted_iota(jnp.int32, (1, LANES), 1)
    g0 = row < N_REP

    end = lax.broadcasted_iota(jnp.int32, (1, nhb), 1) * CMP_STRIDE + (CMP_LEN - 1)
    s = _dot_nt(q, kc.astype(BF16)) - slope * (past - end).astype(F32)
    p = _cmp_softmax(s, jnp.broadcast_to(end <= past, s.shape))
    o_cmp = _dot(p.astype(BF16), vc.astype(BF16))
    imp0 = jnp.sum(jnp.where(g0, p, 0.0), axis=0, keepdims=True)
    imp1 = jnp.sum(jnp.where(g0, 0.0, p), axis=0, keepdims=True)
    imp = jnp.where(g0, imp0, imp1)
    sel = _select_blocks(imp, jnp.full((N_HEADS, 1), cur, jnp.int32), n_blk)
    sel_bf = sel.astype(BF16)

    def new_dot(k_row):
        return jnp.sum(qf * k_row.astype(F32), axis=-1, keepdims=True)

    for i in range(n_pages):
        kpos = i * PAGE + lane
        s = _dot(q, pg[i][2].astype(BF16)) - slope * (past - kpos).astype(F32)
        s_ref[:, i * PAGE:(i + 1) * PAGE] = jnp.where(_expand_sel(sel_bf, i * PAGE, PAGE) > 0.5, s, NEG)
    new_ok = jnp.sum(jnp.where(lane == cur, sel.astype(F32), 0.0), axis=-1, keepdims=True) > 0.5
    s_new = jnp.where(new_ok, new_dot(new_ref[:, 0:LANES]), NEG)
    m = jnp.maximum(jnp.max(s_ref[...], axis=-1, keepdims=True), s_new)
    p_new = jnp.where(new_ok, jnp.exp(s_new - m), 0.0)
    l = p_new
    o_sel = p_new.astype(BF16).astype(F32) * new_ref[:, LANES:2 * LANES].astype(F32)
    for i in range(n_pages):
        sp = s_ref[:, i * PAGE:(i + 1) * PAGE]
        p = jnp.where(sp > 0.5 * NEG, jnp.exp(sp - m), 0.0)
        l = l + jnp.sum(p, axis=-1, keepdims=True)
        o_sel = o_sel + _dot_nt(p.astype(BF16), pg[i][3].astype(BF16))
    o_sel = o_sel / jnp.where(l > 0.0, l, 1.0)

    wk = win_ref[0]
    wv = win_ref[1]
    pos = lax.broadcasted_iota(jnp.int32, (1, wbuf), 1)
    d = wbuf - pos
    s = _dot(q, wk.astype(BF16)) - slope * d.astype(F32)
    s = jnp.where(d < WINDOW, s, NEG)
    s_new = new_dot(new_ref[:, 2 * LANES:3 * LANES])
    m = jnp.maximum(jnp.max(s, axis=-1, keepdims=True), s_new)
    p = jnp.exp(s - m)
    p_new = jnp.exp(s_new - m)
    l = jnp.sum(p, axis=-1, keepdims=True) + p_new
    o_win = (_dot_nt(p.astype(BF16), wv.astype(BF16))
             + p_new.astype(BF16).astype(F32) * new_ref[:, 3 * LANES:4 * LANES].astype(F32)) / l

    o = gate_ref[:, 0:1] * o_cmp + gate_ref[:, 1:2] * o_sel + gate_ref[:, 2:3] * o_win
    in_group = _shr(lane, HEAD_DIM) == _shr(row, N_REP)
    o_ref[...] = jnp.where(in_group, o, 0.0).astype(o_ref.dtype)

    wout_ref[0] = jnp.where(pos == wbuf - 1, _column(wnew_ref[0:LANES, :], bidx), pltpu.roll(wk, wbuf - 1, 1))
    wout_ref[1] = jnp.where(pos == wbuf - 1, _column(wnew_ref[LANES:2 * LANES, :], bidx), pltpu.roll(wv, wbuf - 1, 1))


def _nsa_decode(page_table, cache_nsa, win_state, layer, nq_s, nsab_s, gates_s, wint_s, cw):
    db, n_pages = page_table.shape
    wbuf = win_state.shape[4]
    grid_spec = pltpu.PrefetchScalarGridSpec(
        num_scalar_prefetch=1,
        grid=(db,),
        in_specs=[pl.BlockSpec((None, None, 4, LANES, PAGE), lambda b, pt, i=i: (layer, pt[b, i], 0, 0, 0))
                  for i in range(n_pages)]
                 + [pl.BlockSpec((None, None, 2, LANES, wbuf), lambda b, pt: (layer, b, 0, 0, 0)),
                    pl.BlockSpec((None, N_HEADS, LANES), lambda b, pt: (b, 0, 0)),
                    pl.BlockSpec((None, 1, 512), lambda b, pt: (b, 0, 0)),
                    pl.BlockSpec((None, N_HEADS, 3), lambda b, pt: (b, 0, 0)),
                    pl.BlockSpec(wint_s.shape, lambda b, pt: (0, 0))]
                 + [pl.BlockSpec(a.shape, lambda b, pt, nd=a.ndim: (0,) * nd) for a in cw],
        out_specs=[pl.BlockSpec((None, N_HEADS, LANES), lambda b, pt: (b, 0, 0)),
                   pl.BlockSpec((None, 2, LANES, wbuf), lambda b, pt: (b, 0, 0, 0))],
        scratch_shapes=[pltpu.VMEM((2, n_pages * PAGE, LANES), F32), pltpu.VMEM((N_HEADS, n_pages * PAGE), F32)],
    )
    return pl.pallas_call(
        functools.partial(_nsa_decode_body, n_pages=n_pages),
        grid_spec=grid_spec,
        out_shape=[jax.ShapeDtypeStruct((db, N_HEADS, LANES), BF16),
                   jax.ShapeDtypeStruct((db, 2, LANES, wbuf), F32)],
        compiler_params=_cparams("arbitrary"),
        name="nsa_decode",
    )(page_table, *([cache_nsa] * n_pages), win_state, nq_s, nsab_s, gates_s, wint_s, *cw)


O_F, O_GM, O_NQ, O_NKV, O_NG, O_MG = 1536, 1544, 2568, 3080, 3848, 3872
_HEAD_PLACE = (np.arange(N_HEADS)[:, None] // N_REP == np.arange(N_GROUPS)[None, :]).astype(np.float32)


def _prep_w_in(w):
    d = w.shape[0]
    wq = w[:, O_NQ:O_NKV].reshape(d, N_HEADS, 1, HEAD_DIM)
    wq = (wq * _HEAD_PLACE[None, :, :, None]).reshape(d, N_HEADS * LANES)
    small = jnp.concatenate([w[:, O_F:O_GM], w[:, O_NG:O_MG], jnp.zeros((d, LANES - 32), w.dtype)], axis=1)
    w_tok = jnp.concatenate([w[:, 0:O_F], w[:, O_GM:O_NQ], wq, w[:, O_NKV:O_NG], small], axis=1).astype(BF16)
    wt = w.T
    w_feat = jnp.concatenate([wt[512:O_F], wt[O_NKV:O_NG], wt[O_F:O_GM], jnp.zeros((8, d), w.dtype)],
                             axis=0).astype(BF16)
    return w_tok, w_feat, w[:, O_MG:].astype(BF16)


def _pad_heads(g):
    return (_HEAD_PLACE[:, :, None] * g[None, None, :]).reshape(1, N_HEADS * LANES)


def _prep_compress(w1, w2, pe, kg0):
    eye = np.eye(N_GROUPS, dtype=np.float32)
    w1r = w1.reshape(2, 2, CMP_STRIDE, HEAD_DIM, HEAD_DIM)
    wl = jnp.einsum('chldk,cC,gG->hlcgdCGk', w1r, np.eye(2, dtype=np.float32), eye)
    wl = wl.reshape(2, CMP_STRIDE, 256, 256).astype(BF16)
    w2bd = jnp.einsum('cdk,cC,gG->cgdCGk', w2, np.eye(2, dtype=np.float32), eye).reshape(256, 256).astype(BF16)
    pex = jnp.broadcast_to(pe.transpose(1, 0, 2)[:, :, None, :], (CMP_LEN, 2, N_GROUPS, HEAD_DIM))
    pex = pex.reshape(2, CMP_STRIDE, 1, 256)
    return (wl[0], wl[1], pex[0], pex[1], w2bd, jnp.tile(kg0, 2)[None, :])


def _layer(xp, xs, lw, page_table, caches, layer, dims):
    b, s, db = dims

    xp = _ffn(xp, lw['ffn1_norm'], lw['ffn1_w_gu'], lw['ffn1_w_down'])
    xs = _ffn(xs, lw['ffn1_norm'], lw['ffn1_w_gu'], lw['ffn1_w_down'])

    pw = lw['proj']
    (fq, foxt, foxtb, gmu, gmv, nq, nsat, wint, nsatb, cmpraw, lft, gates) = _proj(xp, pw, b, s, False)
    (fq_s, foxt_s, _, gmu_s, gmv_s, nq_s, nsat_s, wint_s, _, _, lft_s, gates_s,
     foxkv_s, nsab_s) = _proj(xs, pw, 1, db, True)

    c = _cumsum(lft).reshape(b, 4, 2, s)
    o_fox_p = _fox_prompt(fq.reshape(b, s, 512), foxtb, c)
    o_gm_p = _gmlp_prompt(gmu, gmv, lw['gmlp_w_s'], lw['gmlp_bexp'])
    cmpkv = _compress_prompt(cmpraw.reshape(b, s, 256), lw['cw'])
    o_nsa_p = _nsa_prompt(nq.reshape(b, s, 1024), cmpkv, nsatb, gates.reshape(b, s, LANES))
    xp = _merge(xp, lw['mix_norm'], o_fox_p.reshape(b * s, 512), o_gm_p, o_nsa_p.reshape(b * s, 1024),
                lw['w_gate'], lw['wb0'], lw['wb1'], lw['wb2'], lw['w_out'])

    cache_fox_kv, cache_fox_lf, cache_nsa, win_state = caches
    o_fox_s = _fox_decode(page_table, cache_fox_kv, cache_fox_lf, layer, fq_s.reshape(db, 1, 512),
                          foxkv_s.reshape(db, 1, 1024), lft_s.reshape(N_HEADS, db))
    o_gm_s = _gmlp_sample(gmu_s, gmv_s, lw['gmlp_w00'], lw['gmlp_b0'])
    g3 = gates_s[:, 8:32].reshape(db, 3, N_HEADS).transpose(0, 2, 1)
    o_nsa_s, win_next = _nsa_decode(page_table, cache_nsa, win_state, layer, nq_s.reshape(db, N_HEADS, LANES),
                                    nsab_s.reshape(db, 1, 512), g3, wint_s.reshape(256, db), lw['cw'])
    xs = _merge(xs, lw['mix_norm'], o_fox_s.reshape(db, 512), o_gm_s, o_nsa_s.reshape(db, 1024),
                lw['w_gate'], lw['wb0'], lw['wb1'], lw['wb2'], lw['w_out'])

    xp = _ffn(xp, lw['ffn2_norm'], lw['ffn2_w_gu'], lw['ffn2_w_down'])
    xs = _ffn(xs, lw['ffn2_norm'], lw['ffn2_w_gu'], lw['ffn2_w_down'])

    state = dict(
        fox_kv_p=foxt, fox_kv_s=foxt_s, fox_logf_p=lft, fox_logf_s=lft_s, nsa_kv_p=nsat, nsa_kv_s=nsat_s,
        win_p=wint[:, :, s - min(WINDOW, s):], win_s=win_next,
        gmlp_v_p=gmv.reshape(b, s, 512)[:, s - min(CHUNK, s):], gmlp_v_s=gmv_s.reshape(db, 1, 512),
    )
    return xp, xs, state


def kernel(x_prompt, x_sample, page_table, cache_fox_kv, cache_fox_logf, cache_nsa_kv, state_nsa_win, ffn1_norm, ffn1_w_gu, ffn1_w_down, mix_norm, w_in, fox_b_f, fox_qk_gain, gmlp_v_gain, gmlp_w_s, gmlp_b_s, nsa_q_gain, nsa_k_gain, nsa_cmp_pe, nsa_cmp_w1, nsa_cmp_w2, w_branch, w_out, ffn2_norm, ffn2_w_gu, ffn2_w_down):
    b, s, d = x_prompt.shape
    db = x_sample.shape[0]
    depth = w_in.shape[0]
    n_pool = cache_fox_kv.shape[1]
    wb = state_nsa_win.shape[2]
    xp = x_prompt.reshape(b * s, d)
    xs = x_sample.reshape(db, d)

    caches = (cache_fox_kv.transpose(0, 1, 3, 4, 5, 2).reshape(depth, n_pool, 2, N_HEADS * HEAD_DIM, PAGE),
              cache_fox_logf.transpose(0, 1, 3, 2),
              cache_nsa_kv.transpose(0, 1, 3, 4, 5, 2).reshape(depth, n_pool, 4, LANES, PAGE),
              state_nsa_win.transpose(0, 1, 3, 4, 5, 2).reshape(depth, db, 2, LANES, wb))

    nsa_rows = (np.arange(N_HEADS * LANES) % LANES) // HEAD_DIM == (np.arange(N_HEADS * LANES) // LANES) // N_REP
    col = lambda g: g[:, None]
    states = []
    for l in range(depth):
        w_tok, w_feat, w_gate = _prep_w_in(w_in[l])
        wb2 = jnp.zeros((N_HEADS * LANES, d), F32).at[np.nonzero(nsa_rows)[0]].set(w_branch[l, 2])
        proj = (mix_norm[l][None, :], w_tok, w_feat,
                jnp.tile(fox_qk_gain[l, 0], 2)[None, :], jnp.tile(fox_qk_gain[l, 1], 2)[None, :],
                col(fox_qk_gain[l, 1]), gmlp_v_gain[l][None, :], _pad_heads(nsa_q_gain[l]),
                jnp.tile(nsa_k_gain[l, 1], 2)[None, :], jnp.tile(nsa_k_gain[l, 2], 2)[None, :],
                col(nsa_k_gain[l, 1]), col(nsa_k_gain[l, 2]), col(fox_b_f[l]))
        lw = dict(
            ffn1_norm=ffn1_norm[l][None, :], ffn1_w_gu=ffn1_w_gu[l].astype(BF16), ffn1_w_down=ffn1_w_down[l].astype(BF16),
            ffn2_norm=ffn2_norm[l][None, :], ffn2_w_gu=ffn2_w_gu[l].astype(BF16), ffn2_w_down=ffn2_w_down[l].astype(BF16),
            mix_norm=mix_norm[l][None, :], proj=proj, w_gate=w_gate,
            gmlp_w_s=gmlp_w_s[l], gmlp_bexp=jnp.repeat(gmlp_b_s[l].T, HEAD_DIM, axis=1),
            gmlp_w00=jnp.repeat(gmlp_w_s[l, :, 0, 0], HEAD_DIM)[None, :],
            gmlp_b0=jnp.repeat(gmlp_b_s[l, :, 0], HEAD_DIM)[None, :],
            cw=_prep_compress(nsa_cmp_w1[l], nsa_cmp_w2[l], nsa_cmp_pe[l], nsa_k_gain[l, 0]),
            wb0=w_branch[l, 0].astype(BF16), wb1=w_branch[l, 1].astype(BF16), wb2=wb2.astype(BF16),
            w_out=w_out[l].astype(BF16),
        )
        xp, xs, st = _layer(xp, xs, lw, page_table, caches, l, (b, s, db))
        states.append(st)

    def stack(name):
        return jnp.stack([st[name] for st in states])

    def tok_major(a, *feat_dims):
        return jnp.moveaxis(a.reshape(a.shape[:2] + feat_dims + a.shape[3:]), -1, 2)

    def tok_major_s(a, *feat_dims):
        return jnp.swapaxes(tok_major(a, *feat_dims), 1, 2)

    return (xp.reshape(b, s, d), xs.reshape(db, 1, d),
            tok_major(stack('fox_kv_p'), 2, N_HEADS, HEAD_DIM), tok_major_s(stack('fox_kv_s'), 2, N_HEADS, HEAD_DIM),
            tok_major(stack('fox_logf_p'), N_HEADS), tok_major_s(stack('fox_logf_s'), N_HEADS),
            tok_major(stack('nsa_kv_p'), 4, N_GROUPS, HEAD_DIM), tok_major_s(stack('nsa_kv_s'), 4, N_GROUPS, HEAD_DIM),
            tok_major(stack('win_p'), 2, N_GROUPS, HEAD_DIM),
            tok_major(stack('win_s').reshape(depth, db, 2 * LANES, wb), 2, N_GROUPS, HEAD_DIM),
            stack('gmlp_v_p'), stack('gmlp_v_s'))
```

```python
import functools

import jax
import jax.numpy as jnp
import numpy as np
from jax import lax
from jax.experimental import pallas as pl
from jax.experimental.pallas import tpu as pltpu

F32 = jnp.float32
BF16 = jnp.bfloat16

HEAD_DIM = 64
LANES = 128
PAGE = 128
CHUNK = 128
Q_BLOCK = 128
FOX_TQ = 256
ATT_TK = 512
CMP_LEN = 32
CMP_STRIDE = 16
SEL_BLOCK = 64
N_SEL = 8
WINDOW = 512
N_GROUPS = 2
N_REP = 4
N_HEADS = 8
FORCE_BONUS = 1.0e4
NEG = -1.0e30
EPS = 1e-6
SCALE = HEAD_DIM ** -0.5
VMEM_LIMIT = 56 * 1024 * 1024
HI = lax.Precision.HIGHEST


def _cparams(*sem):
    return pltpu.CompilerParams(dimension_semantics=sem, vmem_limit_bytes=VMEM_LIMIT)


def _dot(a, b, precision=None):
    return jnp.dot(a, b, preferred_element_type=F32, precision=precision)


def _dot_nt(a, b):
    return lax.dot_general(a, b, (((1,), (1,)), ((), ())), preferred_element_type=F32)


def _shr(x, pow2):
    return jnp.right_shift(x, int(pow2).bit_length() - 1)


def _rms_rows(x, g):
    ms = jnp.mean(x * x, axis=-1, keepdims=True)
    return x * lax.rsqrt(ms + EPS) * g


def _headnorm_pair(zb, gain):
    lo = lax.broadcasted_iota(jnp.int32, zb.shape, 1) < HEAD_DIM
    sq = zb * zb
    s_lo = jnp.sum(jnp.where(lo, sq, 0.0), axis=-1, keepdims=True)
    s_hi = jnp.sum(jnp.where(lo, 0.0, sq), axis=-1, keepdims=True)
    ms = jnp.where(lo, s_lo, s_hi) * (1.0 / HEAD_DIM)
    return zb * lax.rsqrt(ms + EPS) * gain


def _headnorm_cols(zt, gain_col):
    ms = jnp.mean(zt * zt, axis=0, keepdims=True)
    return zt * lax.rsqrt(ms + EPS) * gain_col


def _silu(x):
    return x * jax.nn.sigmoid(x)


def _gelu_tanh(x):
    return 0.5 * x * (1.0 + jnp.tanh(np.sqrt(2.0 / np.pi) * (x + 0.044715 * (x * x * x))))


def _log_sigmoid(x):
    return jnp.minimum(x, 0.0) - jnp.log1p(jnp.exp(-jnp.abs(x)))


def _token_tile(t):
    for tm in (512, 384, 256, 128):
        if t % tm == 0:
            return tm
    raise ValueError(f"token count {t} is not a multiple of 128")


def _const_spec(shape):
    nd = len(shape)
    return pl.BlockSpec(shape, lambda *_: (0,) * nd, pipeline_mode=pl.Buffered(1))


def _ffn_body(x_ref, g_ref, wgu_ref, wd_ref, o_ref, *, d_ff, fc):
    x = x_ref[...]
    h = _rms_rows(x, g_ref[...]).astype(BF16)
    acc = jnp.zeros_like(x)
    for c in range(d_ff // fc):
        g = _dot(h, wgu_ref[:, c * fc:(c + 1) * fc])
        u = _dot(h, wgu_ref[:, d_ff + c * fc:d_ff + (c + 1) * fc])
        a = (_silu(g) * u).astype(BF16)
        acc = acc + _dot(a, wd_ref[c * fc:(c + 1) * fc, :])
    o_ref[...] = x + 0.5 * acc


def _ffn(x, g, wgu, wd):
    t, d = x.shape
    d_ff = wd.shape[0]
    tm = _token_tile(t)
    fc = 256 if d_ff % 256 == 0 else 128
    return pl.pallas_call(
        functools.partial(_ffn_body, d_ff=d_ff, fc=fc),
        grid=(t // tm,),
        in_specs=[pl.BlockSpec((tm, d), lambda i: (i, 0)),
                  _const_spec((1, d)),
                  _const_spec((d, 2 * d_ff)),
                  _const_spec((d_ff, d))],
        out_specs=pl.BlockSpec((tm, d), lambda i: (i, 0)),
        out_shape=jax.ShapeDtypeStruct((t, d), F32),
        compiler_params=_cparams("parallel"),
        name="ffn",
    )(x, g, wgu, wd)


C_FOX = 0
C_GM = 1536
C_NQ = 2560
C_NKV = 3584
C_SMALL = 4352
C_END = 4480
R_NKV = 1024
R_F = 1792
R_END = 1808


def _proj_body(x_ref, g_ref, w_ref, wt_ref, gq_ref, gk_ref, gkc_ref, gv_ref, gnq_ref, gk1_ref, gk2_ref,
               gk1c_ref, gk2c_ref, bfc_ref, *out_refs, sample):
    (fq_ref, foxt_ref, foxtb_ref, gmu_ref, gmv_ref, nq_ref, nsat_ref, wint_ref, nsatb_ref, cmpraw_ref,
     lft_ref, gate_ref) = out_refs[:12]
    h = _rms_rows(x_ref[...], g_ref[...]).astype(BF16)

    z = _dot(h, w_ref[:, 0:512])
    for j in range(4):
        sl = slice(j * LANES, (j + 1) * LANES)
        fq_ref[:, sl] = (_headnorm_pair(z[:, sl], gq_ref[...]) * SCALE).astype(BF16)
    zt = _dot_nt(wt_ref[0:R_NKV, :], h)
    for hh in range(N_HEADS):
        rs = slice(hh * HEAD_DIM, (hh + 1) * HEAD_DIM)
        kt = _headnorm_cols(zt[rs], gkc_ref[...])
        foxt_ref[rs, :] = kt
        foxtb_ref[rs, :] = kt.astype(BF16)
    foxt_ref[512:1024, :] = zt[512:1024]
    foxtb_ref[512:1024, :] = zt[512:1024].astype(BF16)

    a = _gelu_tanh(_dot(h, w_ref[:, C_GM:C_NQ]))
    gmu_ref[...] = a[:, 0:512]
    gmv_ref[...] = _rms_rows(a[:, 512:1024], gv_ref[...])

    z = _dot(h, w_ref[:, C_NQ:C_NKV])
    for hh in range(N_HEADS):
        sl = slice(hh * LANES, (hh + 1) * LANES)
        zb = z[:, sl]
        ms = jnp.sum(zb * zb, axis=-1, keepdims=True) * (1.0 / HEAD_DIM)
        nq_ref[:, sl] = (zb * lax.rsqrt(ms + EPS) * (gnq_ref[:, sl] * SCALE)).astype(BF16)

    zt = _dot_nt(wt_ref[R_NKV:R_F, :], h)
    nsat_ref[0:256, :] = zt[0:256]
    nsat_ref[384:512, :] = zt[384:512]
    nsatb_ref[128:256, :] = zt[384:512].astype(BF16)
    wint_ref[128:256, :] = zt[640:768]
    nsatb_ref[384:512, :] = zt[640:768].astype(BF16)
    for g in range(N_GROUPS):
        sk = _headnorm_cols(zt[256 + g * HEAD_DIM:256 + (g + 1) * HEAD_DIM], gk1c_ref[...])
        nsat_ref[256 + g * HEAD_DIM:256 + (g + 1) * HEAD_DIM, :] = sk
        nsatb_ref[g * HEAD_DIM:(g + 1) * HEAD_DIM, :] = sk.astype(BF16)
        wk = _headnorm_cols(zt[512 + g * HEAD_DIM:512 + (g + 1) * HEAD_DIM], gk2c_ref[...])
        wint_ref[g * HEAD_DIM:(g + 1) * HEAD_DIM, :] = wk
        nsatb_ref[256 + g * HEAD_DIM:256 + (g + 1) * HEAD_DIM, :] = wk.astype(BF16)

    cmpraw_ref[...] = _dot(h, w_ref[:, C_NKV:C_NKV + 256])

    zt = _dot_nt(wt_ref[R_F:R_END, :], h)
    lft_ref[...] = _log_sigmoid(zt[0:N_HEADS] + bfc_ref[...])
    gate_ref[...] = jax.nn.sigmoid(_dot(h, w_ref[:, C_SMALL:C_END]))

    if sample:
        foxs_ref, nsabs_ref = out_refs[12:]
        z = _dot(h, w_ref[:, 512:C_GM])
        for j in range(4):
            sl = slice(j * LANES, (j + 1) * LANES)
            foxs_ref[:, sl] = _headnorm_pair(z[:, sl], gk_ref[...])
        foxs_ref[:, 512:1024] = z[:, 512:1024]
        z = _dot(h, w_ref[:, C_NKV + 256:C_SMALL])
        nsabs_ref[:, 0:128] = _headnorm_pair(z[:, 0:128], gk1_ref[...]).astype(BF16)
        nsabs_ref[:, 128:256] = z[:, 128:256].astype(BF16)
        nsabs_ref[:, 256:384] = _headnorm_pair(z[:, 256:384], gk2_ref[...]).astype(BF16)
        nsabs_ref[:, 384:512] = z[:, 384:512].astype(BF16)


def _proj(x, pw, b, s, sample):
    t, d = x.shape
    tm = _token_tile(s)
    nst = s // tm
    tok = lambda n, dt: (pl.BlockSpec((tm, n), lambda i: (i, 0)), jax.ShapeDtypeStruct((t, n), dt))
    feat = lambda n, dt: (pl.BlockSpec((None, n, tm), lambda i: (i // nst, 0, i % nst)),
                          jax.ShapeDtypeStruct((b, n, s), dt))
    outs = [tok(512, BF16), feat(1024, F32), feat(1024, BF16), tok(512, F32), tok(512, F32), tok(1024, BF16),
            feat(512, F32), feat(256, F32), feat(512, BF16), tok(256, F32), feat(N_HEADS, F32), tok(LANES, F32)]
    if sample:
        outs += [tok(1024, F32), tok(512, BF16)]
    return pl.pallas_call(
        functools.partial(_proj_body, sample=sample),
        grid=(t // tm,),
        in_specs=[pl.BlockSpec((tm, d), lambda i: (i, 0))] + [_const_spec(a.shape) for a in pw],
        out_specs=[o[0] for o in outs],
        out_shape=[o[1] for o in outs],
        compiler_params=_cparams("parallel"),
        name="in_proj_sample" if sample else "in_proj",
    )(x, *pw)


def _cumsum_body(x_ref, o_ref, *, blk):
    s = x_ref.shape[1]
    r = lax.broadcasted_iota(jnp.int32, (blk, blk), 0)
    c = lax.broadcasted_iota(jnp.int32, (blk, blk), 1)
    tri = (r <= c).astype(F32)
    carry = jnp.zeros((x_ref.shape[0], 1), F32)
    for i in range(s // blk):
        cs = _dot(x_ref[:, i * blk:(i + 1) * blk], tri, precision=HI) + carry
        o_ref[:, i * blk:(i + 1) * blk] = cs
        carry = cs[:, blk - 1:blk]


def _cumsum(x):
    b, n, s = x.shape
    return pl.pallas_call(
        functools.partial(_cumsum_body, blk=min(256, s)),
        grid=(b,),
        in_specs=[pl.BlockSpec((None, n, s), lambda i: (i, 0, 0))],
        out_specs=pl.BlockSpec((None, n, s), lambda i: (i, 0, 0)),
        out_shape=jax.ShapeDtypeStruct((b, n, s), F32),
        compiler_params=_cparams("parallel"),
        name="logf_cumsum",
    )(x)


def _fox_body(q_ref, k_ref, v_ref, c_ref, o_ref, *, tq, tk):
    qi = pl.program_id(2)
    q = q_ref[...]
    lo = lax.broadcasted_iota(jnp.int32, (tq, LANES), 1) < HEAD_DIM
    zero = jnp.zeros_like(q)
    qs = jnp.concatenate([jnp.where(lo, q, zero), jnp.where(lo, zero, q)], axis=0)
    qpos = qi * tq + lax.broadcasted_iota(jnp.int32, (tq, 1), 0)

    def block(kb, carry, masked):
        t0 = pl.multiple_of(kb * tk, tk)
        m_old, l_old, acc = carry
        bias = -c_ref[:, pl.ds(t0, tk)]
        s = _dot(qs, k_ref[:, pl.ds(t0, tk)]).reshape(2, tq, tk) + bias[:, None, :]
        if masked:
            kpos = t0 + lax.broadcasted_iota(jnp.int32, (1, tk), 1)
            s = jnp.where((kpos <= qpos)[None], s, NEG)
        s = s.reshape(2 * tq, tk)
        m_new = jnp.maximum(m_old, jnp.max(s, axis=-1, keepdims=True))
        alpha = jnp.exp(m_old - m_new)
        p = jnp.exp(s - m_new)
        l_new = alpha * l_old + jnp.sum(p, axis=-1, keepdims=True)
        return m_new, l_new, alpha * acc + _dot_nt(p.astype(BF16), v_ref[:, pl.ds(t0, tk)])

    n_kb = _shr(qi * tq + tq + tk - 1, tk)
    carry = (jnp.full((2 * tq, 1), NEG, F32), jnp.zeros((2 * tq, 1), F32), jnp.zeros((2 * tq, LANES), F32))
    carry = lax.fori_loop(0, n_kb - 1, lambda kb, c: block(kb, c, False), carry)
    _, l, acc = block(n_kb - 1, carry, True)
    o = acc / l
    o_ref[...] = jnp.where(lo, o[0:tq], o[tq:2 * tq]).astype(o_ref.dtype)


def _fox_prompt(fq, foxtb, c):
    b, s, _ = fq.shape
    tq = min(FOX_TQ, s)
    tk = min(ATT_TK, s)
    return pl.pallas_call(
        functools.partial(_fox_body, tq=tq, tk=tk),
        grid=(b, 4, s // tq),
        in_specs=[pl.BlockSpec((None, tq, LANES), lambda bi, p, qi: (bi, qi, p)),
                  pl.BlockSpec((None, LANES, s), lambda bi, p, qi: (bi, p, 0)),
                  pl.BlockSpec((None, LANES, s), lambda bi, p, qi: (bi, 4 + p, 0)),
                  pl.BlockSpec((None, None, 2, s), lambda bi, p, qi: (bi, p, 0, 0))],
        out_specs=pl.BlockSpec((None, tq, LANES), lambda bi, p, qi: (bi, qi, p)),
        out_shape=jax.ShapeDtypeStruct((b, s, 512), BF16),
        compiler_params=_cparams("parallel", "parallel", "arbitrary"),
        name="fox_prompt",
    )(fq, foxtb, foxtb, c)


def _gmlp_body(u_ref, v_ref, ws_ref, b_ref, o_ref):
    n = ws_ref.shape[1]
    r = lax.broadcasted_iota(jnp.int32, (n, n), 0)
    c = lax.broadcasted_iota(jnp.int32, (n, n), 1)
    lo = lax.broadcasted_iota(jnp.int32, (n, LANES), 1) < HEAD_DIM
    for j in range(4):
        sl = slice(j * LANES, (j + 1) * LANES)
        vp = v_ref[:, sl].astype(BF16)
        w0 = jnp.where(r >= c, ws_ref[2 * j], 0.0).astype(BF16)
        w1 = jnp.where(r >= c, ws_ref[2 * j + 1], 0.0).astype(BF16)
        mixed = jnp.where(lo, _dot(w0, vp), _dot(w1, vp)) + b_ref[:, sl]
        o_ref[:, sl] = (u_ref[:, sl] * mixed).astype(o_ref.dtype)


def _gmlp_prompt(gmu, gmv, ws, bexp):
    t = gmu.shape[0]
    return pl.pallas_call(
        _gmlp_body,
        grid=(t // CHUNK,),
        in_specs=[pl.BlockSpec((CHUNK, 512), lambda i: (i, 0)),
                  pl.BlockSpec((CHUNK, 512), lambda i: (i, 0)),
                  _const_spec(ws.shape), _const_spec(bexp.shape)],
        out_specs=pl.BlockSpec((CHUNK, 512), lambda i: (i, 0)),
        out_shape=jax.ShapeDtypeStruct((t, 512), BF16),
        compiler_params=_cparams("parallel"),
        name="gmlp_prompt",
    )(gmu, gmv, ws, bexp)


def _gmlp_first_body(u_ref, v_ref, w_ref, b_ref, o_ref):
    o_ref[...] = (u_ref[...] * (v_ref[...] * w_ref[...] + b_ref[...])).astype(o_ref.dtype)


def _gmlp_sample(gmu, gmv, wrow, brow):
    n = gmu.shape[0]
    full = pl.BlockSpec((n, 512), lambda i: (0, 0))
    return pl.pallas_call(
        _gmlp_first_body,
        grid=(1,),
        in_specs=[full, full, _const_spec((1, 512)), _const_spec((1, 512))],
        out_specs=full,
        out_shape=jax.ShapeDtypeStruct((n, 512), BF16),
        compiler_params=_cparams("arbitrary"),
        name="gmlp_sample",
    )(gmu, gmv, wrow, brow)


def _compress(xk_ref, xv_ref, wa_ref, wb_ref, pea_ref, peb_ref, w2_ref, kg0_ref):
    nhb = xk_ref.shape[0] // CMP_STRIDE
    a = jnp.zeros((nhb, 256), F32)
    b = jnp.zeros((nhb, 256), F32)
    for l in range(CMP_STRIDE):
        xl = jnp.concatenate([xk_ref[pl.ds(l, nhb, stride=CMP_STRIDE), :],
                              xv_ref[pl.ds(l, nhb, stride=CMP_STRIDE), :]], axis=1)
        a = a + _dot((xl + pea_ref[l]).astype(BF16), wa_ref[l])
        b = b + _dot((xl + peb_ref[l]).astype(BF16), wb_ref[l])
    pre = a + pltpu.roll(b, nhb - 1, 0)
    out = _dot(_silu(pre).astype(BF16), w2_ref[...])
    kc = _headnorm_pair(out[:, 0:LANES], kg0_ref[...])
    return kc, out[:, LANES:2 * LANES]


def _compress_body(xk_ref, xv_ref, wa_ref, wb_ref, pea_ref, peb_ref, w2_ref, kg0_ref, o_ref):
    kc, vc = _compress(xk_ref, xv_ref, wa_ref, wb_ref, pea_ref, peb_ref, w2_ref, kg0_ref)
    o_ref[:, 0:LANES] = kc.astype(BF16)
    o_ref[:, LANES:2 * LANES] = vc.astype(BF16)


def _compress_prompt(cmpraw, cw):
    b, s, _ = cmpraw.shape
    nhb = s // CMP_STRIDE
    return pl.pallas_call(
        _compress_body,
        grid=(b,),
        in_specs=[pl.BlockSpec((None, s, LANES), lambda i: (i, 0, 0)),
                  pl.BlockSpec((None, s, LANES), lambda i: (i, 0, 1))] + [_const_spec(a.shape) for a in cw],
        out_specs=pl.BlockSpec((None, nhb, 256), lambda i: (i, 0, 0)),
        out_shape=jax.ShapeDtypeStruct((b, nhb, 256), BF16),
        compiler_params=_cparams("parallel"),
        name="nsa_compress",
    )(cmpraw, cmpraw, *cw)


def _slope_col(head):
    out = jnp.zeros(head.shape, F32)
    for hh in range(N_HEADS):
        out = jnp.where(head == hh, 2.0 ** (-(hh + 1)), out)
    return out


def _slope_rows(g):
    return jnp.concatenate([jnp.full((Q_BLOCK, 1), 2.0 ** (-(N_REP * g + r + 1)), F32) for r in range(N_REP)], axis=0)


def _cmp_softmax(s, mask):
    s = jnp.where(mask, s, NEG)
    m = jnp.max(s, axis=-1, keepdims=True)
    e = jnp.where(mask, jnp.exp(s - m), 0.0)
    l = jnp.sum(e, axis=-1, keepdims=True)
    return e / jnp.where(l > 0.0, l, 1.0)


def _select_blocks(imp, cur, n_blk):
    m, n = imp.shape
    pool = (_shr(lax.broadcasted_iota(jnp.int32, (n, LANES), 0), SEL_BLOCK // CMP_STRIDE)
            == lax.broadcasted_iota(jnp.int32, (n, LANES), 1)).astype(F32)
    score = _dot(imp, pool, precision=HI)
    j = lax.broadcasted_iota(jnp.int32, (m, LANES), 1)
    forced = (j == 0) | (j == cur) | (j == cur - 1)
    score = jnp.where(forced, score + FORCE_BONUS, score)
    score = jnp.where(j > cur, NEG, score)
    rank = jnp.zeros((m, LANES), F32)
    for i in range(n_blk):
        ci = score[:, i:i + 1]
        ahead = (ci > score) | ((ci == score) & (j > i))
        rank = rank + ahead.astype(F32)
    return (rank < float(min(N_SEL, n_blk))) & (score > 0.5 * NEG)


def _expand_sel(sel_bf, t0, n):
    jj = lax.broadcasted_iota(jnp.int32, (LANES, n), 0)
    tt = t0 + lax.broadcasted_iota(jnp.int32, (LANES, n), 1)
    e = (_shr(tt, SEL_BLOCK) == jj).astype(BF16)
    return _dot(sel_bf, e)


AUG_SEL = 16
MAX_SEL_BLOCKS = 32


def _aug_rows(pos):
    n = pos.shape[1]
    r = lax.broadcasted_iota(jnp.int32, (HEAD_DIM, n), 0)
    onehot = (r >= AUG_SEL) & (_shr(pos, SEL_BLOCK) == r - AUG_SEL)
    return jnp.where(r == 0, _shr(pos, LANES).astype(F32),
                     jnp.where(r == 1, (pos & (LANES - 1)).astype(F32), onehot.astype(F32)))


def _select_bias_t(imp, qi, off, n_blk):
    n = imp.shape[1]
    base = off + AUG_SEL
    prow = lax.broadcasted_iota(jnp.int32, (LANES, n), 0) - base
    pool_t = (prow == _shr(lax.broadcasted_iota(jnp.int32, (LANES, n), 1), SEL_BLOCK // CMP_STRIDE)).astype(F32)
    score = lax.dot_general(pool_t, imp, (((1,), (1,)), ((), ())), preferred_element_type=F32,
                            precision=HI)[base:base + MAX_SEL_BLOCKS]
    j = lax.broadcasted_iota(jnp.int32, (MAX_SEL_BLOCKS, 1), 0)
    cur = _shr(qi * Q_BLOCK + lax.broadcasted_iota(jnp.int32, (1, Q_BLOCK), 1), SEL_BLOCK)
    forced = (j == 0) | (j == cur) | (j == cur - 1)
    score = jnp.where(forced, score + FORCE_BONUS, score)
    score = jnp.where(j > cur, NEG, score)
    rank = jnp.zeros(score.shape, F32)
    for i in range(n_blk):
        ci = score[i:i + 1, :]
        ahead = (ci > score) | ((ci == score) & (j > i))
        rank = rank + ahead.astype(F32)
    chosen = (rank < float(min(N_SEL, n_blk))) & (score > 0.5 * NEG)
    bias_t = jnp.where(chosen, 0.0, NEG)
    parts = [bias_t, jnp.zeros((LANES - base - MAX_SEL_BLOCKS, Q_BLOCK), F32)]
    if base:
        parts = [jnp.zeros((base, Q_BLOCK), F32)] + parts
    return jnp.concatenate(parts, axis=0).T


def _nsa_body(nq_ref, cmp_ref, kv_ref, aug_ref, gate_ref, o_ref, *, n_blk, tk, wlen):
    qi = pl.program_id(1)
    s_len = kv_ref.shape[1]
    rows = N_REP * Q_BLOCK
    row = lax.broadcasted_iota(jnp.int32, (rows, 1), 0)
    qpos = qi * Q_BLOCK + (row & (Q_BLOCK - 1))
    qpos_q = qi * Q_BLOCK + lax.broadcasted_iota(jnp.int32, (Q_BLOCK, 1), 0)
    lane = lax.broadcasted_iota(jnp.int32, (1, LANES), 1)
    nhb = cmp_ref.shape[0]
    end_col = lax.broadcasted_iota(jnp.int32, (nhb, 1), 0) * CMP_STRIDE + (CMP_LEN - 1)
    end = lax.broadcasted_iota(jnp.int32, (1, nhb), 1) * CMP_STRIDE + (CMP_LEN - 1)

    n_kb = _shr(qi * Q_BLOCK + Q_BLOCK + tk - 1, tk)
    t_last = pl.multiple_of((n_kb - 1) * tk, tk)
    causal = jnp.where(t_last + lax.broadcasted_iota(jnp.int32, (1, tk), 1) <= qpos_q, 0.0, NEG)
    t_win = pl.multiple_of(jnp.clip((qi - WINDOW // Q_BLOCK) * Q_BLOCK, 0, s_len - wlen), Q_BLOCK)
    d_win = qpos_q - (t_win + lax.broadcasted_iota(jnp.int32, (1, wlen), 1))
    band = jnp.where((d_win >= 0) & (d_win < WINDOW), 0.0, NEG)

    def masked(s, bias):
        n = s.shape[1]
        return (s.reshape(N_REP, Q_BLOCK, n) + bias[None]).reshape(rows, n)

    for g in range(N_GROUPS):
        off = HEAD_DIM * (1 - g)
        q = jnp.concatenate([nq_ref[:, (N_REP * g + r) * LANES:(N_REP * g + r + 1) * LANES]
                             for r in range(N_REP)], axis=0).astype(F32)
        slope = _slope_rows(g)
        q_alibi = q + jnp.where(lane == off, slope * float(LANES), 0.0) + jnp.where(lane == off + 1, slope, 0.0)
        q_win = q_alibi.astype(BF16)

        def keys(lo, t0, n):
            k = kv_ref[lo + g * HEAD_DIM:lo + (g + 1) * HEAD_DIM, pl.ds(t0, n)]
            a = aug_ref[:, pl.ds(t0, n)]
            return jnp.concatenate([k, a] if g == 0 else [a, k], axis=0)

        in_g = _shr(lane, HEAD_DIM) == g
        kc = cmp_ref[:, 0:LANES].astype(F32)
        kc = jnp.where(in_g, kc, jnp.where(lane == off, _shr(end_col, LANES).astype(F32),
                                           jnp.where(lane == off + 1, (end_col & (LANES - 1)).astype(F32), 0.0)))
        p = _cmp_softmax(_dot_nt(q_win, kc.astype(BF16)), end <= qpos)
        o_cmp = _dot(p.astype(BF16), cmp_ref[:, LANES:2 * LANES])
        imp = p[0:Q_BLOCK]
        for r in range(1, N_REP):
            imp = imp + p[r * Q_BLOCK:(r + 1) * Q_BLOCK]
        sel_bias = _select_bias_t(imp, qi, off, n_blk)
        q_sel = (q_alibi + jnp.concatenate([sel_bias] * N_REP, axis=0)).astype(BF16)

        def sel_tile(t0, carry, last):
            m_old, l_old, acc = carry
            s = _dot(q_sel, keys(0, t0, tk))
            if last:
                s = masked(s, causal)
            v = kv_ref[LANES:2 * LANES, pl.ds(t0, tk)]
            m_new = jnp.maximum(m_old, jnp.max(s, axis=-1, keepdims=True))
            alpha = jnp.exp(m_old - m_new)
            p = jnp.exp(s - m_new)
            l_new = alpha * l_old + jnp.sum(p, axis=-1, keepdims=True)
            return m_new, l_new, alpha * acc + _dot_nt(p.astype(BF16), v)

        carry = (jnp.full((rows, 1), NEG, F32), jnp.zeros((rows, 1), F32), jnp.zeros((rows, LANES), F32))
        carry = lax.fori_loop(0, n_kb - 1, lambda kb, c: sel_tile(pl.multiple_of(kb * tk, tk), c, False), carry)
        _, l_sel, acc_sel = sel_tile(t_last, carry, True)
        o_sel = acc_sel / l_sel

        s = masked(_dot(q_win, keys(2 * LANES, t_win, wlen)), band)
        p = jnp.exp(s - jnp.max(s, axis=-1, keepdims=True))
        v = kv_ref[3 * LANES:4 * LANES, pl.ds(t_win, wlen)]
        o_win = _dot_nt(p.astype(BF16), v) / jnp.sum(p, axis=-1, keepdims=True)

        for r in range(N_REP):
            hh = N_REP * g + r
            rs = slice(r * Q_BLOCK, (r + 1) * Q_BLOCK)
            o = (gate_ref[:, 8 + hh:9 + hh] * o_cmp[rs] + gate_ref[:, 16 + hh:17 + hh] * o_sel[rs]
                 + gate_ref[:, 24 + hh:25 + hh] * o_win[rs])
            o_ref[:, hh * LANES:(hh + 1) * LANES] = jnp.where(in_g, o, 0.0).astype(o_ref.dtype)


def _nsa_aug_body(o_ref):
    o_ref[...] = _aug_rows(lax.broadcasted_iota(jnp.int32, (1, o_ref.shape[1]), 1)).astype(o_ref.dtype)


def _nsa_aug(s):
    return pl.pallas_call(
        _nsa_aug_body,
        out_shape=jax.ShapeDtypeStruct((HEAD_DIM, s), BF16),
        name="nsa_aug",
    )()


def _nsa_prompt(nq, cmpkv, nsatb, aug, gates):
    b, s, _ = nq.shape
    nhb = cmpkv.shape[1]
    n_blk = s // SEL_BLOCK
    assert n_blk <= MAX_SEL_BLOCKS and s // LANES <= 256
    return pl.pallas_call(
        functools.partial(_nsa_body, n_blk=n_blk, tk=min(ATT_TK, s), wlen=min(WINDOW + Q_BLOCK, s)),
        grid=(b, s // Q_BLOCK),
        in_specs=[pl.BlockSpec((None, Q_BLOCK, 1024), lambda bi, qi: (bi, qi, 0)),
                  pl.BlockSpec((None, nhb, 256), lambda bi, qi: (bi, 0, 0)),
                  pl.BlockSpec((None, 512, s), lambda bi, qi: (bi, 0, 0)),
                  pl.BlockSpec((HEAD_DIM, s), lambda bi, qi: (0, 0)),
                  pl.BlockSpec((None, Q_BLOCK, LANES), lambda bi, qi: (bi, qi, 0))],
        out_specs=pl.BlockSpec((None, Q_BLOCK, 1024), lambda bi, qi: (bi, qi, 0)),
        out_shape=jax.ShapeDtypeStruct((b, s, 1024), BF16),
        compiler_params=_cparams("parallel", "arbitrary"),
        name="nsa_prompt",
    )(nq, cmpkv, nsatb, aug, gates)


def _merge_body(x_ref, g_ref, of_ref, og_ref, on_ref, wg_ref, wb0_ref, wb1_ref, wb2_ref, wo_ref, o_ref):
    x = x_ref[...]
    d = x.shape[1]
    h = _rms_rows(x, g_ref[...]).astype(BF16)
    y = jax.nn.sigmoid(_dot(h, wg_ref[:, 0:d])) * _dot(of_ref[...], wb0_ref[...])
    y = y + jax.nn.sigmoid(_dot(h, wg_ref[:, d:2 * d])) * _dot(og_ref[...], wb1_ref[...])
    y = y + jax.nn.sigmoid(_dot(h, wg_ref[:, 2 * d:3 * d])) * _dot(on_ref[...], wb2_ref[...])
    o_ref[...] = x + _dot(y.astype(BF16), wo_ref[...])


def _merge(x, g, o_fox, o_gm, o_nsa, wg, wb0, wb1, wb2, wo):
    t, d = x.shape
    tm = _token_tile(t)
    row = lambda n: pl.BlockSpec((tm, n), lambda i: (i, 0))
    consts = [wg, wb0, wb1, wb2, wo]
    return pl.pallas_call(
        _merge_body,
        grid=(t // tm,),
        in_specs=[row(d), _const_spec((1, d)), row(512), row(512), row(1024)] + [_const_spec(a.shape) for a in consts],
        out_specs=row(d),
        out_shape=jax.ShapeDtypeStruct((t, d), F32),
        compiler_params=_cparams("parallel"),
        name="merge",
    )(x, g, o_fox, o_gm, o_nsa, *consts)


def _column(x, idx):
    lane = lax.broadcasted_iota(jnp.int32, x.shape, 1)
    return jnp.sum(jnp.where(lane == idx, x, 0.0), axis=1, keepdims=True)


def _fox_decode_body(pt_ref, *refs, n_pages):
    kv = refs[:n_pages]
    lf = refs[n_pages:2 * n_pages]
    q_ref, new_ref, lfn_ref, o_ref, s_ref = refs[2 * n_pages:]
    w = N_HEADS * HEAD_DIM
    row = lax.broadcasted_iota(jnp.int32, (N_HEADS, w), 0)
    lane = lax.broadcasted_iota(jnp.int32, (N_HEADS, w), 1)
    diag = _shr(lane, HEAD_DIM) == row
    qbd = jnp.where(diag, jnp.broadcast_to(q_ref[...].astype(F32), (N_HEADS, w)), 0.0).astype(BF16)
    upper = (lax.broadcasted_iota(jnp.int32, (PAGE, PAGE), 0)
             > lax.broadcasted_iota(jnp.int32, (PAGE, PAGE), 1)).astype(F32)

    later = _column(lfn_ref[...], pl.program_id(0))
    for i in reversed(range(n_pages)):
        s = _dot(qbd, kv[i][0].astype(BF16))
        s_ref[:, i * PAGE:(i + 1) * PAGE] = s + _dot(lf[i][...], upper, precision=HI) + later
        later = later + jnp.sum(lf[i][...], axis=-1, keepdims=True)

    k_new = new_ref[:, 0:w].astype(BF16).astype(F32)
    v_new = new_ref[:, w:2 * w].astype(BF16).astype(F32)
    s_new = jnp.sum(qbd.astype(F32) * k_new, axis=-1, keepdims=True)
    m = jnp.maximum(jnp.max(s_ref[...], axis=-1, keepdims=True), s_new)
    p_new = jnp.exp(s_new - m)
    l = p_new
    o = p_new.astype(BF16).astype(F32) * v_new
    for i in range(n_pages):
        p = jnp.exp(s_ref[:, i * PAGE:(i + 1) * PAGE] - m)
        l = l + jnp.sum(p, axis=-1, keepdims=True)
        o = o + _dot_nt(p.astype(BF16), kv[i][1].astype(BF16))
    o_ref[...] = jnp.sum(jnp.where(diag, o / l, 0.0), axis=0, keepdims=True).astype(o_ref.dtype)


def _fox_decode(page_table, cache_kv, cache_lf, layer, fq_s, foxkv_s, lft_s):
    db, n_pages = page_table.shape
    w = N_HEADS * HEAD_DIM
    grid_spec = pltpu.PrefetchScalarGridSpec(
        num_scalar_prefetch=1,
        grid=(db,),
        in_specs=[pl.BlockSpec((None, None, 2, w, PAGE), lambda b, pt, i=i: (layer, pt[b, i], 0, 0, 0))
                  for i in range(n_pages)]
                 + [pl.BlockSpec((None, None, N_HEADS, PAGE), lambda b, pt, i=i: (layer, pt[b, i], 0, 0))
                    for i in range(n_pages)]
                 + [pl.BlockSpec((None, 1, w), lambda b, pt: (b, 0, 0)),
                    pl.BlockSpec((None, 1, 2 * w), lambda b, pt: (b, 0, 0)),
                    pl.BlockSpec(lft_s.shape, lambda b, pt: (0, 0))],
        out_specs=pl.BlockSpec((None, 1, w), lambda b, pt: (b, 0, 0)),
        scratch_shapes=[pltpu.VMEM((N_HEADS, n_pages * PAGE), F32)],
    )
    return pl.pallas_call(
        functools.partial(_fox_decode_body, n_pages=n_pages),
        grid_spec=grid_spec,
        out_shape=jax.ShapeDtypeStruct((db, 1, w), BF16),
        compiler_params=_cparams("arbitrary"),
        name="fox_decode",
    )(page_table, *([cache_kv] * n_pages), *([cache_lf] * n_pages), fq_s, foxkv_s, lft_s)


def _nsa_decode_body(pt_ref, *refs, n_pages):
    pg = refs[:n_pages]
    (win_ref, q_ref, new_ref, gate_ref, wnew_ref, wa_ref, wb_ref, pea_ref, peb_ref, w2_ref, kg0_ref,
     o_ref, wout_ref, x_ref, s_ref) = refs[n_pages:]
    past = n_pages * PAGE
    wbuf = win_ref.shape[2]
    cur = past // SEL_BLOCK
    n_blk = cur + 1
    bidx = pl.program_id(0)

    for i in range(n_pages):
        x_ref[0, i * PAGE:(i + 1) * PAGE, :] = pg[i][0].T
        x_ref[1, i * PAGE:(i + 1) * PAGE, :] = pg[i][1].T
    kc, vc = _compress(x_ref.at[0], x_ref.at[1], wa_ref, wb_ref, pea_ref, peb_ref, w2_ref, kg0_ref)
    nhb = kc.shape[0]

    q = q_ref[...]
    qf = q.astype(F32)
    row = lax.broadcasted_iota(jnp.int32, (N_HEADS, 1), 0)
    slope = _slope_col(row)
    lane = lax.broadcasted_iota(jnp.int32, (1, LANES), 1)
    g0 = row < N_REP

    end = lax.broadcasted_iota(jnp.int32, (1, nhb), 1) * CMP_STRIDE + (CMP_LEN - 1)
    s = _dot_nt(q, kc.astype(BF16)) - slope * (past - end).astype(F32)
    p = _cmp_softmax(s, jnp.broadcast_to(end <= past, s.shape))
    o_cmp = _dot(p.astype(BF16), vc.astype(BF16))
    imp0 = jnp.sum(jnp.where(g0, p, 0.0), axis=0, keepdims=True)
    imp1 = jnp.sum(jnp.where(g0, 0.0, p), axis=0, keepdims=True)
    imp = jnp.where(g0, imp0, imp1)
    sel = _select_blocks(imp, jnp.full((N_HEADS, 1), cur, jnp.int32), n_blk)
    sel_bf = sel.astype(BF16)

    def new_dot(k_row):
        return jnp.sum(qf * k_row.astype(F32), axis=-1, keepdims=True)

    for i in range(n_pages):
        kpos = i * PAGE + lane
        s = _dot(q, pg[i][2].astype(BF16)) - slope * (past - kpos).astype(F32)
        s_ref[:, i * PAGE:(i + 1) * PAGE] = jnp.where(_expand_sel(sel_bf, i * PAGE, PAGE) > 0.5, s, NEG)
    new_ok = jnp.sum(jnp.where(lane == cur, sel.astype(F32), 0.0), axis=-1, keepdims=True) > 0.5
    s_new = jnp.where(new_ok, new_dot(new_ref[:, 0:LANES]), NEG)
    m = jnp.maximum(jnp.max(s_ref[...], axis=-1, keepdims=True), s_new)
    p_new = jnp.where(new_ok, jnp.exp(s_new - m), 0.0)
    l = p_new
    o_sel = p_new.astype(BF16).astype(F32) * new_ref[:, LANES:2 * LANES].astype(F32)
    for i in range(n_pages):
        sp = s_ref[:, i * PAGE:(i + 1) * PAGE]
        p = jnp.where(sp > 0.5 * NEG, jnp.exp(sp - m), 0.0)
        l = l + jnp.sum(p, axis=-1, keepdims=True)
        o_sel = o_sel + _dot_nt(p.astype(BF16), pg[i][3].astype(BF16))
    o_sel = o_sel / jnp.where(l > 0.0, l, 1.0)

    wk = win_ref[0]
    wv = win_ref[1]
    pos = lax.broadcasted_iota(jnp.int32, (1, wbuf), 1)
    d = wbuf - pos
    s = _dot(q, wk.astype(BF16)) - slope * d.astype(F32)
    s = jnp.where(d < WINDOW, s, NEG)
    s_new = new_dot(new_ref[:, 2 * LANES:3 * LANES])
    m = jnp.maximum(jnp.max(s, axis=-1, keepdims=True), s_new)
    p = jnp.exp(s - m)
    p_new = jnp.exp(s_new - m)
    l = jnp.sum(p, axis=-1, keepdims=True) + p_new
    o_win = (_dot_nt(p.astype(BF16), wv.astype(BF16))
             + p_new.astype(BF16).astype(F32) * new_ref[:, 3 * LANES:4 * LANES].astype(F32)) / l

    o = gate_ref[:, 0:1] * o_cmp + gate_ref[:, 1:2] * o_sel + gate_ref[:, 2:3] * o_win
    in_group = _shr(lane, HEAD_DIM) == _shr(row, N_REP)
    o_ref[...] = jnp.where(in_group, o, 0.0).astype(o_ref.dtype)

    wout_ref[0] = jnp.where(pos == wbuf - 1, _column(wnew_ref[0:LANES, :], bidx), pltpu.roll(wk, wbuf - 1, 1))
    wout_ref[1] = jnp.where(pos == wbuf - 1, _column(wnew_ref[LANES:2 * LANES, :], bidx), pltpu.roll(wv, wbuf - 1, 1))


def _nsa_decode(page_table, cache_nsa, win_state, layer, nq_s, nsab_s, gates_s, wint_s, cw):
    db, n_pages = page_table.shape
    wbuf = win_state.shape[4]
    grid_spec = pltpu.PrefetchScalarGridSpec(
        num_scalar_prefetch=1,
        grid=(db,),
        in_specs=[pl.BlockSpec((None, None, 4, LANES, PAGE), lambda b, pt, i=i: (layer, pt[b, i], 0, 0, 0))
                  for i in range(n_pages)]
                 + [pl.BlockSpec((None, None, 2, LANES, wbuf), lambda b, pt: (layer, b, 0, 0, 0)),
                    pl.BlockSpec((None, N_HEADS, LANES), lambda b, pt: (b, 0, 0)),
                    pl.BlockSpec((None, 1, 512), lambda b, pt: (b, 0, 0)),
                    pl.BlockSpec((None, N_HEADS, 3), lambda b, pt: (b, 0, 0)),
                    pl.BlockSpec(wint_s.shape, lambda b, pt: (0, 0))]
                 + [pl.BlockSpec(a.shape, lambda b, pt, nd=a.ndim: (0,) * nd) for a in cw],
        out_specs=[pl.BlockSpec((None, N_HEADS, LANES), lambda b, pt: (b, 0, 0)),
                   pl.BlockSpec((None, 2, LANES, wbuf), lambda b, pt: (b, 0, 0, 0))],
        scratch_shapes=[pltpu.VMEM((2, n_pages * PAGE, LANES), F32), pltpu.VMEM((N_HEADS, n_pages * PAGE), F32)],
    )
    return pl.pallas_call(
        functools.partial(_nsa_decode_body, n_pages=n_pages),
        grid_spec=grid_spec,
        out_shape=[jax.ShapeDtypeStruct((db, N_HEADS, LANES), BF16),
                   jax.ShapeDtypeStruct((db, 2, LANES, wbuf), F32)],
        compiler_params=_cparams("arbitrary"),
        name="nsa_decode",
    )(page_table, *([cache_nsa] * n_pages), win_state, nq_s, nsab_s, gates_s, wint_s, *cw)


O_F, O_GM, O_NQ, O_NKV, O_NG, O_MG = 1536, 1544, 2568, 3080, 3848, 3872
_HEAD_PLACE = (np.arange(N_HEADS)[:, None] // N_REP == np.arange(N_GROUPS)[None, :]).astype(np.float32)


def _prep_w_in(w):
    d = w.shape[0]
    wq = w[:, O_NQ:O_NKV].reshape(d, N_HEADS, 1, HEAD_DIM)
    wq = (wq * _HEAD_PLACE[None, :, :, None]).reshape(d, N_HEADS * LANES)
    small = jnp.concatenate([w[:, O_F:O_GM], w[:, O_NG:O_MG], jnp.zeros((d, LANES - 32), w.dtype)], axis=1)
    w_tok = jnp.concatenate([w[:, 0:O_F], w[:, O_GM:O_NQ], wq, w[:, O_NKV:O_NG], small], axis=1).astype(BF16)
    wt = w.T
    w_feat = jnp.concatenate([wt[512:O_F], wt[O_NKV:O_NG], wt[O_F:O_GM], jnp.zeros((8, d), w.dtype)],
                             axis=0).astype(BF16)
    return w_tok, w_feat, w[:, O_MG:].astype(BF16)


def _pad_heads(g):
    return (_HEAD_PLACE[:, :, None] * g[None, None, :]).reshape(1, N_HEADS * LANES)


def _prep_compress(w1, w2, pe, kg0):
    eye = np.eye(N_GROUPS, dtype=np.float32)
    w1r = w1.reshape(2, 2, CMP_STRIDE, HEAD_DIM, HEAD_DIM)
    wl = jnp.einsum('chldk,cC,gG->hlcgdCGk', w1r, np.eye(2, dtype=np.float32), eye)
    wl = wl.reshape(2, CMP_STRIDE, 256, 256).astype(BF16)
    w2bd = jnp.einsum('cdk,cC,gG->cgdCGk', w2, np.eye(2, dtype=np.float32), eye).reshape(256, 256).astype(BF16)
    pex = jnp.broadcast_to(pe.transpose(1, 0, 2)[:, :, None, :], (CMP_LEN, 2, N_GROUPS, HEAD_DIM))
    pex = pex.reshape(2, CMP_STRIDE, 1, 256)
    return (wl[0], wl[1], pex[0], pex[1], w2bd, jnp.tile(kg0, 2)[None, :])


def _layer(xp, xs, lw, page_table, caches, layer, dims):
    b, s, db = dims

    xp = _ffn(xp, lw['ffn1_norm'], lw['ffn1_w_gu'], lw['ffn1_w_down'])
    xs = _ffn(xs, lw['ffn1_norm'], lw['ffn1_w_gu'], lw['ffn1_w_down'])

    pw = lw['proj']
    (fq, foxt, foxtb, gmu, gmv, nq, nsat, wint, nsatb, cmpraw, lft, gates) = _proj(xp, pw, b, s, False)
    (fq_s, foxt_s, _, gmu_s, gmv_s, nq_s, nsat_s, wint_s, _, _, lft_s, gates_s,
     foxkv_s, nsab_s) = _proj(xs, pw, 1, db, True)

    c = _cumsum(lft).reshape(b, 4, 2, s)
    o_fox_p = _fox_prompt(fq.reshape(b, s, 512), foxtb, c)
    o_gm_p = _gmlp_prompt(gmu, gmv, lw['gmlp_w_s'], lw['gmlp_bexp'])
    cmpkv = _compress_prompt(cmpraw.reshape(b, s, 256), lw['cw'])
    o_nsa_p = _nsa_prompt(nq.reshape(b, s, 1024), cmpkv, nsatb, lw['nsa_aug'], gates.reshape(b, s, LANES))
    xp = _merge(xp, lw['mix_norm'], o_fox_p.reshape(b * s, 512), o_gm_p, o_nsa_p.reshape(b * s, 1024),
                lw['w_gate'], lw['wb0'], lw['wb1'], lw['wb2'], lw['w_out'])

    cache_fox_kv, cache_fox_lf, cache_nsa, win_state = caches
    o_fox_s = _fox_decode(page_table, cache_fox_kv, cache_fox_lf, layer, fq_s.reshape(db, 1, 512),
                          foxkv_s.reshape(db, 1, 1024), lft_s.reshape(N_HEADS, db))
    o_gm_s = _gmlp_sample(gmu_s, gmv_s, lw['gmlp_w00'], lw['gmlp_b0'])
    g3 = gates_s[:, 8:32].reshape(db, 3, N_HEADS).transpose(0, 2, 1)
    o_nsa_s, win_next = _nsa_decode(page_table, cache_nsa, win_state, layer, nq_s.reshape(db, N_HEADS, LANES),
                                    nsab_s.reshape(db, 1, 512), g3, wint_s.reshape(256, db), lw['cw'])
    xs = _merge(xs, lw['mix_norm'], o_fox_s.reshape(db, 512), o_gm_s, o_nsa_s.reshape(db, 1024),
                lw['w_gate'], lw['wb0'], lw['wb1'], lw['wb2'], lw['w_out'])

    xp = _ffn(xp, lw['ffn2_norm'], lw['ffn2_w_gu'], lw['ffn2_w_down'])
    xs = _ffn(xs, lw['ffn2_norm'], lw['ffn2_w_gu'], lw['ffn2_w_down'])

    state = dict(
        fox_kv_p=foxt, fox_kv_s=foxt_s, fox_logf_p=lft, fox_logf_s=lft_s, nsa_kv_p=nsat, nsa_kv_s=nsat_s,
        win_p=wint[:, :, s - min(WINDOW, s):], win_s=win_next,
        gmlp_v_p=gmv.reshape(b, s, 512)[:, s - min(CHUNK, s):], gmlp_v_s=gmv_s.reshape(db, 1, 512),
    )
    return xp, xs, state


def kernel(x_prompt, x_sample, page_table, cache_fox_kv, cache_fox_logf, cache_nsa_kv, state_nsa_win, ffn1_norm, ffn1_w_gu, ffn1_w_down, mix_norm, w_in, fox_b_f, fox_qk_gain, gmlp_v_gain, gmlp_w_s, gmlp_b_s, nsa_q_gain, nsa_k_gain, nsa_cmp_pe, nsa_cmp_w1, nsa_cmp_w2, w_branch, w_out, ffn2_norm, ffn2_w_gu, ffn2_w_down):
    b, s, d = x_prompt.shape
    db = x_sample.shape[0]
    depth = w_in.shape[0]
    n_pool = cache_fox_kv.shape[1]
    wb = state_nsa_win.shape[2]
    xp = x_prompt.reshape(b * s, d)
    xs = x_sample.reshape(db, d)

    caches = (cache_fox_kv.transpose(0, 1, 3, 4, 5, 2).reshape(depth, n_pool, 2, N_HEADS * HEAD_DIM, PAGE),
              cache_fox_logf.transpose(0, 1, 3, 2),
              cache_nsa_kv.transpose(0, 1, 3, 4, 5, 2).reshape(depth, n_pool, 4, LANES, PAGE),
              state_nsa_win.transpose(0, 1, 3, 4, 5, 2).reshape(depth, db, 2, LANES, wb))

    nsa_rows = (np.arange(N_HEADS * LANES) % LANES) // HEAD_DIM == (np.arange(N_HEADS * LANES) // LANES) // N_REP
    col = lambda g: g[:, None]
    nsa_aug = _nsa_aug(s)
    states = []
    for l in range(depth):
        w_tok, w_feat, w_gate = _prep_w_in(w_in[l])
        wb2 = jnp.zeros((N_HEADS * LANES, d), F32).at[np.nonzero(nsa_rows)[0]].set(w_branch[l, 2])
        proj = (mix_norm[l][None, :], w_tok, w_feat,
                jnp.tile(fox_qk_gain[l, 0], 2)[None, :], jnp.tile(fox_qk_gain[l, 1], 2)[None, :],
                col(fox_qk_gain[l, 1]), gmlp_v_gain[l][None, :], _pad_heads(nsa_q_gain[l]),
                jnp.tile(nsa_k_gain[l, 1], 2)[None, :], jnp.tile(nsa_k_gain[l, 2], 2)[None, :],
                col(nsa_k_gain[l, 1]), col(nsa_k_gain[l, 2]), col(fox_b_f[l]))
        lw = dict(
            ffn1_norm=ffn1_norm[l][None, :], ffn1_w_gu=ffn1_w_gu[l].astype(BF16), ffn1_w_down=ffn1_w_down[l].astype(BF16),
            ffn2_norm=ffn2_norm[l][None, :], ffn2_w_gu=ffn2_w_gu[l].astype(BF16), ffn2_w_down=ffn2_w_down[l].astype(BF16),
            mix_norm=mix_norm[l][None, :], proj=proj, w_gate=w_gate,
            gmlp_w_s=gmlp_w_s[l], gmlp_bexp=jnp.repeat(gmlp_b_s[l].T, HEAD_DIM, axis=1),
            gmlp_w00=jnp.repeat(gmlp_w_s[l, :, 0, 0], HEAD_DIM)[None, :],
            gmlp_b0=jnp.repeat(gmlp_b_s[l, :, 0], HEAD_DIM)[None, :],
            cw=_prep_compress(nsa_cmp_w1[l], nsa_cmp_w2[l], nsa_cmp_pe[l], nsa_k_gain[l, 0]),
            wb0=w_branch[l, 0].astype(BF16), wb1=w_branch[l, 1].astype(BF16), wb2=wb2.astype(BF16),
            w_out=w_out[l].astype(BF16), nsa_aug=nsa_aug,
        )
        xp, xs, st = _layer(xp, xs, lw, page_table, caches, l, (b, s, db))
        states.append(st)

    def stack(name):
        return jnp.stack([st[name] for st in states])

    def tok_major(a, *feat_dims):
        return jnp.moveaxis(a.reshape(a.shape[:2] + feat_dims + a.shape[3:]), -1, 2)

    def tok_major_s(a, *feat_dims):
        return jnp.swapaxes(tok_major(a, *feat_dims), 1, 2)

    return (xp.reshape(b, s, d), xs.reshape(db, 1, d),
            tok_major(stack('fox_kv_p'), 2, N_HEADS, HEAD_DIM), tok_major_s(stack('fox_kv_s'), 2, N_HEADS, HEAD_DIM),
            tok_major(stack('fox_logf_p'), N_HEADS), tok_major_s(stack('fox_logf_s'), N_HEADS),
            tok_major(stack('nsa_kv_p'), 4, N_GROUPS, HEAD_DIM), tok_major_s(stack('nsa_kv_s'), 4, N_GROUPS, HEAD_DIM),
            tok_major(stack('win_p'), 2, N_GROUPS, HEAD_DIM),
            tok_major(stack('win_s').reshape(depth, db, 2 * LANES, wb), 2, N_GROUPS, HEAD_DIM),
            stack('gmlp_v_p'), stack('gmlp_v_s'))
```

```python
import functools

import jax
import jax.numpy as jnp
import numpy as np
from jax import lax
from jax.experimental import pallas as pl
from jax.experimental.pallas import tpu as pltpu

F32 = jnp.float32
BF16 = jnp.bfloat16

HEAD_DIM = 64
LANES = 128
PAGE = 128
CHUNK = 128
GMLP_CHUNKS_PER_STEP = 4
DECODE_ROWS = 2
Q_BLOCK = 128
FOX_TQ = 512
ATT_TK = 512
CMP_LEN = 32
CMP_STRIDE = 16
SEL_BLOCK = 64
N_SEL = 8
WINDOW = 512
N_GROUPS = 2
N_REP = 4
N_HEADS = 8
FORCE_BONUS = 1.0e4
NEG = -1.0e30
EPS = 1e-6
SCALE = HEAD_DIM ** -0.5
VMEM_LIMIT = 56 * 1024 * 1024
HI = lax.Precision.HIGHEST


def _cparams(*sem):
    return pltpu.CompilerParams(dimension_semantics=sem, vmem_limit_bytes=VMEM_LIMIT)


def _dot(a, b, precision=None):
    return jnp.dot(a, b, preferred_element_type=F32, precision=precision)


def _dot_nt(a, b):
    return lax.dot_general(a, b, (((1,), (1,)), ((), ())), preferred_element_type=F32)


def _shr(x, pow2):
    return jnp.right_shift(x, int(pow2).bit_length() - 1)


def _rms_rows(x, g):
    ms = jnp.mean(x * x, axis=-1, keepdims=True)
    return x * lax.rsqrt(ms + EPS) * g


def _headnorm_pair(zb, gain):
    lo = lax.broadcasted_iota(jnp.int32, zb.shape, 1) < HEAD_DIM
    sq = zb * zb
    s_lo = jnp.sum(jnp.where(lo, sq, 0.0), axis=-1, keepdims=True)
    s_hi = jnp.sum(jnp.where(lo, 0.0, sq), axis=-1, keepdims=True)
    ms = jnp.where(lo, s_lo, s_hi) * (1.0 / HEAD_DIM)
    return zb * lax.rsqrt(ms + EPS) * gain


def _headnorm_cols(zt, gain_col):
    ms = jnp.mean(zt * zt, axis=0, keepdims=True)
    return zt * lax.rsqrt(ms + EPS) * gain_col


def _silu(x):
    return x * jax.nn.sigmoid(x)


def _gelu_tanh(x):
    return 0.5 * x * (1.0 + jnp.tanh(np.sqrt(2.0 / np.pi) * (x + 0.044715 * (x * x * x))))


def _log_sigmoid(x):
    return jnp.minimum(x, 0.0) - jnp.log1p(jnp.exp(-jnp.abs(x)))


def _token_tile(t):
    for tm in (512, 384, 256, 128):
        if t % tm == 0:
            return tm
    raise ValueError(f"token count {t} is not a multiple of 128")


def _const_spec(shape):
    nd = len(shape)
    return pl.BlockSpec(shape, lambda *_: (0,) * nd, pipeline_mode=pl.Buffered(1))


def _ffn_body(x_ref, g_ref, wgu_ref, wd_ref, o_ref, *, d_ff, fc):
    x = x_ref[...]
    h = _rms_rows(x, g_ref[...]).astype(BF16)
    acc = jnp.zeros_like(x)
    for c in range(d_ff // fc):
        g = _dot(h, wgu_ref[:, c * fc:(c + 1) * fc])
        u = _dot(h, wgu_ref[:, d_ff + c * fc:d_ff + (c + 1) * fc])
        a = (_silu(g) * u).astype(BF16)
        acc = acc + _dot(a, wd_ref[c * fc:(c + 1) * fc, :])
    o_ref[...] = x + 0.5 * acc


def _ffn(x, g, wgu, wd):
    t, d = x.shape
    d_ff = wd.shape[0]
    tm = _token_tile(t)
    fc = 256 if d_ff % 256 == 0 else 128
    return pl.pallas_call(
        functools.partial(_ffn_body, d_ff=d_ff, fc=fc),
        grid=(t // tm,),
        in_specs=[pl.BlockSpec((tm, d), lambda i: (i, 0)),
                  _const_spec((1, d)),
                  _const_spec((d, 2 * d_ff)),
                  _const_spec((d_ff, d))],
        out_specs=pl.BlockSpec((tm, d), lambda i: (i, 0)),
        out_shape=jax.ShapeDtypeStruct((t, d), F32),
        compiler_params=_cparams("parallel"),
        name="ffn",
    )(x, g, wgu, wd)


C_FOX = 0
C_GM = 1536
C_NQ = 2560
C_NKV = 3584
C_SMALL = 4352
C_END = 4480
R_NKV = 1024
R_F = 1792
R_END = 1808


def _proj_body(x_ref, g_ref, w_ref, wt_ref, gq_ref, gk_ref, gkc_ref, gv_ref, gnq_ref, gk1_ref, gk2_ref,
               gk1c_ref, gk2c_ref, bfc_ref, *out_refs, sample):
    (fq_ref, foxt_ref, foxtb_ref, gmu_ref, gmv_ref, nq_ref, nsat_ref, wint_ref, nsatb_ref, cmpraw_ref,
     lft_ref, gate_ref) = out_refs[:12]
    h = _rms_rows(x_ref[...], g_ref[...]).astype(BF16)

    z = _dot(h, w_ref[:, 0:512])
    for j in range(4):
        sl = slice(j * LANES, (j + 1) * LANES)
        fq_ref[:, sl] = (_headnorm_pair(z[:, sl], gq_ref[...]) * SCALE).astype(BF16)
    zt = _dot_nt(wt_ref[0:R_NKV, :], h)
    for hh in range(N_HEADS):
        rs = slice(hh * HEAD_DIM, (hh + 1) * HEAD_DIM)
        kt = _headnorm_cols(zt[rs], gkc_ref[...])
        foxt_ref[rs, :] = kt
        foxtb_ref[rs, :] = kt.astype(BF16)
    foxt_ref[512:1024, :] = zt[512:1024]
    foxtb_ref[512:1024, :] = zt[512:1024].astype(BF16)

    a = _gelu_tanh(_dot(h, w_ref[:, C_GM:C_NQ]))
    gmu_ref[...] = a[:, 0:512]
    gmv_ref[...] = _rms_rows(a[:, 512:1024], gv_ref[...])

    z = _dot(h, w_ref[:, C_NQ:C_NKV])
    for hh in range(N_HEADS):
        sl = slice(hh * LANES, (hh + 1) * LANES)
        zb = z[:, sl]
        ms = jnp.sum(zb * zb, axis=-1, keepdims=True) * (1.0 / HEAD_DIM)
        nq_ref[:, sl] = (zb * lax.rsqrt(ms + EPS) * (gnq_ref[:, sl] * SCALE)).astype(BF16)

    zt = _dot_nt(wt_ref[R_NKV:R_F, :], h)
    nsat_ref[0:256, :] = zt[0:256]
    nsat_ref[384:512, :] = zt[384:512]
    nsatb_ref[128:256, :] = zt[384:512].astype(BF16)
    wint_ref[128:256, :] = zt[640:768]
    nsatb_ref[384:512, :] = zt[640:768].astype(BF16)
    for g in range(N_GROUPS):
        sk = _headnorm_cols(zt[256 + g * HEAD_DIM:256 + (g + 1) * HEAD_DIM], gk1c_ref[...])
        nsat_ref[256 + g * HEAD_DIM:256 + (g + 1) * HEAD_DIM, :] = sk
        nsatb_ref[g * HEAD_DIM:(g + 1) * HEAD_DIM, :] = sk.astype(BF16)
        wk = _headnorm_cols(zt[512 + g * HEAD_DIM:512 + (g + 1) * HEAD_DIM], gk2c_ref[...])
        wint_ref[g * HEAD_DIM:(g + 1) * HEAD_DIM, :] = wk
        nsatb_ref[256 + g * HEAD_DIM:256 + (g + 1) * HEAD_DIM, :] = wk.astype(BF16)

    cmpraw_ref[...] = _dot(h, w_ref[:, C_NKV:C_NKV + 256])

    zt = _dot_nt(wt_ref[R_F:R_END, :], h)
    lft_ref[...] = _log_sigmoid(zt[0:N_HEADS] + bfc_ref[...])
    gate_ref[...] = jax.nn.sigmoid(_dot(h, w_ref[:, C_SMALL:C_END]))

    if sample:
        foxs_ref, nsabs_ref = out_refs[12:]
        z = _dot(h, w_ref[:, 512:C_GM])
        for j in range(4):
            sl = slice(j * LANES, (j + 1) * LANES)
            foxs_ref[:, sl] = _headnorm_pair(z[:, sl], gk_ref[...])
        foxs_ref[:, 512:1024] = z[:, 512:1024]
        z = _dot(h, w_ref[:, C_NKV + 256:C_SMALL])
        nsabs_ref[:, 0:128] = _headnorm_pair(z[:, 0:128], gk1_ref[...]).astype(BF16)
        nsabs_ref[:, 128:256] = z[:, 128:256].astype(BF16)
        nsabs_ref[:, 256:384] = _headnorm_pair(z[:, 256:384], gk2_ref[...]).astype(BF16)
        nsabs_ref[:, 384:512] = z[:, 384:512].astype(BF16)


def _proj(x, pw, b, s, sample):
    t, d = x.shape
    tm = _token_tile(s)
    nst = s // tm
    tok = lambda n, dt: (pl.BlockSpec((tm, n), lambda i: (i, 0)), jax.ShapeDtypeStruct((t, n), dt))
    feat = lambda n, dt: (pl.BlockSpec((None, n, tm), lambda i: (i // nst, 0, i % nst)),
                          jax.ShapeDtypeStruct((b, n, s), dt))
    outs = [tok(512, BF16), feat(1024, F32), feat(1024, BF16), tok(512, F32), tok(512, F32), tok(1024, BF16),
            feat(512, F32), feat(256, F32), feat(512, BF16), tok(256, F32), feat(N_HEADS, F32), tok(LANES, F32)]
    if sample:
        outs += [tok(1024, F32), tok(512, BF16)]
    return pl.pallas_call(
        functools.partial(_proj_body, sample=sample),
        grid=(t // tm,),
        in_specs=[pl.BlockSpec((tm, d), lambda i: (i, 0))] + [_const_spec(a.shape) for a in pw],
        out_specs=[o[0] for o in outs],
        out_shape=[o[1] for o in outs],
        compiler_params=_cparams("parallel"),
        name="in_proj_sample" if sample else "in_proj",
    )(x, *pw)


def _cumsum_body(x_ref, o_ref, *, blk):
    s = x_ref.shape[1]
    r = lax.broadcasted_iota(jnp.int32, (blk, blk), 0)
    c = lax.broadcasted_iota(jnp.int32, (blk, blk), 1)
    tri = (r <= c).astype(F32)
    carry = jnp.zeros((x_ref.shape[0], 1), F32)
    for i in range(s // blk):
        cs = _dot(x_ref[:, i * blk:(i + 1) * blk], tri, precision=HI) + carry
        o_ref[:, i * blk:(i + 1) * blk] = cs
        carry = cs[:, blk - 1:blk]


def _cumsum(x):
    b, n, s = x.shape
    return pl.pallas_call(
        functools.partial(_cumsum_body, blk=min(256, s)),
        grid=(b,),
        in_specs=[pl.BlockSpec((None, n, s), lambda i: (i, 0, 0))],
        out_specs=pl.BlockSpec((None, n, s), lambda i: (i, 0, 0)),
        out_shape=jax.ShapeDtypeStruct((b, n, s), F32),
        compiler_params=_cparams("parallel"),
        name="logf_cumsum",
    )(x)


def _fox_body(q_ref, k_ref, v_ref, c_ref, o_ref, *, tq, tk):
    qi = pl.program_id(2)
    q = q_ref[...]
    lo = lax.broadcasted_iota(jnp.int32, (tq, LANES), 1) < HEAD_DIM
    zero = jnp.zeros_like(q)
    qs = jnp.concatenate([jnp.where(lo, q, zero), jnp.where(lo, zero, q)], axis=0)
    qpos = qi * tq + lax.broadcasted_iota(jnp.int32, (tq, 1), 0)

    def block(kb, carry, masked):
        t0 = pl.multiple_of(kb * tk, tk)
        m_old, l_old, acc = carry
        bias = -c_ref[:, pl.ds(t0, tk)]
        s = _dot(qs, k_ref[:, pl.ds(t0, tk)]).reshape(2, tq, tk) + bias[:, None, :]
        if masked:
            kpos = t0 + lax.broadcasted_iota(jnp.int32, (1, tk), 1)
            s = jnp.where((kpos <= qpos)[None], s, NEG)
        s = s.reshape(2 * tq, tk)
        m_new = jnp.maximum(m_old, jnp.max(s, axis=-1, keepdims=True))
        alpha = jnp.exp(m_old - m_new)
        p = jnp.exp(s - m_new)
        l_new = alpha * l_old + jnp.sum(p, axis=-1, keepdims=True)
        return m_new, l_new, alpha * acc + _dot_nt(p.astype(BF16), v_ref[:, pl.ds(t0, tk)])

    n_kb = _shr(qi * tq + tq + tk - 1, tk)
    carry = (jnp.full((2 * tq, 1), NEG, F32), jnp.zeros((2 * tq, 1), F32), jnp.zeros((2 * tq, LANES), F32))
    carry = lax.fori_loop(0, n_kb - 1, lambda kb, c: block(kb, c, False), carry)
    _, l, acc = block(n_kb - 1, carry, True)
    o = acc / l
    o_ref[...] = jnp.where(lo, o[0:tq], o[tq:2 * tq]).astype(o_ref.dtype)


def _fox_prompt(fq, foxtb, c):
    b, s, _ = fq.shape
    tq = min(FOX_TQ, s)
    tk = min(ATT_TK, s)
    return pl.pallas_call(
        functools.partial(_fox_body, tq=tq, tk=tk),
        grid=(b, 4, s // tq),
        in_specs=[pl.BlockSpec((None, tq, LANES), lambda bi, p, qi: (bi, qi, p)),
                  pl.BlockSpec((None, LANES, s), lambda bi, p, qi: (bi, p, 0)),
                  pl.BlockSpec((None, LANES, s), lambda bi, p, qi: (bi, 4 + p, 0)),
                  pl.BlockSpec((None, None, 2, s), lambda bi, p, qi: (bi, p, 0, 0))],
        out_specs=pl.BlockSpec((None, tq, LANES), lambda bi, p, qi: (bi, qi, p)),
        out_shape=jax.ShapeDtypeStruct((b, s, 512), BF16),
        compiler_params=_cparams("parallel", "parallel", "arbitrary"),
        name="fox_prompt",
    )(fq, foxtb, foxtb, c)


def _gmlp_body(u_ref, v_ref, ws_ref, b_ref, o_ref):
    n = ws_ref.shape[1]
    r = lax.broadcasted_iota(jnp.int32, (n, n), 0)
    c = lax.broadcasted_iota(jnp.int32, (n, n), 1)
    lo = lax.broadcasted_iota(jnp.int32, (n, LANES), 1) < HEAD_DIM
    for j in range(4):
        sl = slice(j * LANES, (j + 1) * LANES)
        w0 = jnp.where(r >= c, ws_ref[2 * j], 0.0).astype(BF16)
        w1 = jnp.where(r >= c, ws_ref[2 * j + 1], 0.0).astype(BF16)
        for ch in range(u_ref.shape[0] // n):
            rs = slice(ch * n, (ch + 1) * n)
            vp = v_ref[rs, sl].astype(BF16)
            mixed = jnp.where(lo, _dot(w0, vp), _dot(w1, vp)) + b_ref[:, sl]
            o_ref[rs, sl] = (u_ref[rs, sl] * mixed).astype(o_ref.dtype)


def _gmlp_prompt(gmu, gmv, ws, bexp):
    t = gmu.shape[0]
    rows = GMLP_CHUNKS_PER_STEP * CHUNK if t % (GMLP_CHUNKS_PER_STEP * CHUNK) == 0 else CHUNK
    return pl.pallas_call(
        _gmlp_body,
        grid=(t // rows,),
        in_specs=[pl.BlockSpec((rows, 512), lambda i: (i, 0)),
                  pl.BlockSpec((rows, 512), lambda i: (i, 0)),
                  _const_spec(ws.shape), _const_spec(bexp.shape)],
        out_specs=pl.BlockSpec((rows, 512), lambda i: (i, 0)),
        out_shape=jax.ShapeDtypeStruct((t, 512), BF16),
        compiler_params=_cparams("parallel"),
        name="gmlp_prompt",
    )(gmu, gmv, ws, bexp)


def _gmlp_first_body(u_ref, v_ref, w_ref, b_ref, o_ref):
    o_ref[...] = (u_ref[...] * (v_ref[...] * w_ref[...] + b_ref[...])).astype(o_ref.dtype)


def _gmlp_sample(gmu, gmv, wrow, brow):
    n = gmu.shape[0]
    full = pl.BlockSpec((n, 512), lambda i: (0, 0))
    return pl.pallas_call(
        _gmlp_first_body,
        grid=(1,),
        in_specs=[full, full, _const_spec((1, 512)), _const_spec((1, 512))],
        out_specs=full,
        out_shape=jax.ShapeDtypeStruct((n, 512), BF16),
        compiler_params=_cparams("arbitrary"),
        name="gmlp_sample",
    )(gmu, gmv, wrow, brow)


def _compress_finish(xa, xb, wa_ref, wb_ref, w2_ref, kg0_ref):
    nhb = xa.shape[0]
    a = _dot(xa, wa_ref[...])
    b = _dot(xb, wb_ref[...])
    pre = a + pltpu.roll(b, nhb - 1, 0)
    out = _dot(_silu(pre).astype(BF16), w2_ref[...])
    kc = _headnorm_pair(out[:, 0:LANES], kg0_ref[...])
    return kc, out[:, LANES:2 * LANES]


def _compress(xk_ref, xv_ref, wa_ref, wb_ref, pea_ref, peb_ref, w2_ref, kg0_ref):
    nhb = xk_ref.shape[0] // CMP_STRIDE
    xa, xb = [], []
    for l in range(CMP_STRIDE):
        xl = jnp.concatenate([xk_ref[pl.ds(l, nhb, stride=CMP_STRIDE), :],
                              xv_ref[pl.ds(l, nhb, stride=CMP_STRIDE), :]], axis=1)
        xa.append((xl + pea_ref[l]).astype(BF16))
        xb.append((xl + peb_ref[l]).astype(BF16))
    return _compress_finish(jnp.concatenate(xa, axis=1), jnp.concatenate(xb, axis=1), wa_ref, wb_ref, w2_ref, kg0_ref)


def _compress_body(xk_ref, xv_ref, wa_ref, wb_ref, pea_ref, peb_ref, w2_ref, kg0_ref, pet_ref, o_ref):
    kc, vc = _compress(xk_ref, xv_ref, wa_ref, wb_ref, pea_ref, peb_ref, w2_ref, kg0_ref)
    o_ref[:, 0:LANES] = kc.astype(BF16)
    o_ref[:, LANES:2 * LANES] = vc.astype(BF16)


def _compress_prompt(cmpraw, cw):
    b, s, _ = cmpraw.shape
    nhb = s // CMP_STRIDE
    return pl.pallas_call(
        _compress_body,
        grid=(b,),
        in_specs=[pl.BlockSpec((None, s, LANES), lambda i: (i, 0, 0)),
                  pl.BlockSpec((None, s, LANES), lambda i: (i, 0, 1))] + [_const_spec(a.shape) for a in cw],
        out_specs=pl.BlockSpec((None, nhb, 256), lambda i: (i, 0, 0)),
        out_shape=jax.ShapeDtypeStruct((b, nhb, 256), BF16),
        compiler_params=_cparams("parallel"),
        name="nsa_compress",
    )(cmpraw, cmpraw, *cw)


def _slope_col(head):
    out = jnp.zeros(head.shape, F32)
    for hh in range(N_HEADS):
        out = jnp.where(head == hh, 2.0 ** (-(hh + 1)), out)
    return out


def _slope_rows(g):
    return jnp.concatenate([jnp.full((Q_BLOCK, 1), 2.0 ** (-(N_REP * g + r + 1)), F32) for r in range(N_REP)], axis=0)


def _cmp_softmax(s, mask):
    s = jnp.where(mask, s, NEG)
    m = jnp.max(s, axis=-1, keepdims=True)
    e = jnp.where(mask, jnp.exp(s - m), 0.0)
    l = jnp.sum(e, axis=-1, keepdims=True)
    return e / jnp.where(l > 0.0, l, 1.0)


def _select_blocks(imp, cur, n_blk):
    m, n = imp.shape
    pool = (_shr(lax.broadcasted_iota(jnp.int32, (n, LANES), 0), SEL_BLOCK // CMP_STRIDE)
            == lax.broadcasted_iota(jnp.int32, (n, LANES), 1)).astype(F32)
    score = _dot(imp, pool, precision=HI)
    j = lax.broadcasted_iota(jnp.int32, (m, LANES), 1)
    forced = (j == 0) | (j == cur) | (j == cur - 1)
    score = jnp.where(forced, score + FORCE_BONUS, score)
    score = jnp.where(j > cur, NEG, score)
    rank = jnp.zeros((m, LANES), F32)
    for i in range(n_blk):
        ci = score[:, i:i + 1]
        ahead = (ci > score) | ((ci == score) & (j > i))
        rank = rank + ahead.astype(F32)
    return (rank < float(min(N_SEL, n_blk))) & (score > 0.5 * NEG)


def _expand_sel(sel_bf, t0, n):
    jj = lax.broadcasted_iota(jnp.int32, (LANES, n), 0)
    tt = t0 + lax.broadcasted_iota(jnp.int32, (LANES, n), 1)
    e = (_shr(tt, SEL_BLOCK) == jj).astype(BF16)
    return _dot(sel_bf, e)


AUG_SEL = 16
MAX_SEL_BLOCKS = 32


def _aug_rows(pos):
    n = pos.shape[1]
    r = lax.broadcasted_iota(jnp.int32, (HEAD_DIM, n), 0)
    onehot = (r >= AUG_SEL) & (_shr(pos, SEL_BLOCK) == r - AUG_SEL)
    return jnp.where(r == 0, _shr(pos, LANES).astype(F32),
                     jnp.where(r == 1, (pos & (LANES - 1)).astype(F32), onehot.astype(F32)))


def _select_bias_t(imp, qi, off, n_blk):
    n = imp.shape[1]
    base = off + AUG_SEL
    prow = lax.broadcasted_iota(jnp.int32, (LANES, n), 0) - base
    pool_t = (prow == _shr(lax.broadcasted_iota(jnp.int32, (LANES, n), 1), SEL_BLOCK // CMP_STRIDE)).astype(F32)
    score = lax.dot_general(pool_t, imp, (((1,), (1,)), ((), ())), preferred_element_type=F32,
                            precision=HI)[base:base + MAX_SEL_BLOCKS]
    j = lax.broadcasted_iota(jnp.int32, (MAX_SEL_BLOCKS, 1), 0)
    cur = _shr(qi * Q_BLOCK + lax.broadcasted_iota(jnp.int32, (1, Q_BLOCK), 1), SEL_BLOCK)
    forced = (j == 0) | (j == cur) | (j == cur - 1)
    score = jnp.where(forced, score + FORCE_BONUS, score)
    score = jnp.where(j > cur, NEG, score)
    rank = jnp.zeros(score.shape, F32)
    for i in range(n_blk):
        ci = score[i:i + 1, :]
        ahead = (ci > score) | ((ci == score) & (j > i))
        rank = rank + ahead.astype(F32)
    chosen = (rank < float(min(N_SEL, n_blk))) & (score > 0.5 * NEG)
    bias_t = jnp.where(chosen, 0.0, NEG)
    parts = [bias_t, jnp.zeros((LANES - base - MAX_SEL_BLOCKS, Q_BLOCK), F32)]
    if base:
        parts = [jnp.zeros((base, Q_BLOCK), F32)] + parts
    return jnp.concatenate(parts, axis=0).T


def _nsa_body(nq_ref, cmp_ref, kv_ref, aug_ref, gate_ref, o_ref, *, n_blk, tk, wlen):
    qi = pl.program_id(1)
    s_len = kv_ref.shape[1]
    rows = N_REP * Q_BLOCK
    row = lax.broadcasted_iota(jnp.int32, (rows, 1), 0)
    qpos = qi * Q_BLOCK + (row & (Q_BLOCK - 1))
    qpos_q = qi * Q_BLOCK + lax.broadcasted_iota(jnp.int32, (Q_BLOCK, 1), 0)
    lane = lax.broadcasted_iota(jnp.int32, (1, LANES), 1)
    nhb = cmp_ref.shape[0]
    end_col = lax.broadcasted_iota(jnp.int32, (nhb, 1), 0) * CMP_STRIDE + (CMP_LEN - 1)
    end = lax.broadcasted_iota(jnp.int32, (1, nhb), 1) * CMP_STRIDE + (CMP_LEN - 1)

    n_kb = _shr(qi * Q_BLOCK + Q_BLOCK + tk - 1, tk)
    t_last = pl.multiple_of((n_kb - 1) * tk, tk)
    causal = jnp.where(t_last + lax.broadcasted_iota(jnp.int32, (1, tk), 1) <= qpos_q, 0.0, NEG)
    t_win = pl.multiple_of(jnp.clip((qi - WINDOW // Q_BLOCK) * Q_BLOCK, 0, s_len - wlen), Q_BLOCK)
    d_win = qpos_q - (t_win + lax.broadcasted_iota(jnp.int32, (1, wlen), 1))
    band = jnp.where((d_win >= 0) & (d_win < WINDOW), 0.0, NEG)

    def masked(s, bias):
        n = s.shape[1]
        return (s.reshape(N_REP, Q_BLOCK, n) + bias[None]).reshape(rows, n)

    for g in range(N_GROUPS):
        off = HEAD_DIM * (1 - g)
        q = jnp.concatenate([nq_ref[:, (N_REP * g + r) * LANES:(N_REP * g + r + 1) * LANES]
                             for r in range(N_REP)], axis=0).astype(F32)
        slope = _slope_rows(g)
        q_alibi = q + jnp.where(lane == off, slope * float(LANES), 0.0) + jnp.where(lane == off + 1, slope, 0.0)
        q_win = q_alibi.astype(BF16)

        def keys(lo, t0, n):
            k = kv_ref[lo + g * HEAD_DIM:lo + (g + 1) * HEAD_DIM, pl.ds(t0, n)]
            a = aug_ref[:, pl.ds(t0, n)]
            return jnp.concatenate([k, a] if g == 0 else [a, k], axis=0)

        in_g = _shr(lane, HEAD_DIM) == g
        kc = cmp_ref[:, 0:LANES].astype(F32)
        kc = jnp.where(in_g, kc, jnp.where(lane == off, _shr(end_col, LANES).astype(F32),
                                           jnp.where(lane == off + 1, (end_col & (LANES - 1)).astype(F32), 0.0)))
        p = _cmp_softmax(_dot_nt(q_win, kc.astype(BF16)), end <= qpos)
        o_cmp = _dot(p.astype(BF16), cmp_ref[:, LANES:2 * LANES])
        imp = p[0:Q_BLOCK]
        for r in range(1, N_REP):
            imp = imp + p[r * Q_BLOCK:(r + 1) * Q_BLOCK]
        sel_bias = _select_bias_t(imp, qi, off, n_blk)
        q_sel = (q_alibi + jnp.concatenate([sel_bias] * N_REP, axis=0)).astype(BF16)

        def sel_tile(t0, carry, last):
            m_old, l_old, acc = carry
            s = _dot(q_sel, keys(0, t0, tk))
            if last:
                s = masked(s, causal)
            v = kv_ref[LANES:2 * LANES, pl.ds(t0, tk)]
            m_new = jnp.maximum(m_old, jnp.max(s, axis=-1, keepdims=True))
            alpha = jnp.exp(m_old - m_new)
            p = jnp.exp(s - m_new)
            l_new = alpha * l_old + jnp.sum(p, axis=-1, keepdims=True)
            return m_new, l_new, alpha * acc + _dot_nt(p.astype(BF16), v)

        carry = (jnp.full((rows, 1), NEG, F32), jnp.zeros((rows, 1), F32), jnp.zeros((rows, LANES), F32))
        carry = lax.fori_loop(0, n_kb - 1, lambda kb, c: sel_tile(pl.multiple_of(kb * tk, tk), c, False), carry)
        _, l_sel, acc_sel = sel_tile(t_last, carry, True)
        o_sel = acc_sel / l_sel

        s = masked(_dot(q_win, keys(2 * LANES, t_win, wlen)), band)
        p = jnp.exp(s - jnp.max(s, axis=-1, keepdims=True))
        v = kv_ref[3 * LANES:4 * LANES, pl.ds(t_win, wlen)]
        o_win = _dot_nt(p.astype(BF16), v) / jnp.sum(p, axis=-1, keepdims=True)

        for r in range(N_REP):
            hh = N_REP * g + r
            rs = slice(r * Q_BLOCK, (r + 1) * Q_BLOCK)
            o = (gate_ref[:, 8 + hh:9 + hh] * o_cmp[rs] + gate_ref[:, 16 + hh:17 + hh] * o_sel[rs]
                 + gate_ref[:, 24 + hh:25 + hh] * o_win[rs])
            o_ref[:, hh * LANES:(hh + 1) * LANES] = jnp.where(in_g, o, 0.0).astype(o_ref.dtype)


def _nsa_aug_body(o_ref):
    o_ref[...] = _aug_rows(lax.broadcasted_iota(jnp.int32, (1, o_ref.shape[1]), 1)).astype(o_ref.dtype)


def _nsa_aug(s):
    return pl.pallas_call(
        _nsa_aug_body,
        out_shape=jax.ShapeDtypeStruct((HEAD_DIM, s), BF16),
        name="nsa_aug",
    )()


def _nsa_prompt(nq, cmpkv, nsatb, aug, gates):
    b, s, _ = nq.shape
    nhb = cmpkv.shape[1]
    n_blk = s // SEL_BLOCK
    assert n_blk <= MAX_SEL_BLOCKS and s // LANES <= 256
    return pl.pallas_call(
        functools.partial(_nsa_body, n_blk=n_blk, tk=min(ATT_TK, s), wlen=min(WINDOW + Q_BLOCK, s)),
        grid=(b, s // Q_BLOCK),
        in_specs=[pl.BlockSpec((None, Q_BLOCK, 1024), lambda bi, qi: (bi, qi, 0)),
                  pl.BlockSpec((None, nhb, 256), lambda bi, qi: (bi, 0, 0)),
                  pl.BlockSpec((None, 512, s), lambda bi, qi: (bi, 0, 0)),
                  pl.BlockSpec((HEAD_DIM, s), lambda bi, qi: (0, 0)),
                  pl.BlockSpec((None, Q_BLOCK, LANES), lambda bi, qi: (bi, qi, 0))],
        out_specs=pl.BlockSpec((None, Q_BLOCK, 1024), lambda bi, qi: (bi, qi, 0)),
        out_shape=jax.ShapeDtypeStruct((b, s, 1024), BF16),
        compiler_params=_cparams("parallel", "arbitrary"),
        name="nsa_prompt",
    )(nq, cmpkv, nsatb, aug, gates)


def _merge_body(x_ref, g_ref, of_ref, og_ref, on_ref, wg_ref, wb0_ref, wb1_ref, wb2_ref, wo_ref, o_ref):
    x = x_ref[...]
    d = x.shape[1]
    h = _rms_rows(x, g_ref[...]).astype(BF16)
    y = jax.nn.sigmoid(_dot(h, wg_ref[:, 0:d])) * _dot(of_ref[...], wb0_ref[...])
    y = y + jax.nn.sigmoid(_dot(h, wg_ref[:, d:2 * d])) * _dot(og_ref[...], wb1_ref[...])
    y = y + jax.nn.sigmoid(_dot(h, wg_ref[:, 2 * d:3 * d])) * _dot(on_ref[...], wb2_ref[...])
    o_ref[...] = x + _dot(y.astype(BF16), wo_ref[...])


def _merge(x, g, o_fox, o_gm, o_nsa, wg, wb0, wb1, wb2, wo):
    t, d = x.shape
    tm = _token_tile(t)
    row = lambda n: pl.BlockSpec((tm, n), lambda i: (i, 0))
    consts = [wg, wb0, wb1, wb2, wo]
    return pl.pallas_call(
        _merge_body,
        grid=(t // tm,),
        in_specs=[row(d), _const_spec((1, d)), row(512), row(512), row(1024)] + [_const_spec(a.shape) for a in consts],
        out_specs=row(d),
        out_shape=jax.ShapeDtypeStruct((t, d), F32),
        compiler_params=_cparams("parallel"),
        name="merge",
    )(x, g, o_fox, o_gm, o_nsa, *consts)


def _column(x, idx):
    lane = lax.broadcasted_iota(jnp.int32, x.shape, 1)
    return jnp.sum(jnp.where(lane == idx, x, 0.0), axis=1, keepdims=True)


def _logf_suffix_body(x_ref, o_ref):
    n, h, t = x_ref.shape
    tri = (lax.broadcasted_iota(jnp.int32, (t, t), 0) >= lax.broadcasted_iota(jnp.int32, (t, t), 1)).astype(F32)
    o_ref[...] = _dot(x_ref[...].reshape(n * h, t), tri, precision=HI).reshape(n, h, t)


def _logf_suffix(cache_lf, layer):
    n_pool = cache_lf.shape[1]
    blk = next(c for c in (64, 32, 16, 8, 4, 2, 1) if n_pool % c == 0)
    return pl.pallas_call(
        _logf_suffix_body,
        grid=(n_pool // blk,),
        in_specs=[pl.BlockSpec((None, blk, N_HEADS, PAGE), lambda i: (layer, i, 0, 0))],
        out_specs=pl.BlockSpec((blk, N_HEADS, PAGE), lambda i: (i, 0, 0)),
        out_shape=jax.ShapeDtypeStruct((n_pool, N_HEADS, PAGE), F32),
        compiler_params=_cparams("parallel"),
        name="logf_suffix",
    )(cache_lf)


def _fox_decode_body(pt_ref, *refs, n_pages):
    kv = refs[:n_pages]
    q_ref, new_ref, lfn_ref, sfx_ref, o_ref, s_ref = refs[n_pages:]
    b = pl.program_id(0)
    w = N_HEADS * HEAD_DIM
    row = lax.broadcasted_iota(jnp.int32, (N_HEADS, w), 0)
    lane = lax.broadcasted_iota(jnp.int32, (N_HEADS, w), 1)
    diag = _shr(lane, HEAD_DIM) == row
    qbd = jnp.where(diag, jnp.broadcast_to(q_ref[...].astype(F32), (N_HEADS, w)), 0.0).astype(BF16)
    last = lax.broadcasted_iota(jnp.int32, (N_HEADS, PAGE), 1) == PAGE - 1

    later = _column(lfn_ref[...], b)
    for i in reversed(range(n_pages)):
        incl = sfx_ref[pt_ref[b, i]]
        after = jnp.where(last, 0.0, pltpu.roll(incl, PAGE - 1, 1))
        s_ref[:, i * PAGE:(i + 1) * PAGE] = _dot(qbd, kv[i][0].astype(BF16)) + after + later
        later = later + incl[:, 0:1]

    k_new = new_ref[:, 0:w].astype(BF16).astype(F32)
    v_new = new_ref[:, w:2 * w].astype(BF16).astype(F32)
    s_new = jnp.sum(qbd.astype(F32) * k_new, axis=-1, keepdims=True)
    m = jnp.maximum(jnp.max(s_ref[...], axis=-1, keepdims=True), s_new)
    p_new = jnp.exp(s_new - m)
    l = p_new
    o = p_new.astype(BF16).astype(F32) * v_new
    for i in range(n_pages):
        p = jnp.exp(s_ref[:, i * PAGE:(i + 1) * PAGE] - m)
        l = l + jnp.sum(p, axis=-1, keepdims=True)
        o = o + _dot_nt(p.astype(BF16), kv[i][1].astype(BF16))
    o_ref[...] = jnp.sum(jnp.where(diag, o / l, 0.0), axis=0, keepdims=True).astype(o_ref.dtype)


def _fox_decode(page_table, cache_kv, lf_suffix, layer, fq_s, foxkv_s, lft_s):
    db, n_pages = page_table.shape
    w = N_HEADS * HEAD_DIM
    grid_spec = pltpu.PrefetchScalarGridSpec(
        num_scalar_prefetch=1,
        grid=(db,),
        in_specs=[pl.BlockSpec((None, None, 2, w, PAGE), lambda b, pt, i=i: (layer, pt[b, i], 0, 0, 0))
                  for i in range(n_pages)]
                 + [pl.BlockSpec((None, 1, w), lambda b, pt: (b, 0, 0)),
                    pl.BlockSpec((None, 1, 2 * w), lambda b, pt: (b, 0, 0)),
                    pl.BlockSpec(lft_s.shape, lambda b, pt: (0, 0)),
                    pl.BlockSpec(lf_suffix.shape, lambda b, pt: (0, 0, 0), pipeline_mode=pl.Buffered(1))],
        out_specs=pl.BlockSpec((None, 1, w), lambda b, pt: (b, 0, 0)),
        scratch_shapes=[pltpu.VMEM((N_HEADS, n_pages * PAGE), F32)],
    )
    return pl.pallas_call(
        functools.partial(_fox_decode_body, n_pages=n_pages),
        grid_spec=grid_spec,
        out_shape=jax.ShapeDtypeStruct((db, 1, w), BF16),
        compiler_params=_cparams("arbitrary"),
        name="fox_decode",
    )(page_table, *([cache_kv] * n_pages), fq_s, foxkv_s, lft_s, lf_suffix)


def _nsa_decode_body(pt_ref, *refs, n_pages, rps):
    pg = [refs[r * n_pages:(r + 1) * n_pages] for r in range(rps)]
    (win_ref, q_ref, new_ref, gate_ref, wnew_ref, wa_ref, wb_ref, pea_ref, peb_ref, w2_ref, kg0_ref, pet_ref,
     o_ref, wout_ref, xa_ref, xb_ref, s_ref) = refs[rps * n_pages:]
    past = n_pages * PAGE
    nhb = past // CMP_STRIDE
    wbuf = win_ref.shape[3]
    cur = past // SEL_BLOCK
    n_blk = cur + 1
    m = rps * N_HEADS

    def per_row(fn):
        return jnp.concatenate([fn(r) for r in range(rps)], axis=0)

    tp = lax.broadcasted_iota(jnp.int32, (2 * PAGE, 2 * PAGE), 0)
    r16 = tp & (CMP_STRIDE - 1)
    src = _shr(r16, 8) * PAGE + (r16 & 7) * CMP_STRIDE + _shr(tp, CMP_STRIDE)
    perm = (lax.broadcasted_iota(jnp.int32, (2 * PAGE, 2 * PAGE), 1) == src).astype(BF16)
    for r in range(rps):
        for j in range(n_pages // 2):
            pkv = jnp.concatenate([jnp.concatenate([pg[r][2 * j + pp][0], pg[r][2 * j + pp][1]], axis=0)
                                   for pp in range(2)], axis=1)
            for half, x_ref in enumerate((xa_ref, xb_ref)):
                rows = _dot_nt(perm, (pkv + pet_ref[half]).astype(BF16)).astype(BF16)
                for l in range(CMP_STRIDE):
                    x_ref[r * nhb + 16 * j:r * nhb + 16 * (j + 1), 256 * l:256 * (l + 1)] = rows[16 * l:16 * (l + 1), :]
    kc, vc = _compress_finish(xa_ref[...], xb_ref[...], wa_ref, wb_ref, w2_ref, kg0_ref)
    kc = kc.astype(BF16)
    vc = vc.astype(BF16)

    qf = q_ref[...].astype(F32)
    q = [qf[r * N_HEADS:(r + 1) * N_HEADS].astype(BF16) for r in range(rps)]
    row = lax.broadcasted_iota(jnp.int32, (m, 1), 0)
    head = row & (N_HEADS - 1)
    slope = _slope_col(head)
    lane = lax.broadcasted_iota(jnp.int32, (1, LANES), 1)

    def rows_of(x, r):
        return x[r * N_HEADS:(r + 1) * N_HEADS]

    def new_rows(lo):
        return per_row(lambda r: jnp.broadcast_to(new_ref[r][:, lo:lo + LANES].astype(F32), (N_HEADS, LANES)))

    end = lax.broadcasted_iota(jnp.int32, (1, nhb), 1) * CMP_STRIDE + (CMP_LEN - 1)
    s = per_row(lambda r: _dot_nt(q[r], kc[r * nhb:(r + 1) * nhb])) - slope * (past - end).astype(F32)
    p = _cmp_softmax(s, jnp.broadcast_to(end <= past, s.shape))
    o_cmp = per_row(lambda r: _dot(rows_of(p, r).astype(BF16), vc[r * nhb:(r + 1) * nhb]))
    same = (_shr(lax.broadcasted_iota(jnp.int32, (m, m), 0), N_REP)
            == _shr(lax.broadcasted_iota(jnp.int32, (m, m), 1), N_REP)).astype(F32)
    imp = _dot(same, p, precision=HI)
    sel = _select_blocks(imp, jnp.full((m, 1), cur, jnp.int32), n_blk)

    for r in range(rps):
        for i in range(n_pages):
            s_ref[r * N_HEADS:(r + 1) * N_HEADS, i * PAGE:(i + 1) * PAGE] = _dot(q[r], pg[r][i][2].astype(BF16))
    kpos = lax.broadcasted_iota(jnp.int32, (1, past), 1)
    s = s_ref[...] - slope * (past - kpos).astype(F32)
    s = jnp.where(_expand_sel(sel.astype(BF16), 0, past) > 0.5, s, NEG)
    new_ok = jnp.sum(jnp.where(lane == cur, sel.astype(F32), 0.0), axis=-1, keepdims=True) > 0.5
    s_new = jnp.where(new_ok, jnp.sum(qf * new_rows(0), axis=-1, keepdims=True), NEG)
    mx = jnp.maximum(jnp.max(s, axis=-1, keepdims=True), s_new)
    p_new = jnp.where(new_ok, jnp.exp(s_new - mx), 0.0)
    p = jnp.where(s > 0.5 * NEG, jnp.exp(s - mx), 0.0)
    l = jnp.sum(p, axis=-1, keepdims=True) + p_new
    s_ref[...] = p

    def sel_pv(r):
        acc = jnp.zeros((N_HEADS, LANES), F32)
        for i in range(n_pages):
            pr = s_ref[r * N_HEADS:(r + 1) * N_HEADS, i * PAGE:(i + 1) * PAGE]
            acc = acc + _dot_nt(pr.astype(BF16), pg[r][i][3].astype(BF16))
        return acc

    o_sel = (per_row(sel_pv) + p_new.astype(BF16).astype(F32) * new_rows(LANES)) / jnp.where(l > 0.0, l, 1.0)

    pos = lax.broadcasted_iota(jnp.int32, (1, wbuf), 1)
    d = wbuf - pos
    s = per_row(lambda r: _dot(q[r], win_ref[r, 0].astype(BF16))) - slope * d.astype(F32)
    s = jnp.where(d < WINDOW, s, NEG)
    s_new = jnp.sum(qf * new_rows(2 * LANES), axis=-1, keepdims=True)
    mx = jnp.maximum(jnp.max(s, axis=-1, keepdims=True), s_new)
    p = jnp.exp(s - mx)
    p_new = jnp.exp(s_new - mx)
    l = jnp.sum(p, axis=-1, keepdims=True) + p_new
    o_win = (per_row(lambda r: _dot_nt(rows_of(p, r).astype(BF16), win_ref[r, 1].astype(BF16)))
             + p_new.astype(BF16).astype(F32) * new_rows(3 * LANES)) / l

    o = gate_ref[:, 0:1] * o_cmp + gate_ref[:, 1:2] * o_sel + gate_ref[:, 2:3] * o_win
    in_group = _shr(lane, HEAD_DIM) == _shr(head, N_REP)
    o_ref[...] = jnp.where(in_group, o, 0.0).astype(o_ref.dtype)

    for r in range(rps):
        bidx = pl.program_id(0) * rps + r
        for kv in range(2):
            wout_ref[r, kv] = jnp.where(pos == wbuf - 1, _column(wnew_ref[kv * LANES:(kv + 1) * LANES, :], bidx),
                                        pltpu.roll(win_ref[r, kv], wbuf - 1, 1))


def _nsa_decode(page_table, cache_nsa, win_state, layer, nq_s, nsab_s, gates_s, wint_s, cw):
    db, n_pages = page_table.shape
    wbuf = win_state.shape[4]
    rps = DECODE_ROWS
    assert db % rps == 0 and n_pages % 2 == 0
    m = rps * N_HEADS
    nhb = n_pages * PAGE // CMP_STRIDE
    in_specs = [pl.BlockSpec((None, None, 4, LANES, PAGE), lambda b, pt, i=i, r=r: (layer, pt[b * rps + r, i], 0, 0, 0))
                for r in range(rps) for i in range(n_pages)]
    in_specs += [pl.BlockSpec((None, rps, 2, LANES, wbuf), lambda b, pt: (layer, b, 0, 0, 0)),
                 pl.BlockSpec((m, LANES), lambda b, pt: (b, 0)),
                 pl.BlockSpec((rps, 1, 512), lambda b, pt: (b, 0, 0)),
                 pl.BlockSpec((m, 3), lambda b, pt: (b, 0)),
                 pl.BlockSpec(wint_s.shape, lambda b, pt: (0, 0))]
    in_specs += [pl.BlockSpec(a.shape, lambda b, pt, nd=a.ndim: (0,) * nd, pipeline_mode=pl.Buffered(1)) for a in cw]
    grid_spec = pltpu.PrefetchScalarGridSpec(
        num_scalar_prefetch=1,
        grid=(db // rps,),
        in_specs=in_specs,
        out_specs=[pl.BlockSpec((m, LANES), lambda b, pt: (b, 0)),
                   pl.BlockSpec((rps, 2, LANES, wbuf), lambda b, pt: (b, 0, 0, 0))],
        scratch_shapes=[pltpu.VMEM((rps * nhb, CMP_STRIDE * 256), BF16),
                        pltpu.VMEM((rps * nhb, CMP_STRIDE * 256), BF16),
                        pltpu.VMEM((m, n_pages * PAGE), F32)],
    )
    return pl.pallas_call(
        functools.partial(_nsa_decode_body, n_pages=n_pages, rps=rps),
        grid_spec=grid_spec,
        out_shape=[jax.ShapeDtypeStruct((db * N_HEADS, LANES), BF16),
                   jax.ShapeDtypeStruct((db, 2, LANES, wbuf), F32)],
        compiler_params=_cparams("arbitrary"),
        name="nsa_decode",
    )(page_table, *([cache_nsa] * (rps * n_pages)), win_state, nq_s, nsab_s, gates_s, wint_s, *cw)


O_F, O_GM, O_NQ, O_NKV, O_NG, O_MG = 1536, 1544, 2568, 3080, 3848, 3872
_HEAD_PLACE = (np.arange(N_HEADS)[:, None] // N_REP == np.arange(N_GROUPS)[None, :]).astype(np.float32)


def _prep_w_in(w):
    d = w.shape[0]
    wq = w[:, O_NQ:O_NKV].reshape(d, N_HEADS, 1, HEAD_DIM)
    wq = (wq * _HEAD_PLACE[None, :, :, None]).reshape(d, N_HEADS * LANES)
    small = jnp.concatenate([w[:, O_F:O_GM], w[:, O_NG:O_MG], jnp.zeros((d, LANES - 32), w.dtype)], axis=1)
    w_tok = jnp.concatenate([w[:, 0:O_F], w[:, O_GM:O_NQ], wq, w[:, O_NKV:O_NG], small], axis=1).astype(BF16)
    wt = w.T
    w_feat = jnp.concatenate([wt[512:O_F], wt[O_NKV:O_NG], wt[O_F:O_GM], jnp.zeros((8, d), w.dtype)],
                             axis=0).astype(BF16)
    return w_tok, w_feat, w[:, O_MG:].astype(BF16)


def _pad_heads(g):
    return (_HEAD_PLACE[:, :, None] * g[None, None, :]).reshape(1, N_HEADS * LANES)


def _prep_compress(w1, w2, pe, kg0):
    w1r = w1.reshape(2, 2, CMP_STRIDE, HEAD_DIM, HEAD_DIM)
    wl = jnp.zeros((2, CMP_STRIDE, 256, 256), F32)
    w2bd = jnp.zeros((256, 256), F32)
    for c in range(2):
        for g in range(N_GROUPS):
            sl = slice((2 * c + g) * HEAD_DIM, (2 * c + g + 1) * HEAD_DIM)
            wl = wl.at[:, :, sl, sl].set(w1r[c])
            w2bd = w2bd.at[sl, sl].set(w2[c])
    wl = wl.reshape(2, CMP_STRIDE * 256, 256).astype(BF16)
    pex = jnp.broadcast_to(pe.transpose(1, 0, 2)[:, :, None, :], (CMP_LEN, 2, N_GROUPS, HEAD_DIM))
    pex = pex.reshape(2, CMP_STRIDE, 1, 256)
    pet = jnp.tile(pex.reshape(2, CMP_STRIDE, 256).transpose(0, 2, 1), (1, 1, 2 * PAGE // CMP_STRIDE))
    return (wl[0], wl[1], pex[0], pex[1], w2bd.astype(BF16), jnp.tile(kg0, 2)[None, :], pet)


def _layer(xp, xs, lw, page_table, caches, layer, dims):
    b, s, db = dims

    xp = _ffn(xp, lw['ffn1_norm'], lw['ffn1_w_gu'], lw['ffn1_w_down'])
    xs = _ffn(xs, lw['ffn1_norm'], lw['ffn1_w_gu'], lw['ffn1_w_down'])

    pw = lw['proj']
    (fq, foxt, foxtb, gmu, gmv, nq, nsat, wint, nsatb, cmpraw, lft, gates) = _proj(xp, pw, b, s, False)
    (fq_s, foxt_s, _, gmu_s, gmv_s, nq_s, nsat_s, wint_s, _, _, lft_s, gates_s,
     foxkv_s, nsab_s) = _proj(xs, pw, 1, db, True)

    c = _cumsum(lft).reshape(b, 4, 2, s)
    o_fox_p = _fox_prompt(fq.reshape(b, s, 512), foxtb, c)
    o_gm_p = _gmlp_prompt(gmu, gmv, lw['gmlp_w_s'], lw['gmlp_bexp'])
    cmpkv = _compress_prompt(cmpraw.reshape(b, s, 256), lw['cw'])
    o_nsa_p = _nsa_prompt(nq.reshape(b, s, 1024), cmpkv, nsatb, lw['nsa_aug'], gates.reshape(b, s, LANES))
    xp = _merge(xp, lw['mix_norm'], o_fox_p.reshape(b * s, 512), o_gm_p, o_nsa_p.reshape(b * s, 1024),
                lw['w_gate'], lw['wb0'], lw['wb1'], lw['wb2'], lw['w_out'])

    cache_fox_kv, cache_fox_lf, cache_nsa, win_state = caches
    o_fox_s = _fox_decode(page_table, cache_fox_kv, _logf_suffix(cache_fox_lf, layer), layer, fq_s.reshape(db, 1, 512),
                          foxkv_s.reshape(db, 1, 1024), lft_s.reshape(N_HEADS, db))
    o_gm_s = _gmlp_sample(gmu_s, gmv_s, lw['gmlp_w00'], lw['gmlp_b0'])
    g3 = gates_s[:, 8:32].reshape(db, 3, N_HEADS).transpose(0, 2, 1).reshape(db * N_HEADS, 3)
    o_nsa_s, win_next = _nsa_decode(page_table, cache_nsa, win_state, layer, nq_s.reshape(db * N_HEADS, LANES),
                                    nsab_s.reshape(db, 1, 512), g3, wint_s.reshape(256, db), lw['cw'])
    xs = _merge(xs, lw['mix_norm'], o_fox_s.reshape(db, 512), o_gm_s, o_nsa_s.reshape(db, 1024),
                lw['w_gate'], lw['wb0'], lw['wb1'], lw['wb2'], lw['w_out'])

    xp = _ffn(xp, lw['ffn2_norm'], lw['ffn2_w_gu'], lw['ffn2_w_down'])
    xs = _ffn(xs, lw['ffn2_norm'], lw['ffn2_w_gu'], lw['ffn2_w_down'])

    state = dict(
        fox_kv_p=foxt, fox_kv_s=foxt_s, fox_logf_p=lft, fox_logf_s=lft_s, nsa_kv_p=nsat, nsa_kv_s=nsat_s,
        win_p=wint[:, :, s - min(WINDOW, s):], win_s=win_next,
        gmlp_v_p=gmv.reshape(b, s, 512)[:, s - min(CHUNK, s):], gmlp_v_s=gmv_s.reshape(db, 1, 512),
    )
    return xp, xs, state


def kernel(x_prompt, x_sample, page_table, cache_fox_kv, cache_fox_logf, cache_nsa_kv, state_nsa_win, ffn1_norm, ffn1_w_gu, ffn1_w_down, mix_norm, w_in, fox_b_f, fox_qk_gain, gmlp_v_gain, gmlp_w_s, gmlp_b_s, nsa_q_gain, nsa_k_gain, nsa_cmp_pe, nsa_cmp_w1, nsa_cmp_w2, w_branch, w_out, ffn2_norm, ffn2_w_gu, ffn2_w_down):
    b, s, d = x_prompt.shape
    db = x_sample.shape[0]
    depth = w_in.shape[0]
    n_pool = cache_fox_kv.shape[1]
    wb = state_nsa_win.shape[2]
    xp = x_prompt.reshape(b * s, d)
    xs = x_sample.reshape(db, d)

    caches = (cache_fox_kv.transpose(0, 1, 3, 4, 5, 2).reshape(depth, n_pool, 2, N_HEADS * HEAD_DIM, PAGE),
              cache_fox_logf.transpose(0, 1, 3, 2),
              cache_nsa_kv.transpose(0, 1, 3, 4, 5, 2).reshape(depth, n_pool, 4, LANES, PAGE),
              state_nsa_win.transpose(0, 1, 3, 4, 5, 2).reshape(depth, db, 2, LANES, wb))

    nsa_rows = (np.arange(N_HEADS * LANES) % LANES) // HEAD_DIM == (np.arange(N_HEADS * LANES) // LANES) // N_REP
    col = lambda g: g[:, None]
    nsa_aug = _nsa_aug(s)
    states = []
    for l in range(depth):
        w_tok, w_feat, w_gate = _prep_w_in(w_in[l])
        wb2 = jnp.zeros((N_HEADS * LANES, d), F32).at[np.nonzero(nsa_rows)[0]].set(w_branch[l, 2])
        proj = (mix_norm[l][None, :], w_tok, w_feat,
                jnp.tile(fox_qk_gain[l, 0], 2)[None, :], jnp.tile(fox_qk_gain[l, 1], 2)[None, :],
                col(fox_qk_gain[l, 1]), gmlp_v_gain[l][None, :], _pad_heads(nsa_q_gain[l]),
                jnp.tile(nsa_k_gain[l, 1], 2)[None, :], jnp.tile(nsa_k_gain[l, 2], 2)[None, :],
                col(nsa_k_gain[l, 1]), col(nsa_k_gain[l, 2]), col(fox_b_f[l]))
        lw = dict(
            ffn1_norm=ffn1_norm[l][None, :], ffn1_w_gu=ffn1_w_gu[l].astype(BF16), ffn1_w_down=ffn1_w_down[l].astype(BF16),
            ffn2_norm=ffn2_norm[l][None, :], ffn2_w_gu=ffn2_w_gu[l].astype(BF16), ffn2_w_down=ffn2_w_down[l].astype(BF16),
            mix_norm=mix_norm[l][None, :], proj=proj, w_gate=w_gate,
            gmlp_w_s=gmlp_w_s[l], gmlp_bexp=jnp.repeat(gmlp_b_s[l].T, HEAD_DIM, axis=1),
            gmlp_w00=jnp.repeat(gmlp_w_s[l, :, 0, 0], HEAD_DIM)[None, :],
            gmlp_b0=jnp.repeat(gmlp_b_s[l, :, 0], HEAD_DIM)[None, :],
            cw=_prep_compress(nsa_cmp_w1[l], nsa_cmp_w2[l], nsa_cmp_pe[l], nsa_k_gain[l, 0]),
            wb0=w_branch[l, 0].astype(BF16), wb1=w_branch[l, 1].astype(BF16), wb2=wb2.astype(BF16),
            w_out=w_out[l].astype(BF16), nsa_aug=nsa_aug,
        )
        xp, xs, st = _layer(xp, xs, lw, page_table, caches, l, (b, s, db))
        states.append(st)

    def stack(name):
        return jnp.stack([st[name] for st in states])

    def tok_major(a, *feat_dims):
        return jnp.moveaxis(a.reshape(a.shape[:2] + feat_dims + a.shape[3:]), -1, 2)

    def tok_major_s(a, *feat_dims):
        return jnp.swapaxes(tok_major(a, *feat_dims), 1, 2)

    return (xp.reshape(b, s, d), xs.reshape(db, 1, d),
            tok_major(stack('fox_kv_p'), 2, N_HEADS, HEAD_DIM), tok_major_s(stack('fox_kv_s'), 2, N_HEADS, HEAD_DIM),
            tok_major(stack('fox_logf_p'), N_HEADS), tok_major_s(stack('fox_logf_s'), N_HEADS),
            tok_major(stack('nsa_kv_p'), 4, N_GROUPS, HEAD_DIM), tok_major_s(stack('nsa_kv_s'), 4, N_GROUPS, HEAD_DIM),
            tok_major(stack('win_p'), 2, N_GROUPS, HEAD_DIM),
            tok_major(stack('win_s').reshape(depth, db, 2 * LANES, wb), 2, N_GROUPS, HEAD_DIM),
            stack('gmlp_v_p'), stack('gmlp_v_s'))
```

```python
import functools

import jax
import jax.numpy as jnp
import numpy as np
from jax import lax
from jax.experimental import pallas as pl
from jax.experimental.pallas import tpu as pltpu

F32 = jnp.float32
BF16 = jnp.bfloat16

HEAD_DIM = 64
LANES = 128
PAGE = 128
CHUNK = 128
GMLP_CHUNKS_PER_STEP = 4
DECODE_ROWS = 2
Q_BLOCK = 128
FOX_TQ = 512
ATT_TK = 512
CMP_LEN = 32
CMP_STRIDE = 16
SEL_BLOCK = 64
N_SEL = 8
WINDOW = 512
N_GROUPS = 2
N_REP = 4
N_HEADS = 8
FORCE_BONUS = 1.0e4
NEG = -1.0e30
EPS = 1e-6
SCALE = HEAD_DIM ** -0.5
VMEM_LIMIT = 56 * 1024 * 1024
HI = lax.Precision.HIGHEST


def _cparams(*sem):
    return pltpu.CompilerParams(dimension_semantics=sem, vmem_limit_bytes=VMEM_LIMIT)


def _dot(a, b, precision=None):
    return jnp.dot(a, b, preferred_element_type=F32, precision=precision)


def _dot_nt(a, b):
    return lax.dot_general(a, b, (((1,), (1,)), ((), ())), preferred_element_type=F32)


def _shr(x, pow2):
    return jnp.right_shift(x, int(pow2).bit_length() - 1)


def _rms_rows(x, g):
    ms = jnp.mean(x * x, axis=-1, keepdims=True)
    return x * lax.rsqrt(ms + EPS) * g


def _headnorm_pair(zb, gain):
    lo = lax.broadcasted_iota(jnp.int32, zb.shape, 1) < HEAD_DIM
    sq = zb * zb
    s_lo = jnp.sum(jnp.where(lo, sq, 0.0), axis=-1, keepdims=True)
    s_hi = jnp.sum(jnp.where(lo, 0.0, sq), axis=-1, keepdims=True)
    ms = jnp.where(lo, s_lo, s_hi) * (1.0 / HEAD_DIM)
    return zb * lax.rsqrt(ms + EPS) * gain


def _headnorm_cols(zt, gain_col):
    ms = jnp.mean(zt * zt, axis=0, keepdims=True)
    return zt * lax.rsqrt(ms + EPS) * gain_col


def _silu(x):
    return x * jax.nn.sigmoid(x)


def _gelu_tanh(x):
    return 0.5 * x * (1.0 + jnp.tanh(np.sqrt(2.0 / np.pi) * (x + 0.044715 * (x * x * x))))


def _log_sigmoid(x):
    return jnp.minimum(x, 0.0) - jnp.log1p(jnp.exp(-jnp.abs(x)))


def _token_tile(t):
    for tm in (512, 384, 256, 128):
        if t % tm == 0:
            return tm
    raise ValueError(f"token count {t} is not a multiple of 128")


def _const_spec(shape):
    nd = len(shape)
    return pl.BlockSpec(shape, lambda *_: (0,) * nd, pipeline_mode=pl.Buffered(1))


def _ffn_body(x_ref, g_ref, wgu_ref, wd_ref, o_ref, *, d_ff, fc):
    x = x_ref[...]
    h = _rms_rows(x, g_ref[...]).astype(BF16)
    acc = jnp.zeros_like(x)
    for c in range(d_ff // fc):
        g = _dot(h, wgu_ref[:, c * fc:(c + 1) * fc])
        u = _dot(h, wgu_ref[:, d_ff + c * fc:d_ff + (c + 1) * fc])
        a = (_silu(g) * u).astype(BF16)
        acc = acc + _dot(a, wd_ref[c * fc:(c + 1) * fc, :])
    o_ref[...] = x + 0.5 * acc


def _ffn(x, g, wgu, wd):
    t, d = x.shape
    d_ff = wd.shape[0]
    tm = _token_tile(t)
    fc = 256 if d_ff % 256 == 0 else 128
    return pl.pallas_call(
        functools.partial(_ffn_body, d_ff=d_ff, fc=fc),
        grid=(t // tm,),
        in_specs=[pl.BlockSpec((tm, d), lambda i: (i, 0)),
                  _const_spec((1, d)),
                  _const_spec((d, 2 * d_ff)),
                  _const_spec((d_ff, d))],
        out_specs=pl.BlockSpec((tm, d), lambda i: (i, 0)),
        out_shape=jax.ShapeDtypeStruct((t, d), F32),
        compiler_params=_cparams("parallel"),
        name="ffn",
    )(x, g, wgu, wd)


C_FOX = 0
C_GM = 1536
C_NQ = 2560
C_NKV = 3584
C_SMALL = 4352
C_END = 4480
R_NKV = 1024
R_F = 1792
R_END = 1808


def _proj_body(x_ref, g_ref, w_ref, wt_ref, gq_ref, gk_ref, gkc_ref, gv_ref, gnq_ref, gk1_ref, gk2_ref,
               gk1c_ref, gk2c_ref, bfc_ref, *rest, sample, n_prev):
    out_refs = rest[n_prev:]
    (fq_ref, foxt_ref, foxtb_ref, gmu_ref, gmv_ref, nq_ref, nsat_ref, wint_ref, nsatb_ref, cmpraw_ref,
     lft_ref, gate_ref) = out_refs[:12]
    h = _rms_rows(x_ref[...], g_ref[...]).astype(BF16)

    z = _dot(h, w_ref[:, 0:512])
    for j in range(4):
        sl = slice(j * LANES, (j + 1) * LANES)
        fq_ref[:, sl] = (_headnorm_pair(z[:, sl], gq_ref[...]) * SCALE).astype(BF16)
    zt = _dot_nt(wt_ref[0:R_NKV, :], h)
    for hh in range(N_HEADS):
        rs = slice(hh * HEAD_DIM, (hh + 1) * HEAD_DIM)
        kt = _headnorm_cols(zt[rs], gkc_ref[...])
        foxt_ref[rs, :] = kt
        foxtb_ref[rs, :] = kt.astype(BF16)
    foxt_ref[512:1024, :] = zt[512:1024]
    foxtb_ref[512:1024, :] = zt[512:1024].astype(BF16)

    a = _gelu_tanh(_dot(h, w_ref[:, C_GM:C_NQ]))
    gmu_ref[...] = a[:, 0:512]
    gmv_ref[...] = _rms_rows(a[:, 512:1024], gv_ref[...])

    z = _dot(h, w_ref[:, C_NQ:C_NKV])
    for hh in range(N_HEADS):
        sl = slice(hh * LANES, (hh + 1) * LANES)
        zb = z[:, sl]
        ms = jnp.sum(zb * zb, axis=-1, keepdims=True) * (1.0 / HEAD_DIM)
        nq_ref[:, sl] = (zb * lax.rsqrt(ms + EPS) * (gnq_ref[:, sl] * SCALE)).astype(BF16)

    zt = _dot_nt(wt_ref[R_NKV:R_F, :], h)
    nsat_ref[0:256, :] = zt[0:256]
    nsat_ref[384:512, :] = zt[384:512]
    nsatb_ref[128:256, :] = zt[384:512].astype(BF16)
    wint_ref[128:256, :] = zt[640:768]
    nsatb_ref[384:512, :] = zt[640:768].astype(BF16)
    for g in range(N_GROUPS):
        sk = _headnorm_cols(zt[256 + g * HEAD_DIM:256 + (g + 1) * HEAD_DIM], gk1c_ref[...])
        nsat_ref[256 + g * HEAD_DIM:256 + (g + 1) * HEAD_DIM, :] = sk
        nsatb_ref[g * HEAD_DIM:(g + 1) * HEAD_DIM, :] = sk.astype(BF16)
        wk = _headnorm_cols(zt[512 + g * HEAD_DIM:512 + (g + 1) * HEAD_DIM], gk2c_ref[...])
        wint_ref[g * HEAD_DIM:(g + 1) * HEAD_DIM, :] = wk
        nsatb_ref[256 + g * HEAD_DIM:256 + (g + 1) * HEAD_DIM, :] = wk.astype(BF16)

    cmpraw_ref[...] = _dot(h, w_ref[:, C_NKV:C_NKV + 256])

    zt = _dot_nt(wt_ref[R_F:R_END, :], h)
    lft_ref[...] = _log_sigmoid(zt[0:N_HEADS] + bfc_ref[...])
    gate_ref[...] = jax.nn.sigmoid(_dot(h, w_ref[:, C_SMALL:C_END]))

    if sample:
        foxs_ref, nsabs_ref = out_refs[12:]
        z = _dot(h, w_ref[:, 512:C_GM])
        for j in range(4):
            sl = slice(j * LANES, (j + 1) * LANES)
            foxs_ref[:, sl] = _headnorm_pair(z[:, sl], gk_ref[...])
        foxs_ref[:, 512:1024] = z[:, 512:1024]
        z = _dot(h, w_ref[:, C_NKV + 256:C_SMALL])
        nsabs_ref[:, 0:128] = _headnorm_pair(z[:, 0:128], gk1_ref[...]).astype(BF16)
        nsabs_ref[:, 128:256] = z[:, 128:256].astype(BF16)
        nsabs_ref[:, 256:384] = _headnorm_pair(z[:, 256:384], gk2_ref[...]).astype(BF16)
        nsabs_ref[:, 384:512] = z[:, 384:512].astype(BF16)


STATE_OUTS = (1, 6, 7, 10)


def _proj(x, pw, b, s, sample, layer=0, depth=1, prev=None):
    t, d = x.shape
    tm = _token_tile(s)
    nst = s // tm
    tok = lambda n, dt: (pl.BlockSpec((tm, n), lambda i: (i, 0)), jax.ShapeDtypeStruct((t, n), dt))
    feat = lambda n, dt: (pl.BlockSpec((None, n, tm), lambda i: (i // nst, 0, i % nst)),
                          jax.ShapeDtypeStruct((b, n, s), dt))
    state = lambda n: (pl.BlockSpec((None, None, n, tm), lambda i: (layer, i // nst, 0, i % nst)),
                       jax.ShapeDtypeStruct((depth, b, n, s), F32))
    outs = [tok(512, BF16), feat(1024, F32), feat(1024, BF16), tok(512, F32), tok(512, F32), tok(1024, BF16),
            feat(512, F32), feat(256, F32), feat(512, BF16), tok(256, F32), feat(N_HEADS, F32), tok(LANES, F32)]
    if sample:
        outs += [tok(1024, F32), tok(512, BF16)]
    else:
        for k in STATE_OUTS:
            outs[k] = state(outs[k][1].shape[1])
    prev = () if prev is None else tuple(prev)
    return pl.pallas_call(
        functools.partial(_proj_body, sample=sample, n_prev=len(prev)),
        grid=(t // tm,),
        in_specs=[pl.BlockSpec((tm, d), lambda i: (i, 0))] + [_const_spec(a.shape) for a in pw]
                 + [pl.BlockSpec(memory_space=pl.ANY)] * len(prev),
        out_specs=[o[0] for o in outs],
        out_shape=[o[1] for o in outs],
        input_output_aliases={1 + len(pw) + k: STATE_OUTS[k] for k in range(len(prev))},
        compiler_params=_cparams("parallel"),
        name="in_proj_sample" if sample else "in_proj",
    )(x, *pw, *prev)


def _cumsum_body(x_ref, o_ref, *, blk):
    s = x_ref.shape[1]
    r = lax.broadcasted_iota(jnp.int32, (blk, blk), 0)
    c = lax.broadcasted_iota(jnp.int32, (blk, blk), 1)
    tri = (r <= c).astype(F32)
    carry = jnp.zeros((x_ref.shape[0], 1), F32)
    for i in range(s // blk):
        cs = _dot(x_ref[:, i * blk:(i + 1) * blk], tri, precision=HI) + carry
        o_ref[:, i * blk:(i + 1) * blk] = cs
        carry = cs[:, blk - 1:blk]


def _cumsum(x, layer):
    _, b, n, s = x.shape
    return pl.pallas_call(
        functools.partial(_cumsum_body, blk=min(256, s)),
        grid=(b,),
        in_specs=[pl.BlockSpec((None, None, n, s), lambda i: (layer, i, 0, 0))],
        out_specs=pl.BlockSpec((None, n, s), lambda i: (i, 0, 0)),
        out_shape=jax.ShapeDtypeStruct((b, n, s), F32),
        compiler_params=_cparams("parallel"),
        name="logf_cumsum",
    )(x)


def _fox_body(q_ref, k_ref, v_ref, c_ref, o_ref, *, tq, tk):
    qi = pl.program_id(2)
    q = q_ref[...]
    lo = lax.broadcasted_iota(jnp.int32, (tq, LANES), 1) < HEAD_DIM
    zero = jnp.zeros_like(q)
    qs = jnp.concatenate([jnp.where(lo, q, zero), jnp.where(lo, zero, q)], axis=0)
    qpos = qi * tq + lax.broadcasted_iota(jnp.int32, (tq, 1), 0)

    def block(kb, carry, masked):
        t0 = pl.multiple_of(kb * tk, tk)
        m_old, l_old, acc = carry
        bias = -c_ref[:, pl.ds(t0, tk)]
        s = _dot(qs, k_ref[:, pl.ds(t0, tk)]).reshape(2, tq, tk) + bias[:, None, :]
        if masked:
            kpos = t0 + lax.broadcasted_iota(jnp.int32, (1, tk), 1)
            s = jnp.where((kpos <= qpos)[None], s, NEG)
        s = s.reshape(2 * tq, tk)
        m_new = jnp.maximum(m_old, jnp.max(s, axis=-1, keepdims=True))
        alpha = jnp.exp(m_old - m_new)
        p = jnp.exp(s - m_new)
        l_new = alpha * l_old + jnp.sum(p, axis=-1, keepdims=True)
        return m_new, l_new, alpha * acc + _dot_nt(p.astype(BF16), v_ref[:, pl.ds(t0, tk)])

    n_kb = _shr(qi * tq + tq + tk - 1, tk)
    carry = (jnp.full((2 * tq, 1), NEG, F32), jnp.zeros((2 * tq, 1), F32), jnp.zeros((2 * tq, LANES), F32))
    carry = lax.fori_loop(0, n_kb - 1, lambda kb, c: block(kb, c, False), carry)
    _, l, acc = block(n_kb - 1, carry, True)
    o = acc / l
    o_ref[...] = jnp.where(lo, o[0:tq], o[tq:2 * tq]).astype(o_ref.dtype)


def _fox_prompt(fq, foxtb, c):
    b, s, _ = fq.shape
    tq = min(FOX_TQ, s)
    tk = min(ATT_TK, s)
    return pl.pallas_call(
        functools.partial(_fox_body, tq=tq, tk=tk),
        grid=(b, 4, s // tq),
        in_specs=[pl.BlockSpec((None, tq, LANES), lambda bi, p, qi: (bi, qi, p)),
                  pl.BlockSpec((None, LANES, s), lambda bi, p, qi: (bi, p, 0)),
                  pl.BlockSpec((None, LANES, s), lambda bi, p, qi: (bi, 4 + p, 0)),
                  pl.BlockSpec((None, None, 2, s), lambda bi, p, qi: (bi, p, 0, 0))],
        out_specs=pl.BlockSpec((None, tq, LANES), lambda bi, p, qi: (bi, qi, p)),
        out_shape=jax.ShapeDtypeStruct((b, s, 512), BF16),
        compiler_params=_cparams("parallel", "parallel", "arbitrary"),
        name="fox_prompt",
    )(fq, foxtb, foxtb, c)


def _gmlp_body(u_ref, v_ref, ws_ref, b_ref, o_ref):
    n = ws_ref.shape[1]
    r = lax.broadcasted_iota(jnp.int32, (n, n), 0)
    c = lax.broadcasted_iota(jnp.int32, (n, n), 1)
    lo = lax.broadcasted_iota(jnp.int32, (n, LANES), 1) < HEAD_DIM
    for j in range(4):
        sl = slice(j * LANES, (j + 1) * LANES)
        w0 = jnp.where(r >= c, ws_ref[2 * j], 0.0).astype(BF16)
        w1 = jnp.where(r >= c, ws_ref[2 * j + 1], 0.0).astype(BF16)
        for ch in range(u_ref.shape[0] // n):
            rs = slice(ch * n, (ch + 1) * n)
            vp = v_ref[rs, sl].astype(BF16)
            mixed = jnp.where(lo, _dot(w0, vp), _dot(w1, vp)) + b_ref[:, sl]
            o_ref[rs, sl] = (u_ref[rs, sl] * mixed).astype(o_ref.dtype)


def _gmlp_prompt(gmu, gmv, ws, bexp):
    t = gmu.shape[0]
    rows = GMLP_CHUNKS_PER_STEP * CHUNK if t % (GMLP_CHUNKS_PER_STEP * CHUNK) == 0 else CHUNK
    return pl.pallas_call(
        _gmlp_body,
        grid=(t // rows,),
        in_specs=[pl.BlockSpec((rows, 512), lambda i: (i, 0)),
                  pl.BlockSpec((rows, 512), lambda i: (i, 0)),
                  _const_spec(ws.shape), _const_spec(bexp.shape)],
        out_specs=pl.BlockSpec((rows, 512), lambda i: (i, 0)),
        out_shape=jax.ShapeDtypeStruct((t, 512), BF16),
        compiler_params=_cparams("parallel"),
        name="gmlp_prompt",
    )(gmu, gmv, ws, bexp)


def _gmlp_first_body(u_ref, v_ref, w_ref, b_ref, o_ref):
    o_ref[...] = (u_ref[...] * (v_ref[...] * w_ref[...] + b_ref[...])).astype(o_ref.dtype)


def _gmlp_sample(gmu, gmv, wrow, brow):
    n = gmu.shape[0]
    full = pl.BlockSpec((n, 512), lambda i: (0, 0))
    return pl.pallas_call(
        _gmlp_first_body,
        grid=(1,),
        in_specs=[full, full, _const_spec((1, 512)), _const_spec((1, 512))],
        out_specs=full,
        out_shape=jax.ShapeDtypeStruct((n, 512), BF16),
        compiler_params=_cparams("arbitrary"),
        name="gmlp_sample",
    )(gmu, gmv, wrow, brow)


def _compress_finish(xa, xb, wa_ref, wb_ref, w2_ref, kg0_ref):
    nhb = xa.shape[0]
    a = _dot(xa, wa_ref[...])
    b = _dot(xb, wb_ref[...])
    pre = a + pltpu.roll(b, nhb - 1, 0)
    out = _dot(_silu(pre).astype(BF16), w2_ref[...])
    kc = _headnorm_pair(out[:, 0:LANES], kg0_ref[...])
    return kc, out[:, LANES:2 * LANES]


def _compress(xk_ref, xv_ref, wa_ref, wb_ref, pea_ref, peb_ref, w2_ref, kg0_ref):
    nhb = xk_ref.shape[0] // CMP_STRIDE
    xa, xb = [], []
    for l in range(CMP_STRIDE):
        xl = jnp.concatenate([xk_ref[pl.ds(l, nhb, stride=CMP_STRIDE), :],
                              xv_ref[pl.ds(l, nhb, stride=CMP_STRIDE), :]], axis=1)
        xa.append((xl + pea_ref[l]).astype(BF16))
        xb.append((xl + peb_ref[l]).astype(BF16))
    return _compress_finish(jnp.concatenate(xa, axis=1), jnp.concatenate(xb, axis=1), wa_ref, wb_ref, w2_ref, kg0_ref)


def _compress_body(xk_ref, xv_ref, wa_ref, wb_ref, pea_ref, peb_ref, w2_ref, kg0_ref, pet_ref, o_ref):
    kc, vc = _compress(xk_ref, xv_ref, wa_ref, wb_ref, pea_ref, peb_ref, w2_ref, kg0_ref)
    o_ref[:, 0:LANES] = kc.astype(BF16)
    o_ref[:, LANES:2 * LANES] = vc.astype(BF16)


def _compress_prompt(cmpraw, cw):
    b, s, _ = cmpraw.shape
    nhb = s // CMP_STRIDE
    return pl.pallas_call(
        _compress_body,
        grid=(b,),
        in_specs=[pl.BlockSpec((None, s, LANES), lambda i: (i, 0, 0)),
                  pl.BlockSpec((None, s, LANES), lambda i: (i, 0, 1))] + [_const_spec(a.shape) for a in cw],
        out_specs=pl.BlockSpec((None, nhb, 256), lambda i: (i, 0, 0)),
        out_shape=jax.ShapeDtypeStruct((b, nhb, 256), BF16),
        compiler_params=_cparams("parallel"),
        name="nsa_compress",
    )(cmpraw, cmpraw, *cw)


def _slope_col(head):
    out = jnp.zeros(head.shape, F32)
    for hh in range(N_HEADS):
        out = jnp.where(head == hh, 2.0 ** (-(hh + 1)), out)
    return out


def _slope_rows(g):
    return jnp.concatenate([jnp.full((Q_BLOCK, 1), 2.0 ** (-(N_REP * g + r + 1)), F32) for r in range(N_REP)], axis=0)


def _cmp_softmax(s, mask):
    s = jnp.where(mask, s, NEG)
    m = jnp.max(s, axis=-1, keepdims=True)
    e = jnp.where(mask, jnp.exp(s - m), 0.0)
    l = jnp.sum(e, axis=-1, keepdims=True)
    return e / jnp.where(l > 0.0, l, 1.0)


def _select_blocks(imp, cur, n_blk):
    m, n = imp.shape
    pool = (_shr(lax.broadcasted_iota(jnp.int32, (n, LANES), 0), SEL_BLOCK // CMP_STRIDE)
            == lax.broadcasted_iota(jnp.int32, (n, LANES), 1)).astype(F32)
    score = _dot(imp, pool, precision=HI)
    j = lax.broadcasted_iota(jnp.int32, (m, LANES), 1)
    forced = (j == 0) | (j == cur) | (j == cur - 1)
    score = jnp.where(forced, score + FORCE_BONUS, score)
    score = jnp.where(j > cur, NEG, score)
    rank = jnp.zeros((m, LANES), F32)
    for i in range(n_blk):
        ci = score[:, i:i + 1]
        ahead = (ci > score) | ((ci == score) & (j > i))
        rank = rank + ahead.astype(F32)
    return (rank < float(min(N_SEL, n_blk))) & (score > 0.5 * NEG)


def _expand_sel(sel_bf, t0, n):
    jj = lax.broadcasted_iota(jnp.int32, (LANES, n), 0)
    tt = t0 + lax.broadcasted_iota(jnp.int32, (LANES, n), 1)
    e = (_shr(tt, SEL_BLOCK) == jj).astype(BF16)
    return _dot(sel_bf, e)


AUG_SEL = 16
MAX_SEL_BLOCKS = 32


def _aug_rows(pos):
    n = pos.shape[1]
    r = lax.broadcasted_iota(jnp.int32, (HEAD_DIM, n), 0)
    onehot = (r >= AUG_SEL) & (_shr(pos, SEL_BLOCK) == r - AUG_SEL)
    return jnp.where(r == 0, _shr(pos, LANES).astype(F32),
                     jnp.where(r == 1, (pos & (LANES - 1)).astype(F32), onehot.astype(F32)))


def _select_bias_t(imp, qi, off, n_blk):
    n = imp.shape[1]
    base = off + AUG_SEL
    prow = lax.broadcasted_iota(jnp.int32, (LANES, n), 0) - base
    pool_t = (prow == _shr(lax.broadcasted_iota(jnp.int32, (LANES, n), 1), SEL_BLOCK // CMP_STRIDE)).astype(F32)
    score = lax.dot_general(pool_t, imp, (((1,), (1,)), ((), ())), preferred_element_type=F32,
                            precision=HI)[base:base + MAX_SEL_BLOCKS]
    j = lax.broadcasted_iota(jnp.int32, (MAX_SEL_BLOCKS, 1), 0)
    cur = _shr(qi * Q_BLOCK + lax.broadcasted_iota(jnp.int32, (1, Q_BLOCK), 1), SEL_BLOCK)
    forced = (j == 0) | (j == cur) | (j == cur - 1)
    score = jnp.where(forced, score + FORCE_BONUS, score)
    score = jnp.where(j > cur, NEG, score)
    rank = jnp.zeros(score.shape, F32)
    for i in range(n_blk):
        ci = score[i:i + 1, :]
        ahead = (ci > score) | ((ci == score) & (j > i))
        rank = rank + ahead.astype(F32)
    chosen = (rank < float(min(N_SEL, n_blk))) & (score > 0.5 * NEG)
    bias_t = jnp.where(chosen, 0.0, NEG)
    parts = [bias_t, jnp.zeros((LANES - base - MAX_SEL_BLOCKS, Q_BLOCK), F32)]
    if base:
        parts = [jnp.zeros((base, Q_BLOCK), F32)] + parts
    return jnp.concatenate(parts, axis=0).T


def _nsa_body(nq_ref, cmp_ref, kv_ref, aug_ref, gate_ref, o_ref, *, n_blk, tk, wlen):
    qi = pl.program_id(1)
    s_len = kv_ref.shape[1]
    rows = N_REP * Q_BLOCK
    row = lax.broadcasted_iota(jnp.int32, (rows, 1), 0)
    qpos = qi * Q_BLOCK + (row & (Q_BLOCK - 1))
    qpos_q = qi * Q_BLOCK + lax.broadcasted_iota(jnp.int32, (Q_BLOCK, 1), 0)
    lane = lax.broadcasted_iota(jnp.int32, (1, LANES), 1)
    nhb = cmp_ref.shape[0]
    end_col = lax.broadcasted_iota(jnp.int32, (nhb, 1), 0) * CMP_STRIDE + (CMP_LEN - 1)
    end = lax.broadcasted_iota(jnp.int32, (1, nhb), 1) * CMP_STRIDE + (CMP_LEN - 1)

    n_kb = _shr(qi * Q_BLOCK + Q_BLOCK + tk - 1, tk)
    t_last = pl.multiple_of((n_kb - 1) * tk, tk)
    causal = jnp.where(t_last + lax.broadcasted_iota(jnp.int32, (1, tk), 1) <= qpos_q, 0.0, NEG)
    t_win = pl.multiple_of(jnp.clip((qi - WINDOW // Q_BLOCK) * Q_BLOCK, 0, s_len - wlen), Q_BLOCK)
    d_win = qpos_q - (t_win + lax.broadcasted_iota(jnp.int32, (1, wlen), 1))
    band = jnp.where((d_win >= 0) & (d_win < WINDOW), 0.0, NEG)

    def masked(s, bias):
        n = s.shape[1]
        return (s.reshape(N_REP, Q_BLOCK, n) + bias[None]).reshape(rows, n)

    for g in range(N_GROUPS):
        off = HEAD_DIM * (1 - g)
        q = jnp.concatenate([nq_ref[:, (N_REP * g + r) * LANES:(N_REP * g + r + 1) * LANES]
                             for r in range(N_REP)], axis=0).astype(F32)
        slope = _slope_rows(g)
        q_alibi = q + jnp.where(lane == off, slope * float(LANES), 0.0) + jnp.where(lane == off + 1, slope, 0.0)
        q_win = q_alibi.astype(BF16)

        def keys(lo, t0, n):
            k = kv_ref[lo + g * HEAD_DIM:lo + (g + 1) * HEAD_DIM, pl.ds(t0, n)]
            a = aug_ref[:, pl.ds(t0, n)]
            return jnp.concatenate([k, a] if g == 0 else [a, k], axis=0)

        in_g = _shr(lane, HEAD_DIM) == g
        kc = cmp_ref[:, 0:LANES].astype(F32)
        kc = jnp.where(in_g, kc, jnp.where(lane == off, _shr(end_col, LANES).astype(F32),
                                           jnp.where(lane == off + 1, (end_col & (LANES - 1)).astype(F32), 0.0)))
        p = _cmp_softmax(_dot_nt(q_win, kc.astype(BF16)), end <= qpos)
        o_cmp = _dot(p.astype(BF16), cmp_ref[:, LANES:2 * LANES])
        imp = p[0:Q_BLOCK]
        for r in range(1, N_REP):
            imp = imp + p[r * Q_BLOCK:(r + 1) * Q_BLOCK]
        sel_bias = _select_bias_t(imp, qi, off, n_blk)
        q_sel = (q_alibi + jnp.concatenate([sel_bias] * N_REP, axis=0)).astype(BF16)

        def sel_tile(t0, carry, last):
            m_old, l_old, acc = carry
            s = _dot(q_sel, keys(0, t0, tk))
            if last:
                s = masked(s, causal)
            v = kv_ref[LANES:2 * LANES, pl.ds(t0, tk)]
            m_new = jnp.maximum(m_old, jnp.max(s, axis=-1, keepdims=True))
            alpha = jnp.exp(m_old - m_new)
            p = jnp.exp(s - m_new)
            l_new = alpha * l_old + jnp.sum(p, axis=-1, keepdims=True)
            return m_new, l_new, alpha * acc + _dot_nt(p.astype(BF16), v)

        carry = (jnp.full((rows, 1), NEG, F32), jnp.zeros((rows, 1), F32), jnp.zeros((rows, LANES), F32))
        carry = lax.fori_loop(0, n_kb - 1, lambda kb, c: sel_tile(pl.multiple_of(kb * tk, tk), c, False), carry)
        _, l_sel, acc_sel = sel_tile(t_last, carry, True)
        o_sel = acc_sel / l_sel

        s = masked(_dot(q_win, keys(2 * LANES, t_win, wlen)), band)
        p = jnp.exp(s - jnp.max(s, axis=-1, keepdims=True))
        v = kv_ref[3 * LANES:4 * LANES, pl.ds(t_win, wlen)]
        o_win = _dot_nt(p.astype(BF16), v) / jnp.sum(p, axis=-1, keepdims=True)

        for r in range(N_REP):
            hh = N_REP * g + r
            rs = slice(r * Q_BLOCK, (r + 1) * Q_BLOCK)
            o = (gate_ref[:, 8 + hh:9 + hh] * o_cmp[rs] + gate_ref[:, 16 + hh:17 + hh] * o_sel[rs]
                 + gate_ref[:, 24 + hh:25 + hh] * o_win[rs])
            o_ref[:, hh * LANES:(hh + 1) * LANES] = jnp.where(in_g, o, 0.0).astype(o_ref.dtype)


def _nsa_aug_body(o_ref):
    o_ref[...] = _aug_rows(lax.broadcasted_iota(jnp.int32, (1, o_ref.shape[1]), 1)).astype(o_ref.dtype)


def _nsa_aug(s):
    return pl.pallas_call(
        _nsa_aug_body,
        out_shape=jax.ShapeDtypeStruct((HEAD_DIM, s), BF16),
        name="nsa_aug",
    )()


def _nsa_prompt(nq, cmpkv, nsatb, aug, gates):
    b, s, _ = nq.shape
    nhb = cmpkv.shape[1]
    n_blk = s // SEL_BLOCK
    assert n_blk <= MAX_SEL_BLOCKS and s // LANES <= 256
    return pl.pallas_call(
        functools.partial(_nsa_body, n_blk=n_blk, tk=min(ATT_TK, s), wlen=min(WINDOW + Q_BLOCK, s)),
        grid=(b, s // Q_BLOCK),
        in_specs=[pl.BlockSpec((None, Q_BLOCK, 1024), lambda bi, qi: (bi, qi, 0)),
                  pl.BlockSpec((None, nhb, 256), lambda bi, qi: (bi, 0, 0)),
                  pl.BlockSpec((None, 512, s), lambda bi, qi: (bi, 0, 0)),
                  pl.BlockSpec((HEAD_DIM, s), lambda bi, qi: (0, 0)),
                  pl.BlockSpec((None, Q_BLOCK, LANES), lambda bi, qi: (bi, qi, 0))],
        out_specs=pl.BlockSpec((None, Q_BLOCK, 1024), lambda bi, qi: (bi, qi, 0)),
        out_shape=jax.ShapeDtypeStruct((b, s, 1024), BF16),
        compiler_params=_cparams("parallel", "arbitrary"),
        name="nsa_prompt",
    )(nq, cmpkv, nsatb, aug, gates)


def _merge_body(x_ref, g_ref, of_ref, og_ref, on_ref, wg_ref, wb0_ref, wb1_ref, wb2_ref, wo_ref, o_ref):
    x = x_ref[...]
    d = x.shape[1]
    h = _rms_rows(x, g_ref[...]).astype(BF16)
    y = jax.nn.sigmoid(_dot(h, wg_ref[:, 0:d])) * _dot(of_ref[...], wb0_ref[...])
    y = y + jax.nn.sigmoid(_dot(h, wg_ref[:, d:2 * d])) * _dot(og_ref[...], wb1_ref[...])
    y = y + jax.nn.sigmoid(_dot(h, wg_ref[:, 2 * d:3 * d])) * _dot(on_ref[...], wb2_ref[...])
    o_ref[...] = x + _dot(y.astype(BF16), wo_ref[...])


def _merge(x, g, o_fox, o_gm, o_nsa, wg, wb0, wb1, wb2, wo):
    t, d = x.shape
    tm = _token_tile(t)
    row = lambda n: pl.BlockSpec((tm, n), lambda i: (i, 0))
    consts = [wg, wb0, wb1, wb2, wo]
    return pl.pallas_call(
        _merge_body,
        grid=(t // tm,),
        in_specs=[row(d), _const_spec((1, d)), row(512), row(512), row(1024)] + [_const_spec(a.shape) for a in consts],
        out_specs=row(d),
        out_shape=jax.ShapeDtypeStruct((t, d), F32),
        compiler_params=_cparams("parallel"),
        name="merge",
    )(x, g, o_fox, o_gm, o_nsa, *consts)


def _column(x, idx):
    lane = lax.broadcasted_iota(jnp.int32, x.shape, 1)
    return jnp.sum(jnp.where(lane == idx, x, 0.0), axis=1, keepdims=True)


def _logf_suffix_body(x_ref, o_ref):
    n, h, t = x_ref.shape
    tri = (lax.broadcasted_iota(jnp.int32, (t, t), 0) >= lax.broadcasted_iota(jnp.int32, (t, t), 1)).astype(F32)
    o_ref[...] = _dot(x_ref[...].reshape(n * h, t), tri, precision=HI).reshape(n, h, t)


def _logf_suffix(cache_lf, layer):
    n_pool = cache_lf.shape[1]
    blk = next(c for c in (64, 32, 16, 8, 4, 2, 1) if n_pool % c == 0)
    return pl.pallas_call(
        _logf_suffix_body,
        grid=(n_pool // blk,),
        in_specs=[pl.BlockSpec((None, blk, N_HEADS, PAGE), lambda i: (layer, i, 0, 0))],
        out_specs=pl.BlockSpec((blk, N_HEADS, PAGE), lambda i: (i, 0, 0)),
        out_shape=jax.ShapeDtypeStruct((n_pool, N_HEADS, PAGE), F32),
        compiler_params=_cparams("parallel"),
        name="logf_suffix",
    )(cache_lf)


def _fox_decode_body(pt_ref, *refs, n_pages):
    kv = refs[:n_pages]
    q_ref, new_ref, lfn_ref, sfx_ref, o_ref, s_ref = refs[n_pages:]
    b = pl.program_id(0)
    w = N_HEADS * HEAD_DIM
    row = lax.broadcasted_iota(jnp.int32, (N_HEADS, w), 0)
    lane = lax.broadcasted_iota(jnp.int32, (N_HEADS, w), 1)
    diag = _shr(lane, HEAD_DIM) == row
    qbd = jnp.where(diag, jnp.broadcast_to(q_ref[...].astype(F32), (N_HEADS, w)), 0.0).astype(BF16)
    last = lax.broadcasted_iota(jnp.int32, (N_HEADS, PAGE), 1) == PAGE - 1

    later = _column(lfn_ref[...], b)
    for i in reversed(range(n_pages)):
        incl = sfx_ref[pt_ref[b, i]]
        after = jnp.where(last, 0.0, pltpu.roll(incl, PAGE - 1, 1))
        s_ref[:, i * PAGE:(i + 1) * PAGE] = _dot(qbd, kv[i][0].astype(BF16)) + after + later
        later = later + incl[:, 0:1]

    k_new = new_ref[:, 0:w].astype(BF16).astype(F32)
    v_new = new_ref[:, w:2 * w].astype(BF16).astype(F32)
    s_new = jnp.sum(qbd.astype(F32) * k_new, axis=-1, keepdims=True)
    m = jnp.maximum(jnp.max(s_ref[...], axis=-1, keepdims=True), s_new)
    p_new = jnp.exp(s_new - m)
    l = p_new
    o = p_new.astype(BF16).astype(F32) * v_new
    for i in range(n_pages):
        p = jnp.exp(s_ref[:, i * PAGE:(i + 1) * PAGE] - m)
        l = l + jnp.sum(p, axis=-1, keepdims=True)
        o = o + _dot_nt(p.astype(BF16), kv[i][1].astype(BF16))
    o_ref[...] = jnp.sum(jnp.where(diag, o / l, 0.0), axis=0, keepdims=True).astype(o_ref.dtype)


def _fox_decode(page_table, cache_kv, lf_suffix, layer, fq_s, foxkv_s, lft_s):
    db, n_pages = page_table.shape
    w = N_HEADS * HEAD_DIM
    grid_spec = pltpu.PrefetchScalarGridSpec(
        num_scalar_prefetch=1,
        grid=(db,),
        in_specs=[pl.BlockSpec((None, None, 2, w, PAGE), lambda b, pt, i=i: (layer, pt[b, i], 0, 0, 0))
                  for i in range(n_pages)]
                 + [pl.BlockSpec((None, 1, w), lambda b, pt: (b, 0, 0)),
                    pl.BlockSpec((None, 1, 2 * w), lambda b, pt: (b, 0, 0)),
                    pl.BlockSpec(lft_s.shape, lambda b, pt: (0, 0)),
                    pl.BlockSpec(lf_suffix.shape, lambda b, pt: (0, 0, 0), pipeline_mode=pl.Buffered(1))],
        out_specs=pl.BlockSpec((None, 1, w), lambda b, pt: (b, 0, 0)),
        scratch_shapes=[pltpu.VMEM((N_HEADS, n_pages * PAGE), F32)],
    )
    return pl.pallas_call(
        functools.partial(_fox_decode_body, n_pages=n_pages),
        grid_spec=grid_spec,
        out_shape=jax.ShapeDtypeStruct((db, 1, w), BF16),
        compiler_params=_cparams("arbitrary"),
        name="fox_decode",
    )(page_table, *([cache_kv] * n_pages), fq_s, foxkv_s, lft_s, lf_suffix)


def _nsa_decode_body(pt_ref, *refs, n_pages, rps, n_prev):
    pg = [refs[r * n_pages:(r + 1) * n_pages] for r in range(rps)]
    (win_ref, q_ref, new_ref, gate_ref, wnew_ref, wa_ref, wb_ref, pea_ref, peb_ref, w2_ref, kg0_ref,
     pet_ref) = refs[rps * n_pages:rps * n_pages + 12]
    o_ref, wout_ref, xa_ref, xb_ref, s_ref = refs[rps * n_pages + 12 + n_prev:]
    past = n_pages * PAGE
    nhb = past // CMP_STRIDE
    wbuf = win_ref.shape[3]
    cur = past // SEL_BLOCK
    n_blk = cur + 1
    m = rps * N_HEADS

    def per_row(fn):
        return jnp.concatenate([fn(r) for r in range(rps)], axis=0)

    tp = lax.broadcasted_iota(jnp.int32, (2 * PAGE, 2 * PAGE), 0)
    r16 = tp & (CMP_STRIDE - 1)
    src = _shr(r16, 8) * PAGE + (r16 & 7) * CMP_STRIDE + _shr(tp, CMP_STRIDE)
    perm = (lax.broadcasted_iota(jnp.int32, (2 * PAGE, 2 * PAGE), 1) == src).astype(BF16)
    for r in range(rps):
        for j in range(n_pages // 2):
            pkv = jnp.concatenate([jnp.concatenate([pg[r][2 * j + pp][0], pg[r][2 * j + pp][1]], axis=0)
                                   for pp in range(2)], axis=1)
            for half, x_ref in enumerate((xa_ref, xb_ref)):
                rows = _dot_nt(perm, (pkv + pet_ref[half]).astype(BF16)).astype(BF16)
                for l in range(CMP_STRIDE):
                    x_ref[r * nhb + 16 * j:r * nhb + 16 * (j + 1), 256 * l:256 * (l + 1)] = rows[16 * l:16 * (l + 1), :]
    kc, vc = _compress_finish(xa_ref[...], xb_ref[...], wa_ref, wb_ref, w2_ref, kg0_ref)
    kc = kc.astype(BF16)
    vc = vc.astype(BF16)

    qf = q_ref[...].astype(F32)
    q = [qf[r * N_HEADS:(r + 1) * N_HEADS].astype(BF16) for r in range(rps)]
    row = lax.broadcasted_iota(jnp.int32, (m, 1), 0)
    head = row & (N_HEADS - 1)
    slope = _slope_col(head)
    lane = lax.broadcasted_iota(jnp.int32, (1, LANES), 1)

    def rows_of(x, r):
        return x[r * N_HEADS:(r + 1) * N_HEADS]

    def new_rows(lo):
        return per_row(lambda r: jnp.broadcast_to(new_ref[r][:, lo:lo + LANES].astype(F32), (N_HEADS, LANES)))

    end = lax.broadcasted_iota(jnp.int32, (1, nhb), 1) * CMP_STRIDE + (CMP_LEN - 1)
    s = per_row(lambda r: _dot_nt(q[r], kc[r * nhb:(r + 1) * nhb])) - slope * (past - end).astype(F32)
    p = _cmp_softmax(s, jnp.broadcast_to(end <= past, s.shape))
    o_cmp = per_row(lambda r: _dot(rows_of(p, r).astype(BF16), vc[r * nhb:(r + 1) * nhb]))
    same = (_shr(lax.broadcasted_iota(jnp.int32, (m, m), 0), N_REP)
            == _shr(lax.broadcasted_iota(jnp.int32, (m, m), 1), N_REP)).astype(F32)
    imp = _dot(same, p, precision=HI)
    sel = _select_blocks(imp, jnp.full((m, 1), cur, jnp.int32), n_blk)

    for r in range(rps):
        for i in range(n_pages):
            s_ref[r * N_HEADS:(r + 1) * N_HEADS, i * PAGE:(i + 1) * PAGE] = _dot(q[r], pg[r][i][2].astype(BF16))
    kpos = lax.broadcasted_iota(jnp.int32, (1, past), 1)
    s = s_ref[...] - slope * (past - kpos).astype(F32)
    s = jnp.where(_expand_sel(sel.astype(BF16), 0, past) > 0.5, s, NEG)
    new_ok = jnp.sum(jnp.where(lane == cur, sel.astype(F32), 0.0), axis=-1, keepdims=True) > 0.5
    s_new = jnp.where(new_ok, jnp.sum(qf * new_rows(0), axis=-1, keepdims=True), NEG)
    mx = jnp.maximum(jnp.max(s, axis=-1, keepdims=True), s_new)
    p_new = jnp.where(new_ok, jnp.exp(s_new - mx), 0.0)
    p = jnp.where(s > 0.5 * NEG, jnp.exp(s - mx), 0.0)
    l = jnp.sum(p, axis=-1, keepdims=True) + p_new
    s_ref[...] = p

    def sel_pv(r):
        acc = jnp.zeros((N_HEADS, LANES), F32)
        for i in range(n_pages):
            pr = s_ref[r * N_HEADS:(r + 1) * N_HEADS, i * PAGE:(i + 1) * PAGE]
            acc = acc + _dot_nt(pr.astype(BF16), pg[r][i][3].astype(BF16))
        return acc

    o_sel = (per_row(sel_pv) + p_new.astype(BF16).astype(F32) * new_rows(LANES)) / jnp.where(l > 0.0, l, 1.0)

    pos = lax.broadcasted_iota(jnp.int32, (1, wbuf), 1)
    d = wbuf - pos
    s = per_row(lambda r: _dot(q[r], win_ref[r, 0].astype(BF16))) - slope * d.astype(F32)
    s = jnp.where(d < WINDOW, s, NEG)
    s_new = jnp.sum(qf * new_rows(2 * LANES), axis=-1, keepdims=True)
    mx = jnp.maximum(jnp.max(s, axis=-1, keepdims=True), s_new)
    p = jnp.exp(s - mx)
    p_new = jnp.exp(s_new - mx)
    l = jnp.sum(p, axis=-1, keepdims=True) + p_new
    o_win = (per_row(lambda r: _dot_nt(rows_of(p, r).astype(BF16), win_ref[r, 1].astype(BF16)))
             + p_new.astype(BF16).astype(F32) * new_rows(3 * LANES)) / l

    o = gate_ref[:, 0:1] * o_cmp + gate_ref[:, 1:2] * o_sel + gate_ref[:, 2:3] * o_win
    in_group = _shr(lane, HEAD_DIM) == _shr(head, N_REP)
    o_ref[...] = jnp.where(in_group, o, 0.0).astype(o_ref.dtype)

    for r in range(rps):
        bidx = pl.program_id(0) * rps + r
        for kv in range(2):
            wout_ref[r, kv] = jnp.where(pos == wbuf - 1, _column(wnew_ref[kv * LANES:(kv + 1) * LANES, :], bidx),
                                        pltpu.roll(win_ref[r, kv], wbuf - 1, 1))


def _nsa_decode(page_table, cache_nsa, win_state, layer, nq_s, nsab_s, gates_s, wint_s, cw, prev_win=None):
    db, n_pages = page_table.shape
    depth, wbuf = win_state.shape[0], win_state.shape[4]
    prev = () if prev_win is None else (prev_win,)
    rps = DECODE_ROWS
    assert db % rps == 0 and n_pages % 2 == 0
    m = rps * N_HEADS
    nhb = n_pages * PAGE // CMP_STRIDE
    in_specs = [pl.BlockSpec((None, None, 4, LANES, PAGE), lambda b, pt, i=i, r=r: (layer, pt[b * rps + r, i], 0, 0, 0))
                for r in range(rps) for i in range(n_pages)]
    in_specs += [pl.BlockSpec((None, rps, 2, LANES, wbuf), lambda b, pt: (layer, b, 0, 0, 0)),
                 pl.BlockSpec((m, LANES), lambda b, pt: (b, 0)),
                 pl.BlockSpec((rps, 1, 512), lambda b, pt: (b, 0, 0)),
                 pl.BlockSpec((m, 3), lambda b, pt: (b, 0)),
                 pl.BlockSpec(wint_s.shape, lambda b, pt: (0, 0))]
    in_specs += [pl.BlockSpec(a.shape, lambda b, pt, nd=a.ndim: (0,) * nd, pipeline_mode=pl.Buffered(1)) for a in cw]
    in_specs += [pl.BlockSpec(memory_space=pl.ANY)] * len(prev)
    grid_spec = pltpu.PrefetchScalarGridSpec(
        num_scalar_prefetch=1,
        grid=(db // rps,),
        in_specs=in_specs,
        out_specs=[pl.BlockSpec((m, LANES), lambda b, pt: (b, 0)),
                   pl.BlockSpec((None, rps, 2, LANES, wbuf), lambda b, pt: (layer, b, 0, 0, 0))],
        scratch_shapes=[pltpu.VMEM((rps * nhb, CMP_STRIDE * 256), BF16),
                        pltpu.VMEM((rps * nhb, CMP_STRIDE * 256), BF16),
                        pltpu.VMEM((m, n_pages * PAGE), F32)],
    )
    return pl.pallas_call(
        functools.partial(_nsa_decode_body, n_pages=n_pages, rps=rps, n_prev=len(prev)),
        grid_spec=grid_spec,
        out_shape=[jax.ShapeDtypeStruct((db * N_HEADS, LANES), BF16),
                   jax.ShapeDtypeStruct((depth, db, 2, LANES, wbuf), F32)],
        input_output_aliases={1 + rps * n_pages + 5 + len(cw): 1} if prev else {},
        compiler_params=_cparams("arbitrary"),
        name="nsa_decode",
    )(page_table, *([cache_nsa] * (rps * n_pages)), win_state, nq_s, nsab_s, gates_s, wint_s, *cw, *prev)


O_F, O_GM, O_NQ, O_NKV, O_NG, O_MG = 1536, 1544, 2568, 3080, 3848, 3872
_HEAD_PLACE = (np.arange(N_HEADS)[:, None] // N_REP == np.arange(N_GROUPS)[None, :]).astype(np.float32)


def _prep_w_in(w):
    d = w.shape[0]
    wq = w[:, O_NQ:O_NKV].reshape(d, N_HEADS, 1, HEAD_DIM)
    wq = (wq * _HEAD_PLACE[None, :, :, None]).reshape(d, N_HEADS * LANES)
    small = jnp.concatenate([w[:, O_F:O_GM], w[:, O_NG:O_MG], jnp.zeros((d, LANES - 32), w.dtype)], axis=1)
    w_tok = jnp.concatenate([w[:, 0:O_F], w[:, O_GM:O_NQ], wq, w[:, O_NKV:O_NG], small], axis=1).astype(BF16)
    wt = w.T
    w_feat = jnp.concatenate([wt[512:O_F], wt[O_NKV:O_NG], wt[O_F:O_GM], jnp.zeros((8, d), w.dtype)],
                             axis=0).astype(BF16)
    return w_tok, w_feat, w[:, O_MG:].astype(BF16)


def _pad_heads(g):
    return (_HEAD_PLACE[:, :, None] * g[None, None, :]).reshape(1, N_HEADS * LANES)


def _prep_compress(w1, w2, pe, kg0):
    w1r = w1.reshape(2, 2, CMP_STRIDE, HEAD_DIM, HEAD_DIM).astype(BF16)

    def block_diag(blocks):
        z = jnp.zeros_like(blocks[0])
        return jnp.concatenate([jnp.concatenate([blk if i == j else z for j in range(4)], axis=-1)
                                for i, blk in enumerate(blocks)], axis=-2)

    wl = block_diag([w1r[c] for c in range(2) for _ in range(N_GROUPS)]).reshape(2, CMP_STRIDE * 256, 256)
    w2bd = block_diag([w2[c] for c in range(2) for _ in range(N_GROUPS)])
    pex = jnp.broadcast_to(pe.transpose(1, 0, 2)[:, :, None, :], (CMP_LEN, 2, N_GROUPS, HEAD_DIM))
    pex = pex.reshape(2, CMP_STRIDE, 1, 256)
    pet = jnp.tile(pex.reshape(2, CMP_STRIDE, 256).transpose(0, 2, 1), (1, 1, 2 * PAGE // CMP_STRIDE))
    return (wl[0], wl[1], pex[0], pex[1], w2bd.astype(BF16), jnp.tile(kg0, 2)[None, :], pet)


def _layer(xp, xs, lw, page_table, caches, layer, depth, prev, dims):
    b, s, db = dims

    xp = _ffn(xp, lw['ffn1_norm'], lw['ffn1_w_gu'], lw['ffn1_w_down'])
    xs = _ffn(xs, lw['ffn1_norm'], lw['ffn1_w_gu'], lw['ffn1_w_down'])

    pw = lw['proj']
    (fq, foxt, foxtb, gmu, gmv, nq, nsat, wint, nsatb, cmpraw, lft, gates) = _proj(
        xp, pw, b, s, False, layer, depth, None if prev is None else prev[:4])
    (fq_s, foxt_s, _, gmu_s, gmv_s, nq_s, nsat_s, wint_s, _, _, lft_s, gates_s,
     foxkv_s, nsab_s) = _proj(xs, pw, 1, db, True)

    c = _cumsum(lft, layer).reshape(b, 4, 2, s)
    o_fox_p = _fox_prompt(fq.reshape(b, s, 512), foxtb, c)
    o_gm_p = _gmlp_prompt(gmu, gmv, lw['gmlp_w_s'], lw['gmlp_bexp'])
    cmpkv = _compress_prompt(cmpraw.reshape(b, s, 256), lw['cw'])
    o_nsa_p = _nsa_prompt(nq.reshape(b, s, 1024), cmpkv, nsatb, lw['nsa_aug'], gates.reshape(b, s, LANES))
    xp = _merge(xp, lw['mix_norm'], o_fox_p.reshape(b * s, 512), o_gm_p, o_nsa_p.reshape(b * s, 1024),
                lw['w_gate'], lw['wb0'], lw['wb1'], lw['wb2'], lw['w_out'])

    cache_fox_kv, cache_fox_lf, cache_nsa, win_state = caches
    o_fox_s = _fox_decode(page_table, cache_fox_kv, _logf_suffix(cache_fox_lf, layer), layer, fq_s.reshape(db, 1, 512),
                          foxkv_s.reshape(db, 1, 1024), lft_s.reshape(N_HEADS, db))
    o_gm_s = _gmlp_sample(gmu_s, gmv_s, lw['gmlp_w00'], lw['gmlp_b0'])
    g3 = gates_s[:, 8:32].reshape(db, 3, N_HEADS).transpose(0, 2, 1).reshape(db * N_HEADS, 3)
    o_nsa_s, win_next = _nsa_decode(page_table, cache_nsa, win_state, layer, nq_s.reshape(db * N_HEADS, LANES),
                                    nsab_s.reshape(db, 1, 512), g3, wint_s.reshape(256, db), lw['cw'],
                                    None if prev is None else prev[4])
    xs = _merge(xs, lw['mix_norm'], o_fox_s.reshape(db, 512), o_gm_s, o_nsa_s.reshape(db, 1024),
                lw['w_gate'], lw['wb0'], lw['wb1'], lw['wb2'], lw['w_out'])

    xp = _ffn(xp, lw['ffn2_norm'], lw['ffn2_w_gu'], lw['ffn2_w_down'])
    xs = _ffn(xs, lw['ffn2_norm'], lw['ffn2_w_gu'], lw['ffn2_w_down'])

    state = dict(
        fox_kv_s=foxt_s, fox_logf_s=lft_s, nsa_kv_s=nsat_s,
        gmlp_v_p=gmv.reshape(b, s, 512)[:, s - min(CHUNK, s):], gmlp_v_s=gmv_s.reshape(db, 1, 512),
    )
    return xp, xs, state, (foxt, nsat, wint, lft, win_next)


def kernel(x_prompt, x_sample, page_table, cache_fox_kv, cache_fox_logf, cache_nsa_kv, state_nsa_win, ffn1_norm, ffn1_w_gu, ffn1_w_down, mix_norm, w_in, fox_b_f, fox_qk_gain, gmlp_v_gain, gmlp_w_s, gmlp_b_s, nsa_q_gain, nsa_k_gain, nsa_cmp_pe, nsa_cmp_w1, nsa_cmp_w2, w_branch, w_out, ffn2_norm, ffn2_w_gu, ffn2_w_down):
    b, s, d = x_prompt.shape
    db = x_sample.shape[0]
    depth = w_in.shape[0]
    n_pool = cache_fox_kv.shape[1]
    wb = state_nsa_win.shape[2]
    xp = x_prompt.reshape(b * s, d)
    xs = x_sample.reshape(db, d)

    caches = (cache_fox_kv.transpose(0, 1, 3, 4, 5, 2).reshape(depth, n_pool, 2, N_HEADS * HEAD_DIM, PAGE),
              cache_fox_logf.transpose(0, 1, 3, 2),
              cache_nsa_kv.transpose(0, 1, 3, 4, 5, 2).reshape(depth, n_pool, 4, LANES, PAGE),
              state_nsa_win.transpose(0, 1, 3, 4, 5, 2).reshape(depth, db, 2, LANES, wb))

    nsa_rows = (np.arange(N_HEADS * LANES) % LANES) // HEAD_DIM == (np.arange(N_HEADS * LANES) // LANES) // N_REP
    col = lambda g: g[:, None]
    nsa_aug = _nsa_aug(s)
    states = []
    stacked = None
    for l in range(depth):
        w_tok, w_feat, w_gate = _prep_w_in(w_in[l])
        wb2 = jnp.zeros((N_HEADS * LANES, d), F32).at[np.nonzero(nsa_rows)[0]].set(w_branch[l, 2])
        proj = (mix_norm[l][None, :], w_tok, w_feat,
                jnp.tile(fox_qk_gain[l, 0], 2)[None, :], jnp.tile(fox_qk_gain[l, 1], 2)[None, :],
                col(fox_qk_gain[l, 1]), gmlp_v_gain[l][None, :], _pad_heads(nsa_q_gain[l]),
                jnp.tile(nsa_k_gain[l, 1], 2)[None, :], jnp.tile(nsa_k_gain[l, 2], 2)[None, :],
                col(nsa_k_gain[l, 1]), col(nsa_k_gain[l, 2]), col(fox_b_f[l]))
        lw = dict(
            ffn1_norm=ffn1_norm[l][None, :], ffn1_w_gu=ffn1_w_gu[l].astype(BF16), ffn1_w_down=ffn1_w_down[l].astype(BF16),
            ffn2_norm=ffn2_norm[l][None, :], ffn2_w_gu=ffn2_w_gu[l].astype(BF16), ffn2_w_down=ffn2_w_down[l].astype(BF16),
            mix_norm=mix_norm[l][None, :], proj=proj, w_gate=w_gate,
            gmlp_w_s=gmlp_w_s[l], gmlp_bexp=jnp.repeat(gmlp_b_s[l].T, HEAD_DIM, axis=1),
            gmlp_w00=jnp.repeat(gmlp_w_s[l, :, 0, 0], HEAD_DIM)[None, :],
            gmlp_b0=jnp.repeat(gmlp_b_s[l, :, 0], HEAD_DIM)[None, :],
            cw=_prep_compress(nsa_cmp_w1[l], nsa_cmp_w2[l], nsa_cmp_pe[l], nsa_k_gain[l, 0]),
            wb0=w_branch[l, 0].astype(BF16), wb1=w_branch[l, 1].astype(BF16), wb2=wb2.astype(BF16),
            w_out=w_out[l].astype(BF16), nsa_aug=nsa_aug,
        )
        xp, xs, st, stacked = _layer(xp, xs, lw, page_table, caches, l, depth, stacked, (b, s, db))
        states.append(st)

    def stack(name):
        return jnp.stack([st[name] for st in states])

    def tok_major(a, *feat_dims):
        return jnp.moveaxis(a.reshape(a.shape[:2] + feat_dims + a.shape[3:]), -1, 2)

    def tok_major_s(a, *feat_dims):
        return jnp.swapaxes(tok_major(a, *feat_dims), 1, 2)

    foxt, nsat, wint, lft, win_next = stacked
    return (xp.reshape(b, s, d), xs.reshape(db, 1, d),
            tok_major(foxt, 2, N_HEADS, HEAD_DIM), tok_major_s(stack('fox_kv_s'), 2, N_HEADS, HEAD_DIM),
            tok_major(lft, N_HEADS), tok_major_s(stack('fox_logf_s'), N_HEADS),
            tok_major(nsat, 4, N_GROUPS, HEAD_DIM), tok_major_s(stack('nsa_kv_s'), 4, N_GROUPS, HEAD_DIM),
            tok_major(wint[:, :, :, s - min(WINDOW, s):], 2, N_GROUPS, HEAD_DIM),
            tok_major(win_next.reshape(depth, db, 2 * LANES, wb), 2, N_GROUPS, HEAD_DIM),
            stack('gmlp_v_p'), stack('gmlp_v_s'))
```

```python
import functools

import jax
import jax.numpy as jnp
import numpy as np
from jax import lax
from jax.experimental import pallas as pl
from jax.experimental.pallas import tpu as pltpu

F32 = jnp.float32
BF16 = jnp.bfloat16

HEAD_DIM = 64
LANES = 128
PAGE = 128
CHUNK = 128
GMLP_CHUNKS_PER_STEP = 4
DECODE_ROWS = 2
Q_BLOCK = 256
FOX_TQ = 512
ATT_TK = 512
CMP_LEN = 32
CMP_STRIDE = 16
SEL_BLOCK = 64
N_SEL = 8
WINDOW = 512
N_GROUPS = 2
N_REP = 4
N_HEADS = 8
FORCE_BONUS = 1.0e4
NEG = -1.0e30
EPS = 1e-6
SCALE = HEAD_DIM ** -0.5
VMEM_LIMIT = 56 * 1024 * 1024
HI = lax.Precision.HIGHEST


def _cparams(*sem):
    return pltpu.CompilerParams(dimension_semantics=sem, vmem_limit_bytes=VMEM_LIMIT)


def _dot(a, b, precision=None):
    return jnp.dot(a, b, preferred_element_type=F32, precision=precision)


def _dot_nt(a, b):
    return lax.dot_general(a, b, (((1,), (1,)), ((), ())), preferred_element_type=F32)


def _shr(x, pow2):
    return jnp.right_shift(x, int(pow2).bit_length() - 1)


def _rms_rows(x, g):
    ms = jnp.mean(x * x, axis=-1, keepdims=True)
    return x * lax.rsqrt(ms + EPS) * g


def _headnorm_pair(zb, gain):
    lo = lax.broadcasted_iota(jnp.int32, zb.shape, 1) < HEAD_DIM
    sq = zb * zb
    s_lo = jnp.sum(jnp.where(lo, sq, 0.0), axis=-1, keepdims=True)
    s_hi = jnp.sum(jnp.where(lo, 0.0, sq), axis=-1, keepdims=True)
    ms = jnp.where(lo, s_lo, s_hi) * (1.0 / HEAD_DIM)
    return zb * lax.rsqrt(ms + EPS) * gain


def _headnorm_cols(zt, gain_col):
    ms = jnp.mean(zt * zt, axis=0, keepdims=True)
    return zt * lax.rsqrt(ms + EPS) * gain_col


def _silu(x):
    return x * jax.nn.sigmoid(x)


def _gelu_tanh(x):
    return 0.5 * x * (1.0 + jnp.tanh(np.sqrt(2.0 / np.pi) * (x + 0.044715 * (x * x * x))))


def _log_sigmoid(x):
    return jnp.minimum(x, 0.0) - jnp.log1p(jnp.exp(-jnp.abs(x)))


def _token_tile(t):
    for tm in (512, 384, 256, 128):
        if t % tm == 0:
            return tm
    raise ValueError(f"token count {t} is not a multiple of 128")


def _const_spec(shape):
    nd = len(shape)
    return pl.BlockSpec(shape, lambda *_: (0,) * nd, pipeline_mode=pl.Buffered(1))


def _slab_spec(a, idx):
    nd = a.ndim - 1
    return pl.BlockSpec((None,) + a.shape[1:], lambda *_: (idx,) + (0,) * nd, pipeline_mode=pl.Buffered(1))


def _ffn_body(x_ref, g_ref, wgu_ref, wd_ref, o_ref, *, d_ff, fc):
    x = x_ref[...]
    h = _rms_rows(x, g_ref[...]).astype(BF16)
    acc = jnp.zeros_like(x)
    for c in range(d_ff // fc):
        g = _dot(h, wgu_ref[:, c * fc:(c + 1) * fc])
        u = _dot(h, wgu_ref[:, d_ff + c * fc:d_ff + (c + 1) * fc])
        a = (_silu(g) * u).astype(BF16)
        acc = acc + _dot(a, wd_ref[c * fc:(c + 1) * fc, :])
    o_ref[...] = x + 0.5 * acc


def _ffn(x, g, wgu, wd, layer):
    t, d = x.shape
    d_ff = wd.shape[1]
    tm = _token_tile(t)
    fc = 256 if d_ff % 256 == 0 else 128
    return pl.pallas_call(
        functools.partial(_ffn_body, d_ff=d_ff, fc=fc),
        grid=(t // tm,),
        in_specs=[pl.BlockSpec((tm, d), lambda i: (i, 0)),
                  _slab_spec(g, layer), _slab_spec(wgu, layer), _slab_spec(wd, layer)],
        out_specs=pl.BlockSpec((tm, d), lambda i: (i, 0)),
        out_shape=jax.ShapeDtypeStruct((t, d), F32),
        compiler_params=_cparams("parallel"),
        name="ffn",
    )(x, g, wgu, wd)


C_FOX = 0
C_GM = 1536
C_NQ = 2560
C_NKV = 3584
C_SMALL = 4352
C_END = 4480
R_NKV = 1024
R_F = 1792
R_END = 1808


def _proj_body(x_ref, g_ref, w_ref, wt_ref, gq_ref, gk_ref, gkc_ref, gv_ref, gnq_ref, gk1_ref, gk2_ref,
               gk1c_ref, gk2c_ref, bfc_ref, *rest, sample, n_prev):
    out_refs = rest[n_prev:]
    (fq_ref, foxt_ref, foxtb_ref, gmu_ref, gmv_ref, nq_ref, nsat_ref, wint_ref, nsatb_ref, cmpraw_ref,
     lft_ref, gate_ref) = out_refs[:12]
    h = _rms_rows(x_ref[...], g_ref[...]).astype(BF16)

    z = _dot(h, w_ref[:, 0:512])
    for j in range(4):
        sl = slice(j * LANES, (j + 1) * LANES)
        fq_ref[:, sl] = (_headnorm_pair(z[:, sl], gq_ref[...]) * SCALE).astype(BF16)
    zt = _dot_nt(wt_ref[0:R_NKV, :], h)
    for hh in range(N_HEADS):
        rs = slice(hh * HEAD_DIM, (hh + 1) * HEAD_DIM)
        kt = _headnorm_cols(zt[rs], gkc_ref[...])
        foxt_ref[rs, :] = kt
        foxtb_ref[rs, :] = kt.astype(BF16)
    foxt_ref[512:1024, :] = zt[512:1024]
    foxtb_ref[512:1024, :] = zt[512:1024].astype(BF16)

    a = _gelu_tanh(_dot(h, w_ref[:, C_GM:C_NQ]))
    gmu_ref[...] = a[:, 0:512]
    gmv_ref[...] = _rms_rows(a[:, 512:1024], gv_ref[...])

    z = _dot(h, w_ref[:, C_NQ:C_NKV])
    for hh in range(N_HEADS):
        sl = slice(hh * LANES, (hh + 1) * LANES)
        zb = z[:, sl]
        ms = jnp.sum(zb * zb, axis=-1, keepdims=True) * (1.0 / HEAD_DIM)
        nq_ref[:, sl] = (zb * lax.rsqrt(ms + EPS) * (gnq_ref[:, sl] * SCALE)).astype(BF16)

    zt = _dot_nt(wt_ref[R_NKV:R_F, :], h)
    nsat_ref[0:256, :] = zt[0:256]
    nsat_ref[384:512, :] = zt[384:512]
    nsatb_ref[128:256, :] = zt[384:512].astype(BF16)
    wint_ref[128:256, :] = zt[640:768]
    nsatb_ref[384:512, :] = zt[640:768].astype(BF16)
    for g in range(N_GROUPS):
        sk = _headnorm_cols(zt[256 + g * HEAD_DIM:256 + (g + 1) * HEAD_DIM], gk1c_ref[...])
        nsat_ref[256 + g * HEAD_DIM:256 + (g + 1) * HEAD_DIM, :] = sk
        nsatb_ref[g * HEAD_DIM:(g + 1) * HEAD_DIM, :] = sk.astype(BF16)
        wk = _headnorm_cols(zt[512 + g * HEAD_DIM:512 + (g + 1) * HEAD_DIM], gk2c_ref[...])
        wint_ref[g * HEAD_DIM:(g + 1) * HEAD_DIM, :] = wk
        nsatb_ref[256 + g * HEAD_DIM:256 + (g + 1) * HEAD_DIM, :] = wk.astype(BF16)

    cmpraw_ref[...] = _dot(h, w_ref[:, C_NKV:C_NKV + 256])

    zt = _dot_nt(wt_ref[R_F:R_END, :], h)
    lft_ref[...] = _log_sigmoid(zt[0:N_HEADS] + bfc_ref[...])
    gate_ref[...] = jax.nn.sigmoid(_dot(h, w_ref[:, C_SMALL:C_END]))

    if sample:
        foxs_ref, nsabs_ref = out_refs[12:]
        z = _dot(h, w_ref[:, 512:C_GM])
        for j in range(4):
            sl = slice(j * LANES, (j + 1) * LANES)
            foxs_ref[:, sl] = _headnorm_pair(z[:, sl], gk_ref[...])
        foxs_ref[:, 512:1024] = z[:, 512:1024]
        z = _dot(h, w_ref[:, C_NKV + 256:C_SMALL])
        nsabs_ref[:, 0:128] = _headnorm_pair(z[:, 0:128], gk1_ref[...]).astype(BF16)
        nsabs_ref[:, 128:256] = z[:, 128:256].astype(BF16)
        nsabs_ref[:, 256:384] = _headnorm_pair(z[:, 256:384], gk2_ref[...]).astype(BF16)
        nsabs_ref[:, 384:512] = z[:, 384:512].astype(BF16)


STATE_OUTS = (1, 6, 7, 10)


def _proj(x, pw, b, s, sample, layer=0, depth=1, prev=None):
    t, d = x.shape
    tm = _token_tile(s)
    nst = s // tm
    tok = lambda n, dt: (pl.BlockSpec((tm, n), lambda i: (i, 0)), jax.ShapeDtypeStruct((t, n), dt))
    feat = lambda n, dt: (pl.BlockSpec((None, n, tm), lambda i: (i // nst, 0, i % nst)),
                          jax.ShapeDtypeStruct((b, n, s), dt))
    state = lambda n: (pl.BlockSpec((None, None, n, tm), lambda i: (layer, i // nst, 0, i % nst)),
                       jax.ShapeDtypeStruct((depth, b, n, s), F32))
    outs = [tok(512, BF16), feat(1024, F32), feat(1024, BF16), tok(512, F32), tok(512, F32), tok(1024, BF16),
            feat(512, F32), feat(256, F32), feat(512, BF16), tok(256, F32), feat(N_HEADS, F32), tok(LANES, F32)]
    if sample:
        outs += [tok(1024, F32), tok(512, BF16)]
    else:
        for k in STATE_OUTS:
            outs[k] = state(outs[k][1].shape[1])
    prev = () if prev is None else tuple(prev)
    return pl.pallas_call(
        functools.partial(_proj_body, sample=sample, n_prev=len(prev)),
        grid=(t // tm,),
        in_specs=[pl.BlockSpec((tm, d), lambda i: (i, 0))] + [_const_spec(a.shape) for a in pw]
                 + [pl.BlockSpec(memory_space=pl.ANY)] * len(prev),
        out_specs=[o[0] for o in outs],
        out_shape=[o[1] for o in outs],
        input_output_aliases={1 + len(pw) + k: STATE_OUTS[k] for k in range(len(prev))},
        compiler_params=_cparams("parallel"),
        name="in_proj_sample" if sample else "in_proj",
    )(x, *pw, *prev)


def _cumsum_body(x_ref, o_ref, *, blk):
    s = x_ref.shape[1]
    r = lax.broadcasted_iota(jnp.int32, (blk, blk), 0)
    c = lax.broadcasted_iota(jnp.int32, (blk, blk), 1)
    tri = (r <= c).astype(F32)
    carry = jnp.zeros((x_ref.shape[0], 1), F32)
    for i in range(s // blk):
        cs = _dot(x_ref[:, i * blk:(i + 1) * blk], tri, precision=HI) + carry
        o_ref[:, i * blk:(i + 1) * blk] = cs
        carry = cs[:, blk - 1:blk]


def _cumsum(x, layer):
    _, b, n, s = x.shape
    return pl.pallas_call(
        functools.partial(_cumsum_body, blk=min(256, s)),
        grid=(b,),
        in_specs=[pl.BlockSpec((None, None, n, s), lambda i: (layer, i, 0, 0))],
        out_specs=pl.BlockSpec((None, n, s), lambda i: (i, 0, 0)),
        out_shape=jax.ShapeDtypeStruct((b, n, s), F32),
        compiler_params=_cparams("parallel"),
        name="logf_cumsum",
    )(x)


def _fox_body(q_ref, k_ref, v_ref, c_ref, o_ref, *, tq, tk):
    qi = pl.program_id(2)
    q = q_ref[...]
    lo = lax.broadcasted_iota(jnp.int32, (tq, LANES), 1) < HEAD_DIM
    zero = jnp.zeros_like(q)
    qs = jnp.concatenate([jnp.where(lo, q, zero), jnp.where(lo, zero, q)], axis=0)
    qpos = qi * tq + lax.broadcasted_iota(jnp.int32, (tq, 1), 0)

    def block(kb, carry, masked):
        t0 = pl.multiple_of(kb * tk, tk)
        m_old, l_old, acc = carry
        bias = -c_ref[:, pl.ds(t0, tk)]
        s = _dot(qs, k_ref[:, pl.ds(t0, tk)]).reshape(2, tq, tk) + bias[:, None, :]
        if masked:
            kpos = t0 + lax.broadcasted_iota(jnp.int32, (1, tk), 1)
            s = jnp.where((kpos <= qpos)[None], s, NEG)
        s = s.reshape(2 * tq, tk)
        m_new = jnp.maximum(m_old, jnp.max(s, axis=-1, keepdims=True))
        alpha = jnp.exp(m_old - m_new)
        p = jnp.exp(s - m_new)
        l_new = alpha * l_old + jnp.sum(p, axis=-1, keepdims=True)
        return m_new, l_new, alpha * acc + _dot_nt(p.astype(BF16), v_ref[:, pl.ds(t0, tk)])

    n_kb = _shr(qi * tq + tq + tk - 1, tk)
    carry = (jnp.full((2 * tq, 1), NEG, F32), jnp.zeros((2 * tq, 1), F32), jnp.zeros((2 * tq, LANES), F32))
    carry = lax.fori_loop(0, n_kb - 1, lambda kb, c: block(kb, c, False), carry)
    _, l, acc = block(n_kb - 1, carry, True)
    o = acc / l
    o_ref[...] = jnp.where(lo, o[0:tq], o[tq:2 * tq]).astype(o_ref.dtype)


def _fox_prompt(fq, foxtb, c):
    b, s, _ = fq.shape
    tq = min(FOX_TQ, s)
    tk = min(ATT_TK, s)
    return pl.pallas_call(
        functools.partial(_fox_body, tq=tq, tk=tk),
        grid=(b, 4, s // tq),
        in_specs=[pl.BlockSpec((None, tq, LANES), lambda bi, p, qi: (bi, qi, p)),
                  pl.BlockSpec((None, LANES, s), lambda bi, p, qi: (bi, p, 0)),
                  pl.BlockSpec((None, LANES, s), lambda bi, p, qi: (bi, 4 + p, 0)),
                  pl.BlockSpec((None, None, 2, s), lambda bi, p, qi: (bi, p, 0, 0))],
        out_specs=pl.BlockSpec((None, tq, LANES), lambda bi, p, qi: (bi, qi, p)),
        out_shape=jax.ShapeDtypeStruct((b, s, 512), BF16),
        compiler_params=_cparams("parallel", "parallel", "arbitrary"),
        name="fox_prompt",
    )(fq, foxtb, foxtb, c)


def _gmlp_body(u_ref, v_ref, ws_ref, b_ref, o_ref):
    n = ws_ref.shape[1]
    r = lax.broadcasted_iota(jnp.int32, (n, n), 0)
    c = lax.broadcasted_iota(jnp.int32, (n, n), 1)
    lo = lax.broadcasted_iota(jnp.int32, (n, LANES), 1) < HEAD_DIM
    for j in range(4):
        sl = slice(j * LANES, (j + 1) * LANES)
        w0 = jnp.where(r >= c, ws_ref[2 * j], 0.0).astype(BF16)
        w1 = jnp.where(r >= c, ws_ref[2 * j + 1], 0.0).astype(BF16)
        for ch in range(u_ref.shape[0] // n):
            rs = slice(ch * n, (ch + 1) * n)
            vp = v_ref[rs, sl].astype(BF16)
            mixed = jnp.where(lo, _dot(w0, vp), _dot(w1, vp)) + b_ref[:, sl]
            o_ref[rs, sl] = (u_ref[rs, sl] * mixed).astype(o_ref.dtype)


def _gmlp_prompt(gmu, gmv, ws, bexp):
    t = gmu.shape[0]
    rows = GMLP_CHUNKS_PER_STEP * CHUNK if t % (GMLP_CHUNKS_PER_STEP * CHUNK) == 0 else CHUNK
    return pl.pallas_call(
        _gmlp_body,
        grid=(t // rows,),
        in_specs=[pl.BlockSpec((rows, 512), lambda i: (i, 0)),
                  pl.BlockSpec((rows, 512), lambda i: (i, 0)),
                  _const_spec(ws.shape), _const_spec(bexp.shape)],
        out_specs=pl.BlockSpec((rows, 512), lambda i: (i, 0)),
        out_shape=jax.ShapeDtypeStruct((t, 512), BF16),
        compiler_params=_cparams("parallel"),
        name="gmlp_prompt",
    )(gmu, gmv, ws, bexp)


def _gmlp_first_body(u_ref, v_ref, w_ref, b_ref, o_ref):
    o_ref[...] = (u_ref[...] * (v_ref[...] * w_ref[...] + b_ref[...])).astype(o_ref.dtype)


def _gmlp_sample(gmu, gmv, wrow, brow):
    n = gmu.shape[0]
    full = pl.BlockSpec((n, 512), lambda i: (0, 0))
    return pl.pallas_call(
        _gmlp_first_body,
        grid=(1,),
        in_specs=[full, full, _const_spec((1, 512)), _const_spec((1, 512))],
        out_specs=full,
        out_shape=jax.ShapeDtypeStruct((n, 512), BF16),
        compiler_params=_cparams("arbitrary"),
        name="gmlp_sample",
    )(gmu, gmv, wrow, brow)


def _compress_finish(xa, xb, wa_ref, wb_ref, w2_ref, kg0_ref):
    nhb = xa.shape[0]
    a = _dot(xa, wa_ref[...])
    b = _dot(xb, wb_ref[...])
    pre = a + pltpu.roll(b, nhb - 1, 0)
    out = _dot(_silu(pre).astype(BF16), w2_ref[...])
    kc = _headnorm_pair(out[:, 0:LANES], kg0_ref[...])
    return kc, out[:, LANES:2 * LANES]


def _compress(xk_ref, xv_ref, wa_ref, wb_ref, pea_ref, peb_ref, w2_ref, kg0_ref):
    nhb = xk_ref.shape[0] // CMP_STRIDE
    xa, xb = [], []
    for l in range(CMP_STRIDE):
        xl = jnp.concatenate([xk_ref[pl.ds(l, nhb, stride=CMP_STRIDE), :],
                              xv_ref[pl.ds(l, nhb, stride=CMP_STRIDE), :]], axis=1)
        xa.append((xl + pea_ref[l]).astype(BF16))
        xb.append((xl + peb_ref[l]).astype(BF16))
    return _compress_finish(jnp.concatenate(xa, axis=1), jnp.concatenate(xb, axis=1), wa_ref, wb_ref, w2_ref, kg0_ref)


def _compress_body(xk_ref, xv_ref, wa_ref, wb_ref, pea_ref, peb_ref, w2_ref, kg0_ref, pet_ref, o_ref):
    kc, vc = _compress(xk_ref, xv_ref, wa_ref, wb_ref, pea_ref, peb_ref, w2_ref, kg0_ref)
    o_ref[:, 0:LANES] = kc.astype(BF16)
    o_ref[:, LANES:2 * LANES] = vc.astype(BF16)


def _compress_prompt(cmpraw, cw):
    b, s, _ = cmpraw.shape
    nhb = s // CMP_STRIDE
    return pl.pallas_call(
        _compress_body,
        grid=(b,),
        in_specs=[pl.BlockSpec((None, s, LANES), lambda i: (i, 0, 0)),
                  pl.BlockSpec((None, s, LANES), lambda i: (i, 0, 1))] + [_const_spec(a.shape) for a in cw],
        out_specs=pl.BlockSpec((None, nhb, 256), lambda i: (i, 0, 0)),
        out_shape=jax.ShapeDtypeStruct((b, nhb, 256), BF16),
        compiler_params=_cparams("parallel"),
        name="nsa_compress",
    )(cmpraw, cmpraw, *cw)


def _slope_col(head):
    out = jnp.zeros(head.shape, F32)
    for hh in range(N_HEADS):
        out = jnp.where(head == hh, 2.0 ** (-(hh + 1)), out)
    return out


def _slope_rows(g):
    return jnp.concatenate([jnp.full((Q_BLOCK, 1), 2.0 ** (-(N_REP * g + r + 1)), F32) for r in range(N_REP)], axis=0)


def _cmp_softmax(s, mask):
    s = jnp.where(mask, s, NEG)
    m = jnp.max(s, axis=-1, keepdims=True)
    e = jnp.where(mask, jnp.exp(s - m), 0.0)
    l = jnp.sum(e, axis=-1, keepdims=True)
    return e / jnp.where(l > 0.0, l, 1.0)


def _select_blocks(imp, cur, n_blk):
    m, n = imp.shape
    pool = (_shr(lax.broadcasted_iota(jnp.int32, (n, LANES), 0), SEL_BLOCK // CMP_STRIDE)
            == lax.broadcasted_iota(jnp.int32, (n, LANES), 1)).astype(F32)
    score = _dot(imp, pool, precision=HI)
    j = lax.broadcasted_iota(jnp.int32, (m, LANES), 1)
    forced = (j == 0) | (j == cur) | (j == cur - 1)
    score = jnp.where(forced, score + FORCE_BONUS, score)
    score = jnp.where(j > cur, NEG, score)
    rank = jnp.zeros((m, LANES), F32)
    for i in range(n_blk):
        ci = score[:, i:i + 1]
        ahead = (ci > score) | ((ci == score) & (j > i))
        rank = rank + ahead.astype(F32)
    return (rank < float(min(N_SEL, n_blk))) & (score > 0.5 * NEG)


def _expand_sel(sel_bf, t0, n):
    jj = lax.broadcasted_iota(jnp.int32, (LANES, n), 0)
    tt = t0 + lax.broadcasted_iota(jnp.int32, (LANES, n), 1)
    e = (_shr(tt, SEL_BLOCK) == jj).astype(BF16)
    return _dot(sel_bf, e)


AUG_SEL = 16
MAX_SEL_BLOCKS = 32


def _aug_rows(pos):
    n = pos.shape[1]
    r = lax.broadcasted_iota(jnp.int32, (HEAD_DIM, n), 0)
    onehot = (r >= AUG_SEL) & (_shr(pos, SEL_BLOCK) == r - AUG_SEL)
    return jnp.where(r == 0, _shr(pos, LANES).astype(F32),
                     jnp.where(r == 1, (pos & (LANES - 1)).astype(F32), onehot.astype(F32)))


def _select_bias_t(imp, qi, off, n_blk):
    n = imp.shape[1]
    base = off + AUG_SEL
    prow = lax.broadcasted_iota(jnp.int32, (LANES, n), 0) - base
    pool_t = (prow == _shr(lax.broadcasted_iota(jnp.int32, (LANES, n), 1), SEL_BLOCK // CMP_STRIDE)).astype(F32)
    score = lax.dot_general(pool_t, imp, (((1,), (1,)), ((), ())), preferred_element_type=F32,
                            precision=HI)[base:base + MAX_SEL_BLOCKS]
    j = lax.broadcasted_iota(jnp.int32, (MAX_SEL_BLOCKS, 1), 0)
    cur = _shr(qi * Q_BLOCK + lax.broadcasted_iota(jnp.int32, (1, Q_BLOCK), 1), SEL_BLOCK)
    forced = (j == 0) | (j == cur) | (j == cur - 1)
    score = jnp.where(forced, score + FORCE_BONUS, score)
    score = jnp.where(j > cur, NEG, score)
    rank = jnp.zeros(score.shape, F32)
    for i in range(n_blk):
        ci = score[i:i + 1, :]
        ahead = (ci > score) | ((ci == score) & (j > i))
        rank = rank + ahead.astype(F32)
    chosen = (rank < float(min(N_SEL, n_blk))) & (score > 0.5 * NEG)
    bias_t = jnp.where(chosen, 0.0, NEG)
    parts = [bias_t, jnp.zeros((LANES - base - MAX_SEL_BLOCKS, Q_BLOCK), F32)]
    if base:
        parts = [jnp.zeros((base, Q_BLOCK), F32)] + parts
    return jnp.concatenate(parts, axis=0).T


def _nsa_body(nq_ref, cmp_ref, kv_ref, aug_ref, gate_ref, o_ref, *, n_blk, tk, wlen):
    qi = pl.program_id(1)
    s_len = kv_ref.shape[1]
    rows = N_REP * Q_BLOCK
    row = lax.broadcasted_iota(jnp.int32, (rows, 1), 0)
    qpos = qi * Q_BLOCK + (row & (Q_BLOCK - 1))
    qpos_q = qi * Q_BLOCK + lax.broadcasted_iota(jnp.int32, (Q_BLOCK, 1), 0)
    lane = lax.broadcasted_iota(jnp.int32, (1, LANES), 1)
    nhb = cmp_ref.shape[0]
    end_col = lax.broadcasted_iota(jnp.int32, (nhb, 1), 0) * CMP_STRIDE + (CMP_LEN - 1)
    end = lax.broadcasted_iota(jnp.int32, (1, nhb), 1) * CMP_STRIDE + (CMP_LEN - 1)

    n_kb = _shr(qi * Q_BLOCK + Q_BLOCK + tk - 1, tk)
    t_last = pl.multiple_of((n_kb - 1) * tk, tk)
    causal = jnp.where(t_last + lax.broadcasted_iota(jnp.int32, (1, tk), 1) <= qpos_q, 0.0, NEG)
    t_win = pl.multiple_of(jnp.clip((qi - WINDOW // Q_BLOCK) * Q_BLOCK, 0, s_len - wlen), Q_BLOCK)
    d_win = qpos_q - (t_win + lax.broadcasted_iota(jnp.int32, (1, wlen), 1))
    band = jnp.where((d_win >= 0) & (d_win < WINDOW), 0.0, NEG)

    def masked(s, bias):
        n = s.shape[1]
        return (s.reshape(N_REP, Q_BLOCK, n) + bias[None]).reshape(rows, n)

    for g in range(N_GROUPS):
        off = HEAD_DIM * (1 - g)
        q = jnp.concatenate([nq_ref[:, (N_REP * g + r) * LANES:(N_REP * g + r + 1) * LANES]
                             for r in range(N_REP)], axis=0).astype(F32)
        slope = _slope_rows(g)
        q_alibi = q + jnp.where(lane == off, slope * float(LANES), 0.0) + jnp.where(lane == off + 1, slope, 0.0)
        q_win = q_alibi.astype(BF16)

        def keys(lo, t0, n):
            k = kv_ref[lo + g * HEAD_DIM:lo + (g + 1) * HEAD_DIM, pl.ds(t0, n)]
            a = aug_ref[:, pl.ds(t0, n)]
            return jnp.concatenate([k, a] if g == 0 else [a, k], axis=0)

        in_g = _shr(lane, HEAD_DIM) == g
        kc = cmp_ref[:, 0:LANES].astype(F32)
        kc = jnp.where(in_g, kc, jnp.where(lane == off, _shr(end_col, LANES).astype(F32),
                                           jnp.where(lane == off + 1, (end_col & (LANES - 1)).astype(F32), 0.0)))
        p = _cmp_softmax(_dot_nt(q_win, kc.astype(BF16)), end <= qpos)
        o_cmp = _dot(p.astype(BF16), cmp_ref[:, LANES:2 * LANES])
        imp = p[0:Q_BLOCK]
        for r in range(1, N_REP):
            imp = imp + p[r * Q_BLOCK:(r + 1) * Q_BLOCK]
        sel_bias = _select_bias_t(imp, qi, off, n_blk)
        q_sel = (q_alibi + jnp.concatenate([sel_bias] * N_REP, axis=0)).astype(BF16)

        def sel_tile(t0, carry, last):
            m_old, l_old, acc = carry
            s = _dot(q_sel, keys(0, t0, tk))
            if last:
                s = masked(s, causal)
            v = kv_ref[LANES:2 * LANES, pl.ds(t0, tk)]
            m_new = jnp.maximum(m_old, jnp.max(s, axis=-1, keepdims=True))
            alpha = jnp.exp(m_old - m_new)
            p = jnp.exp(s - m_new)
            l_new = alpha * l_old + jnp.sum(p, axis=-1, keepdims=True)
            return m_new, l_new, alpha * acc + _dot_nt(p.astype(BF16), v)

        carry = (jnp.full((rows, 1), NEG, F32), jnp.zeros((rows, 1), F32), jnp.zeros((rows, LANES), F32))
        carry = lax.fori_loop(0, n_kb - 1, lambda kb, c: sel_tile(pl.multiple_of(kb * tk, tk), c, False), carry)
        _, l_sel, acc_sel = sel_tile(t_last, carry, True)
        o_sel = acc_sel / l_sel

        s = masked(_dot(q_win, keys(2 * LANES, t_win, wlen)), band)
        p = jnp.exp(s - jnp.max(s, axis=-1, keepdims=True))
        v = kv_ref[3 * LANES:4 * LANES, pl.ds(t_win, wlen)]
        o_win = _dot_nt(p.astype(BF16), v) / jnp.sum(p, axis=-1, keepdims=True)

        for r in range(N_REP):
            hh = N_REP * g + r
            rs = slice(r * Q_BLOCK, (r + 1) * Q_BLOCK)
            o = (gate_ref[:, 8 + hh:9 + hh] * o_cmp[rs] + gate_ref[:, 16 + hh:17 + hh] * o_sel[rs]
                 + gate_ref[:, 24 + hh:25 + hh] * o_win[rs])
            o_ref[:, hh * LANES:(hh + 1) * LANES] = jnp.where(in_g, o, 0.0).astype(o_ref.dtype)


def _nsa_aug_body(o_ref):
    o_ref[...] = _aug_rows(lax.broadcasted_iota(jnp.int32, (1, o_ref.shape[1]), 1)).astype(o_ref.dtype)


def _nsa_aug(s):
    return pl.pallas_call(
        _nsa_aug_body,
        out_shape=jax.ShapeDtypeStruct((HEAD_DIM, s), BF16),
        name="nsa_aug",
    )()


def _nsa_prompt(nq, cmpkv, nsatb, aug, gates):
    b, s, _ = nq.shape
    nhb = cmpkv.shape[1]
    n_blk = s // SEL_BLOCK
    assert n_blk <= MAX_SEL_BLOCKS and s // LANES <= 256
    return pl.pallas_call(
        functools.partial(_nsa_body, n_blk=n_blk, tk=min(ATT_TK, s), wlen=min(WINDOW + Q_BLOCK, s)),
        grid=(b, s // Q_BLOCK),
        in_specs=[pl.BlockSpec((None, Q_BLOCK, 1024), lambda bi, qi: (bi, qi, 0)),
                  pl.BlockSpec((None, nhb, 256), lambda bi, qi: (bi, 0, 0)),
                  pl.BlockSpec((None, 512, s), lambda bi, qi: (bi, 0, 0)),
                  pl.BlockSpec((HEAD_DIM, s), lambda bi, qi: (0, 0)),
                  pl.BlockSpec((None, Q_BLOCK, LANES), lambda bi, qi: (bi, qi, 0))],
        out_specs=pl.BlockSpec((None, Q_BLOCK, 1024), lambda bi, qi: (bi, qi, 0)),
        out_shape=jax.ShapeDtypeStruct((b, s, 1024), BF16),
        compiler_params=_cparams("parallel", "arbitrary"),
        name="nsa_prompt",
    )(nq, cmpkv, nsatb, aug, gates)


def _merge_body(x_ref, g_ref, of_ref, og_ref, on_ref, wg_ref, wb0_ref, wb1_ref, wb2_ref, wo_ref, o_ref):
    x = x_ref[...]
    d = x.shape[1]
    h = _rms_rows(x, g_ref[...]).astype(BF16)
    y = jax.nn.sigmoid(_dot(h, wg_ref[:, 0:d])) * _dot(of_ref[...], wb0_ref[...])
    y = y + jax.nn.sigmoid(_dot(h, wg_ref[:, d:2 * d])) * _dot(og_ref[...], wb1_ref[...])
    y = y + jax.nn.sigmoid(_dot(h, wg_ref[:, 2 * d:3 * d])) * _dot(on_ref[...], wb2_ref[...])
    o_ref[...] = x + _dot(y.astype(BF16), wo_ref[...])


def _merge(x, g, o_fox, o_gm, o_nsa, wg, wb, wb2, wo, layer):
    t, d = x.shape
    tm = _token_tile(t)
    row = lambda n: pl.BlockSpec((tm, n), lambda i: (i, 0))
    return pl.pallas_call(
        _merge_body,
        grid=(t // tm,),
        in_specs=[row(d), _const_spec((1, d)), row(512), row(512), row(1024), _const_spec(wg.shape),
                  _slab_spec(wb, 3 * layer), _slab_spec(wb, 3 * layer + 1), _const_spec(wb2.shape),
                  _slab_spec(wo, layer)],
        out_specs=row(d),
        out_shape=jax.ShapeDtypeStruct((t, d), F32),
        compiler_params=_cparams("parallel"),
        name="merge",
    )(x, g, o_fox, o_gm, o_nsa, wg, wb, wb, wb2, wo)


def _column(x, idx):
    lane = lax.broadcasted_iota(jnp.int32, x.shape, 1)
    return jnp.sum(jnp.where(lane == idx, x, 0.0), axis=1, keepdims=True)


def _logf_suffix_body(x_ref, o_ref):
    n, h, t = x_ref.shape
    tri = (lax.broadcasted_iota(jnp.int32, (t, t), 0) >= lax.broadcasted_iota(jnp.int32, (t, t), 1)).astype(F32)
    o_ref[...] = _dot(x_ref[...].reshape(n * h, t), tri, precision=HI).reshape(n, h, t)


def _logf_suffix(cache_lf, layer):
    n_pool = cache_lf.shape[1]
    blk = next(c for c in (64, 32, 16, 8, 4, 2, 1) if n_pool % c == 0)
    return pl.pallas_call(
        _logf_suffix_body,
        grid=(n_pool // blk,),
        in_specs=[pl.BlockSpec((None, blk, N_HEADS, PAGE), lambda i: (layer, i, 0, 0))],
        out_specs=pl.BlockSpec((blk, N_HEADS, PAGE), lambda i: (i, 0, 0)),
        out_shape=jax.ShapeDtypeStruct((n_pool, N_HEADS, PAGE), F32),
        compiler_params=_cparams("parallel"),
        name="logf_suffix",
    )(cache_lf)


def _fox_decode_body(pt_ref, *refs, n_pages):
    kv = refs[:n_pages]
    q_ref, new_ref, lfn_ref, sfx_ref, o_ref, s_ref = refs[n_pages:]
    b = pl.program_id(0)
    w = N_HEADS * HEAD_DIM
    row = lax.broadcasted_iota(jnp.int32, (N_HEADS, w), 0)
    lane = lax.broadcasted_iota(jnp.int32, (N_HEADS, w), 1)
    diag = _shr(lane, HEAD_DIM) == row
    qbd = jnp.where(diag, jnp.broadcast_to(q_ref[...].astype(F32), (N_HEADS, w)), 0.0).astype(BF16)
    last = lax.broadcasted_iota(jnp.int32, (N_HEADS, PAGE), 1) == PAGE - 1

    later = _column(lfn_ref[...], b)
    for i in reversed(range(n_pages)):
        incl = sfx_ref[pt_ref[b, i]]
        after = jnp.where(last, 0.0, pltpu.roll(incl, PAGE - 1, 1))
        s_ref[:, i * PAGE:(i + 1) * PAGE] = _dot(qbd, kv[i][0].astype(BF16)) + after + later
        later = later + incl[:, 0:1]

    k_new = new_ref[:, 0:w].astype(BF16).astype(F32)
    v_new = new_ref[:, w:2 * w].astype(BF16).astype(F32)
    s_new = jnp.sum(qbd.astype(F32) * k_new, axis=-1, keepdims=True)
    m = jnp.maximum(jnp.max(s_ref[...], axis=-1, keepdims=True), s_new)
    p_new = jnp.exp(s_new - m)
    l = p_new
    o = p_new.astype(BF16).astype(F32) * v_new
    for i in range(n_pages):
        p = jnp.exp(s_ref[:, i * PAGE:(i + 1) * PAGE] - m)
        l = l + jnp.sum(p, axis=-1, keepdims=True)
        o = o + _dot_nt(p.astype(BF16), kv[i][1].astype(BF16))
    o_ref[...] = jnp.sum(jnp.where(diag, o / l, 0.0), axis=0, keepdims=True).astype(o_ref.dtype)


def _fox_decode(page_table, cache_kv, lf_suffix, layer, fq_s, foxkv_s, lft_s):
    db, n_pages = page_table.shape
    w = N_HEADS * HEAD_DIM
    grid_spec = pltpu.PrefetchScalarGridSpec(
        num_scalar_prefetch=1,
        grid=(db,),
        in_specs=[pl.BlockSpec((None, None, 2, w, PAGE), lambda b, pt, i=i: (layer, pt[b, i], 0, 0, 0))
                  for i in range(n_pages)]
                 + [pl.BlockSpec((None, 1, w), lambda b, pt: (b, 0, 0)),
                    pl.BlockSpec((None, 1, 2 * w), lambda b, pt: (b, 0, 0)),
                    pl.BlockSpec(lft_s.shape, lambda b, pt: (0, 0)),
                    pl.BlockSpec(lf_suffix.shape, lambda b, pt: (0, 0, 0), pipeline_mode=pl.Buffered(1))],
        out_specs=pl.BlockSpec((None, 1, w), lambda b, pt: (b, 0, 0)),
        scratch_shapes=[pltpu.VMEM((N_HEADS, n_pages * PAGE), F32)],
    )
    return pl.pallas_call(
        functools.partial(_fox_decode_body, n_pages=n_pages),
        grid_spec=grid_spec,
        out_shape=jax.ShapeDtypeStruct((db, 1, w), BF16),
        compiler_params=_cparams("arbitrary"),
        name="fox_decode",
    )(page_table, *([cache_kv] * n_pages), fq_s, foxkv_s, lft_s, lf_suffix)


def _nsa_decode_body(pt_ref, *refs, n_pages, rps, n_prev):
    pg = [refs[r * n_pages:(r + 1) * n_pages] for r in range(rps)]
    (win_ref, q_ref, new_ref, gate_ref, wnew_ref, wa_ref, wb_ref, pea_ref, peb_ref, w2_ref, kg0_ref,
     pet_ref) = refs[rps * n_pages:rps * n_pages + 12]
    o_ref, wout_ref, xa_ref, xb_ref, s_ref = refs[rps * n_pages + 12 + n_prev:]
    past = n_pages * PAGE
    nhb = past // CMP_STRIDE
    wbuf = win_ref.shape[3]
    cur = past // SEL_BLOCK
    n_blk = cur + 1
    m = rps * N_HEADS

    def per_row(fn):
        return jnp.concatenate([fn(r) for r in range(rps)], axis=0)

    tp = lax.broadcasted_iota(jnp.int32, (2 * PAGE, 2 * PAGE), 0)
    r16 = tp & (CMP_STRIDE - 1)
    src = _shr(r16, 8) * PAGE + (r16 & 7) * CMP_STRIDE + _shr(tp, CMP_STRIDE)
    perm = (lax.broadcasted_iota(jnp.int32, (2 * PAGE, 2 * PAGE), 1) == src).astype(BF16)
    for r in range(rps):
        for j in range(n_pages // 2):
            pkv = jnp.concatenate([jnp.concatenate([pg[r][2 * j + pp][0], pg[r][2 * j + pp][1]], axis=0)
                                   for pp in range(2)], axis=1)
            for half, x_ref in enumerate((xa_ref, xb_ref)):
                rows = _dot_nt(perm, (pkv + pet_ref[half]).astype(BF16)).astype(BF16)
                for l in range(CMP_STRIDE):
                    x_ref[r * nhb + 16 * j:r * nhb + 16 * (j + 1), 256 * l:256 * (l + 1)] = rows[16 * l:16 * (l + 1), :]
    kc, vc = _compress_finish(xa_ref[...], xb_ref[...], wa_ref, wb_ref, w2_ref, kg0_ref)
    kc = kc.astype(BF16)
    vc = vc.astype(BF16)

    qf = q_ref[...].astype(F32)
    q = [qf[r * N_HEADS:(r + 1) * N_HEADS].astype(BF16) for r in range(rps)]
    row = lax.broadcasted_iota(jnp.int32, (m, 1), 0)
    head = row & (N_HEADS - 1)
    slope = _slope_col(head)
    lane = lax.broadcasted_iota(jnp.int32, (1, LANES), 1)

    def rows_of(x, r):
        return x[r * N_HEADS:(r + 1) * N_HEADS]

    def new_rows(lo):
        return per_row(lambda r: jnp.broadcast_to(new_ref[r][:, lo:lo + LANES].astype(F32), (N_HEADS, LANES)))

    end = lax.broadcasted_iota(jnp.int32, (1, nhb), 1) * CMP_STRIDE + (CMP_LEN - 1)
    s = per_row(lambda r: _dot_nt(q[r], kc[r * nhb:(r + 1) * nhb])) - slope * (past - end).astype(F32)
    p = _cmp_softmax(s, jnp.broadcast_to(end <= past, s.shape))
    o_cmp = per_row(lambda r: _dot(rows_of(p, r).astype(BF16), vc[r * nhb:(r + 1) * nhb]))
    same = (_shr(lax.broadcasted_iota(jnp.int32, (m, m), 0), N_REP)
            == _shr(lax.broadcasted_iota(jnp.int32, (m, m), 1), N_REP)).astype(F32)
    imp = _dot(same, p, precision=HI)
    sel = _select_blocks(imp, jnp.full((m, 1), cur, jnp.int32), n_blk)

    for r in range(rps):
        for i in range(n_pages):
            s_ref[r * N_HEADS:(r + 1) * N_HEADS, i * PAGE:(i + 1) * PAGE] = _dot(q[r], pg[r][i][2].astype(BF16))
    kpos = lax.broadcasted_iota(jnp.int32, (1, past), 1)
    s = s_ref[...] - slope * (past - kpos).astype(F32)
    s = jnp.where(_expand_sel(sel.astype(BF16), 0, past) > 0.5, s, NEG)
    new_ok = jnp.sum(jnp.where(lane == cur, sel.astype(F32), 0.0), axis=-1, keepdims=True) > 0.5
    s_new = jnp.where(new_ok, jnp.sum(qf * new_rows(0), axis=-1, keepdims=True), NEG)
    mx = jnp.maximum(jnp.max(s, axis=-1, keepdims=True), s_new)
    p_new = jnp.where(new_ok, jnp.exp(s_new - mx), 0.0)
    p = jnp.where(s > 0.5 * NEG, jnp.exp(s - mx), 0.0)
    l = jnp.sum(p, axis=-1, keepdims=True) + p_new
    s_ref[...] = p

    def sel_pv(r):
        acc = jnp.zeros((N_HEADS, LANES), F32)
        for i in range(n_pages):
            pr = s_ref[r * N_HEADS:(r + 1) * N_HEADS, i * PAGE:(i + 1) * PAGE]
            acc = acc + _dot_nt(pr.astype(BF16), pg[r][i][3].astype(BF16))
        return acc

    o_sel = (per_row(sel_pv) + p_new.astype(BF16).astype(F32) * new_rows(LANES)) / jnp.where(l > 0.0, l, 1.0)

    pos = lax.broadcasted_iota(jnp.int32, (1, wbuf), 1)
    d = wbuf - pos
    s = per_row(lambda r: _dot(q[r], win_ref[r, 0].astype(BF16))) - slope * d.astype(F32)
    s = jnp.where(d < WINDOW, s, NEG)
    s_new = jnp.sum(qf * new_rows(2 * LANES), axis=-1, keepdims=True)
    mx = jnp.maximum(jnp.max(s, axis=-1, keepdims=True), s_new)
    p = jnp.exp(s - mx)
    p_new = jnp.exp(s_new - mx)
    l = jnp.sum(p, axis=-1, keepdims=True) + p_new
    o_win = (per_row(lambda r: _dot_nt(rows_of(p, r).astype(BF16), win_ref[r, 1].astype(BF16)))
             + p_new.astype(BF16).astype(F32) * new_rows(3 * LANES)) / l

    o = gate_ref[:, 0:1] * o_cmp + gate_ref[:, 1:2] * o_sel + gate_ref[:, 2:3] * o_win
    in_group = _shr(lane, HEAD_DIM) == _shr(head, N_REP)
    o_ref[...] = jnp.where(in_group, o, 0.0).astype(o_ref.dtype)

    for r in range(rps):
        bidx = pl.program_id(0) * rps + r
        for kv in range(2):
            wout_ref[r, kv] = jnp.where(pos == wbuf - 1, _column(wnew_ref[kv * LANES:(kv + 1) * LANES, :], bidx),
                                        pltpu.roll(win_ref[r, kv], wbuf - 1, 1))


def _nsa_decode(page_table, cache_nsa, win_state, layer, nq_s, nsab_s, gates_s, wint_s, cw, prev_win=None):
    db, n_pages = page_table.shape
    depth, wbuf = win_state.shape[0], win_state.shape[4]
    prev = () if prev_win is None else (prev_win,)
    rps = DECODE_ROWS
    assert db % rps == 0 and n_pages % 2 == 0
    m = rps * N_HEADS
    nhb = n_pages * PAGE // CMP_STRIDE
    in_specs = [pl.BlockSpec((None, None, 4, LANES, PAGE), lambda b, pt, i=i, r=r: (layer, pt[b * rps + r, i], 0, 0, 0))
                for r in range(rps) for i in range(n_pages)]
    in_specs += [pl.BlockSpec((None, rps, 2, LANES, wbuf), lambda b, pt: (layer, b, 0, 0, 0)),
                 pl.BlockSpec((m, LANES), lambda b, pt: (b, 0)),
                 pl.BlockSpec((rps, 1, 512), lambda b, pt: (b, 0, 0)),
                 pl.BlockSpec((m, 3), lambda b, pt: (b, 0)),
                 pl.BlockSpec(wint_s.shape, lambda b, pt: (0, 0))]
    in_specs += [pl.BlockSpec(a.shape, lambda b, pt, nd=a.ndim: (0,) * nd, pipeline_mode=pl.Buffered(1)) for a in cw]
    in_specs += [pl.BlockSpec(memory_space=pl.ANY)] * len(prev)
    grid_spec = pltpu.PrefetchScalarGridSpec(
        num_scalar_prefetch=1,
        grid=(db // rps,),
        in_specs=in_specs,
        out_specs=[pl.BlockSpec((m, LANES), lambda b, pt: (b, 0)),
                   pl.BlockSpec((None, rps, 2, LANES, wbuf), lambda b, pt: (layer, b, 0, 0, 0))],
        scratch_shapes=[pltpu.VMEM((rps * nhb, CMP_STRIDE * 256), BF16),
                        pltpu.VMEM((rps * nhb, CMP_STRIDE * 256), BF16),
                        pltpu.VMEM((m, n_pages * PAGE), F32)],
    )
    return pl.pallas_call(
        functools.partial(_nsa_decode_body, n_pages=n_pages, rps=rps, n_prev=len(prev)),
        grid_spec=grid_spec,
        out_shape=[jax.ShapeDtypeStruct((db * N_HEADS, LANES), BF16),
                   jax.ShapeDtypeStruct((depth, db, 2, LANES, wbuf), F32)],
        input_output_aliases={1 + rps * n_pages + 5 + len(cw): 1} if prev else {},
        compiler_params=_cparams("arbitrary"),
        name="nsa_decode",
    )(page_table, *([cache_nsa] * (rps * n_pages)), win_state, nq_s, nsab_s, gates_s, wint_s, *cw, *prev)


O_F, O_GM, O_NQ, O_NKV, O_NG, O_MG = 1536, 1544, 2568, 3080, 3848, 3872
_HEAD_PLACE = (np.arange(N_HEADS)[:, None] // N_REP == np.arange(N_GROUPS)[None, :]).astype(np.float32)


def _prep_w_in(w):
    d = w.shape[0]
    wq = w[:, O_NQ:O_NKV].reshape(d, N_HEADS, 1, HEAD_DIM)
    wq = (wq * _HEAD_PLACE[None, :, :, None]).reshape(d, N_HEADS * LANES)
    small = jnp.concatenate([w[:, O_F:O_GM], w[:, O_NG:O_MG], jnp.zeros((d, LANES - 32), w.dtype)], axis=1)
    w_tok = jnp.concatenate([w[:, 0:O_F], w[:, O_GM:O_NQ], wq, w[:, O_NKV:O_NG], small], axis=1).astype(BF16)
    wt = w.T
    w_feat = jnp.concatenate([wt[512:O_F], wt[O_NKV:O_NG], wt[O_F:O_GM], jnp.zeros((8, d), w.dtype)],
                             axis=0).astype(BF16)
    return w_tok, w_feat, w[:, O_MG:].astype(BF16)


def _pad_heads(g):
    return (_HEAD_PLACE[:, :, None] * g[None, None, :]).reshape(1, N_HEADS * LANES)


def _prep_compress(w1, w2, pe, kg0):
    w1r = w1.reshape(2, 2, CMP_STRIDE, HEAD_DIM, HEAD_DIM).astype(BF16)

    def block_diag(blocks):
        z = jnp.zeros_like(blocks[0])
        return jnp.concatenate([jnp.concatenate([blk if i == j else z for j in range(4)], axis=-1)
                                for i, blk in enumerate(blocks)], axis=-2)

    wl = block_diag([w1r[c] for c in range(2) for _ in range(N_GROUPS)]).reshape(2, CMP_STRIDE * 256, 256)
    w2bd = block_diag([w2[c] for c in range(2) for _ in range(N_GROUPS)])
    pex = jnp.broadcast_to(pe.transpose(1, 0, 2)[:, :, None, :], (CMP_LEN, 2, N_GROUPS, HEAD_DIM))
    pex = pex.reshape(2, CMP_STRIDE, 1, 256)
    pet = jnp.tile(pex.reshape(2, CMP_STRIDE, 256).transpose(0, 2, 1), (1, 1, 2 * PAGE // CMP_STRIDE))
    return (wl[0], wl[1], pex[0], pex[1], w2bd.astype(BF16), jnp.tile(kg0, 2)[None, :], pet)


def _layer(xp, xs, lw, page_table, caches, layer, depth, prev, dims):
    b, s, db = dims

    xp = _ffn(xp, *lw['ffn1'], layer)
    xs = _ffn(xs, *lw['ffn1'], layer)

    pw = lw['proj']
    (fq, foxt, foxtb, gmu, gmv, nq, nsat, wint, nsatb, cmpraw, lft, gates) = _proj(
        xp, pw, b, s, False, layer, depth, None if prev is None else prev[:4])
    (fq_s, foxt_s, _, gmu_s, gmv_s, nq_s, nsat_s, wint_s, _, _, lft_s, gates_s,
     foxkv_s, nsab_s) = _proj(xs, pw, 1, db, True)

    c = _cumsum(lft, layer).reshape(b, 4, 2, s)
    o_fox_p = _fox_prompt(fq.reshape(b, s, 512), foxtb, c)
    o_gm_p = _gmlp_prompt(gmu, gmv, lw['gmlp_w_s'], lw['gmlp_bexp'])
    cmpkv = _compress_prompt(cmpraw.reshape(b, s, 256), lw['cw'])
    o_nsa_p = _nsa_prompt(nq.reshape(b, s, 1024), cmpkv, nsatb, lw['nsa_aug'], gates.reshape(b, s, LANES))
    xp = _merge(xp, lw['mix_norm'], o_fox_p.reshape(b * s, 512), o_gm_p, o_nsa_p.reshape(b * s, 1024),
                lw['w_gate'], lw['wb'], lw['wb2'], lw['w_out'], layer)

    cache_fox_kv, cache_fox_lf, cache_nsa, win_state = caches
    o_fox_s = _fox_decode(page_table, cache_fox_kv, _logf_suffix(cache_fox_lf, layer), layer, fq_s.reshape(db, 1, 512),
                          foxkv_s.reshape(db, 1, 1024), lft_s.reshape(N_HEADS, db))
    o_gm_s = _gmlp_sample(gmu_s, gmv_s, lw['gmlp_w00'], lw['gmlp_b0'])
    g3 = gates_s[:, 8:32].reshape(db, 3, N_HEADS).transpose(0, 2, 1).reshape(db * N_HEADS, 3)
    o_nsa_s, win_next = _nsa_decode(page_table, cache_nsa, win_state, layer, nq_s.reshape(db * N_HEADS, LANES),
                                    nsab_s.reshape(db, 1, 512), g3, wint_s.reshape(256, db), lw['cw'],
                                    None if prev is None else prev[4])
    xs = _merge(xs, lw['mix_norm'], o_fox_s.reshape(db, 512), o_gm_s, o_nsa_s.reshape(db, 1024),
                lw['w_gate'], lw['wb'], lw['wb2'], lw['w_out'], layer)

    xp = _ffn(xp, *lw['ffn2'], layer)
    xs = _ffn(xs, *lw['ffn2'], layer)

    state = dict(
        fox_kv_s=foxt_s, fox_logf_s=lft_s, nsa_kv_s=nsat_s,
        gmlp_v_p=gmv.reshape(b, s, 512)[:, s - min(CHUNK, s):], gmlp_v_s=gmv_s.reshape(db, 1, 512),
    )
    return xp, xs, state, (foxt, nsat, wint, lft, win_next)


def kernel(x_prompt, x_sample, page_table, cache_fox_kv, cache_fox_logf, cache_nsa_kv, state_nsa_win, ffn1_norm, ffn1_w_gu, ffn1_w_down, mix_norm, w_in, fox_b_f, fox_qk_gain, gmlp_v_gain, gmlp_w_s, gmlp_b_s, nsa_q_gain, nsa_k_gain, nsa_cmp_pe, nsa_cmp_w1, nsa_cmp_w2, w_branch, w_out, ffn2_norm, ffn2_w_gu, ffn2_w_down):
    b, s, d = x_prompt.shape
    db = x_sample.shape[0]
    depth = w_in.shape[0]
    n_pool = cache_fox_kv.shape[1]
    wb = state_nsa_win.shape[2]
    xp = x_prompt.reshape(b * s, d)
    xs = x_sample.reshape(db, d)

    caches = (cache_fox_kv.transpose(0, 1, 3, 4, 5, 2).reshape(depth, n_pool, 2, N_HEADS * HEAD_DIM, PAGE),
              cache_fox_logf.transpose(0, 1, 3, 2),
              cache_nsa_kv.transpose(0, 1, 3, 4, 5, 2).reshape(depth, n_pool, 4, LANES, PAGE),
              state_nsa_win.transpose(0, 1, 3, 4, 5, 2).reshape(depth, db, 2, LANES, wb))

    nsa_rows = (np.arange(N_HEADS * LANES) % LANES) // HEAD_DIM == (np.arange(N_HEADS * LANES) // LANES) // N_REP
    col = lambda g: g[:, None]
    ffn1 = (ffn1_norm[:, None, :], ffn1_w_gu.astype(BF16), ffn1_w_down.astype(BF16))
    ffn2 = (ffn2_norm[:, None, :], ffn2_w_gu.astype(BF16), ffn2_w_down.astype(BF16))
    wb_all = w_branch.astype(BF16).reshape(depth * 3, w_branch.shape[2], d)
    wo_all = w_out.astype(BF16)
    nsa_aug = _nsa_aug(s)
    states = []
    stacked = None
    for l in range(depth):
        w_tok, w_feat, w_gate = _prep_w_in(w_in[l])
        wb2 = jnp.zeros((N_HEADS * LANES, d), F32).at[np.nonzero(nsa_rows)[0]].set(w_branch[l, 2])
        proj = (mix_norm[l][None, :], w_tok, w_feat,
                jnp.tile(fox_qk_gain[l, 0], 2)[None, :], jnp.tile(fox_qk_gain[l, 1], 2)[None, :],
                col(fox_qk_gain[l, 1]), gmlp_v_gain[l][None, :], _pad_heads(nsa_q_gain[l]),
                jnp.tile(nsa_k_gain[l, 1], 2)[None, :], jnp.tile(nsa_k_gain[l, 2], 2)[None, :],
                col(nsa_k_gain[l, 1]), col(nsa_k_gain[l, 2]), col(fox_b_f[l]))
        lw = dict(
            ffn1=ffn1, ffn2=ffn2, wb=wb_all, w_out=wo_all,
            mix_norm=mix_norm[l][None, :], proj=proj, w_gate=w_gate,
            gmlp_w_s=gmlp_w_s[l], gmlp_bexp=jnp.repeat(gmlp_b_s[l].T, HEAD_DIM, axis=1),
            gmlp_w00=jnp.repeat(gmlp_w_s[l, :, 0, 0], HEAD_DIM)[None, :],
            gmlp_b0=jnp.repeat(gmlp_b_s[l, :, 0], HEAD_DIM)[None, :],
            cw=_prep_compress(nsa_cmp_w1[l], nsa_cmp_w2[l], nsa_cmp_pe[l], nsa_k_gain[l, 0]),
            wb2=wb2.astype(BF16), nsa_aug=nsa_aug,
        )
        xp, xs, st, stacked = _layer(xp, xs, lw, page_table, caches, l, depth, stacked, (b, s, db))
        states.append(st)

    def stack(name):
        return jnp.stack([st[name] for st in states])

    def tok_major(a, *feat_dims):
        return jnp.moveaxis(a.reshape(a.shape[:2] + feat_dims + a.shape[3:]), -1, 2)

    def tok_major_s(a, *feat_dims):
        return jnp.swapaxes(tok_major(a, *feat_dims), 1, 2)

    foxt, nsat, wint, lft, win_next = stacked
    return (xp.reshape(b, s, d), xs.reshape(db, 1, d),
            tok_major(foxt, 2, N_HEADS, HEAD_DIM), tok_major_s(stack('fox_kv_s'), 2, N_HEADS, HEAD_DIM),
            tok_major(lft, N_HEADS), tok_major_s(stack('fox_logf_s'), N_HEADS),
            tok_major(nsat, 4, N_GROUPS, HEAD_DIM), tok_major_s(stack('nsa_kv_s'), 4, N_GROUPS, HEAD_DIM),
            tok_major(wint[:, :, :, s - min(WINDOW, s):], 2, N_GROUPS, HEAD_DIM),
            tok_major(win_next.reshape(depth, db, 2 * LANES, wb), 2, N_GROUPS, HEAD_DIM),
            stack('gmlp_v_p'), stack('gmlp_v_s'))
```

```python
import functools

import jax
import jax.numpy as jnp
import numpy as np
from jax import lax
from jax.experimental import pallas as pl
from jax.experimental.pallas import tpu as pltpu

F32 = jnp.float32
BF16 = jnp.bfloat16

HEAD_DIM = 64
LANES = 128
PAGE = 128
CHUNK = 128
GMLP_CHUNKS_PER_STEP = 4
DECODE_ROWS = 2
Q_BLOCK = 256
FOX_TQ = 512
ATT_TK = 512
CMP_LEN = 32
CMP_STRIDE = 16
SEL_BLOCK = 64
N_SEL = 8
WINDOW = 512
N_GROUPS = 2
N_REP = 4
N_HEADS = 8
FORCE_BONUS = 1.0e4
NEG = -1.0e30
EPS = 1e-6
SCALE = HEAD_DIM ** -0.5
VMEM_LIMIT = 56 * 1024 * 1024
HI = lax.Precision.HIGHEST


def _cparams(*sem):
    return pltpu.CompilerParams(dimension_semantics=sem, vmem_limit_bytes=VMEM_LIMIT)


def _dot(a, b, precision=None):
    return jnp.dot(a, b, preferred_element_type=F32, precision=precision)


def _dot_nt(a, b):
    return lax.dot_general(a, b, (((1,), (1,)), ((), ())), preferred_element_type=F32)


def _shr(x, pow2):
    return jnp.right_shift(x, int(pow2).bit_length() - 1)


def _rms_rows(x, g):
    ms = jnp.mean(x * x, axis=-1, keepdims=True)
    return x * lax.rsqrt(ms + EPS) * g


def _headnorm_pair(zb, gain):
    lo = lax.broadcasted_iota(jnp.int32, zb.shape, 1) < HEAD_DIM
    sq = zb * zb
    s_lo = jnp.sum(jnp.where(lo, sq, 0.0), axis=-1, keepdims=True)
    s_hi = jnp.sum(jnp.where(lo, 0.0, sq), axis=-1, keepdims=True)
    ms = jnp.where(lo, s_lo, s_hi) * (1.0 / HEAD_DIM)
    return zb * lax.rsqrt(ms + EPS) * gain


def _headnorm_cols(zt, gain_col):
    ms = jnp.mean(zt * zt, axis=0, keepdims=True)
    return zt * lax.rsqrt(ms + EPS) * gain_col


def _silu(x):
    return x * jax.nn.sigmoid(x)


def _gelu_tanh(x):
    return 0.5 * x * (1.0 + jnp.tanh(np.sqrt(2.0 / np.pi) * (x + 0.044715 * (x * x * x))))


def _log_sigmoid(x):
    return jnp.minimum(x, 0.0) - jnp.log1p(jnp.exp(-jnp.abs(x)))


def _token_tile(t):
    for tm in (512, 384, 256, 128):
        if t % tm == 0:
            return tm
    raise ValueError(f"token count {t} is not a multiple of 128")


def _const_spec(shape):
    nd = len(shape)
    return pl.BlockSpec(shape, lambda *_: (0,) * nd, pipeline_mode=pl.Buffered(1))


def _slab_spec(a, idx):
    nd = a.ndim - 1
    return pl.BlockSpec((None,) + a.shape[1:], lambda *_: (idx,) + (0,) * nd, pipeline_mode=pl.Buffered(1))


def _ffn_body(x_ref, g_ref, wgu_ref, wd_ref, o_ref, *, d_ff, fc):
    x = x_ref[...]
    h = _rms_rows(x, g_ref[...]).astype(BF16)
    acc = jnp.zeros_like(x)
    for c in range(d_ff // fc):
        g = _dot(h, wgu_ref[:, c * fc:(c + 1) * fc])
        u = _dot(h, wgu_ref[:, d_ff + c * fc:d_ff + (c + 1) * fc])
        a = (_silu(g) * u).astype(BF16)
        acc = acc + _dot(a, wd_ref[c * fc:(c + 1) * fc, :])
    o_ref[...] = x + 0.5 * acc


def _ffn(x, g, wgu, wd, layer):
    t, d = x.shape
    d_ff = wd.shape[1]
    tm = _token_tile(t)
    fc = 256 if d_ff % 256 == 0 else 128
    return pl.pallas_call(
        functools.partial(_ffn_body, d_ff=d_ff, fc=fc),
        grid=(t // tm,),
        in_specs=[pl.BlockSpec((tm, d), lambda i: (i, 0)),
                  _slab_spec(g, layer), _slab_spec(wgu, layer), _slab_spec(wd, layer)],
        out_specs=pl.BlockSpec((tm, d), lambda i: (i, 0)),
        out_shape=jax.ShapeDtypeStruct((t, d), F32),
        compiler_params=_cparams("parallel"),
        name="ffn",
    )(x, g, wgu, wd)


C_FOX = 0
C_GM = 1536
C_NQ = 2560
C_NKV = 3584
C_SMALL = 4352
C_END = 4480
R_NKV = 1024
R_F = 1792
R_END = 1808


def _proj_body(x_ref, g_ref, w_ref, wt_ref, gq_ref, gk_ref, gkc_ref, gv_ref, gnq_ref, gk1_ref, gk2_ref,
               gk1c_ref, gk2c_ref, bfc_ref, gnqc_ref, *rest, sample, n_prev):
    out_refs = rest[n_prev:]
    (fq_ref, foxt_ref, foxtb_ref, gmu_ref, gmv_ref, nq_ref, nsat_ref, wint_ref, nsatb_ref, cmpraw_ref,
     lft_ref, gate_ref) = out_refs[:12]
    h = _rms_rows(x_ref[...], g_ref[...]).astype(BF16)

    z = _dot(h, w_ref[:, 0:512])
    for j in range(4):
        sl = slice(j * LANES, (j + 1) * LANES)
        fq_ref[:, sl] = (_headnorm_pair(z[:, sl], gq_ref[...]) * SCALE).astype(BF16)
    zt = _dot_nt(wt_ref[0:R_NKV, :], h)
    for hh in range(N_HEADS):
        rs = slice(hh * HEAD_DIM, (hh + 1) * HEAD_DIM)
        kt = _headnorm_cols(zt[rs], gkc_ref[...])
        foxt_ref[rs, :] = kt
        foxtb_ref[rs, :] = kt.astype(BF16)
    foxt_ref[512:1024, :] = zt[512:1024]
    foxtb_ref[512:1024, :] = zt[512:1024].astype(BF16)

    a = _gelu_tanh(_dot(h, w_ref[:, C_GM:C_NQ]))
    gmu_ref[...] = a[:, 0:512]
    gmv_ref[...] = _rms_rows(a[:, 512:1024], gv_ref[...])

    if sample:
        z = _dot(h, w_ref[:, C_NQ:C_NKV])
        for hh in range(N_HEADS):
            sl = slice(hh * LANES, (hh + 1) * LANES)
            zb = z[:, sl]
            ms = jnp.sum(zb * zb, axis=-1, keepdims=True) * (1.0 / HEAD_DIM)
            nq_ref[:, sl] = (zb * lax.rsqrt(ms + EPS) * (gnq_ref[:, sl] * SCALE)).astype(BF16)
    else:
        z = _dot(h, w_ref[:, C_END:C_END + 512])
        for j in range(4):
            sl = slice(j * LANES, (j + 1) * LANES)
            nq_ref[:, sl] = (_headnorm_pair(z[:, sl], gnqc_ref[...]) * SCALE).astype(BF16)

    zt = _dot_nt(wt_ref[R_NKV:R_F, :], h)
    nsat_ref[0:256, :] = zt[0:256]
    nsat_ref[384:512, :] = zt[384:512]
    nsatb_ref[128:256, :] = zt[384:512].astype(BF16)
    wint_ref[128:256, :] = zt[640:768]
    nsatb_ref[384:512, :] = zt[640:768].astype(BF16)
    for g in range(N_GROUPS):
        sk = _headnorm_cols(zt[256 + g * HEAD_DIM:256 + (g + 1) * HEAD_DIM], gk1c_ref[...])
        nsat_ref[256 + g * HEAD_DIM:256 + (g + 1) * HEAD_DIM, :] = sk
        nsatb_ref[g * HEAD_DIM:(g + 1) * HEAD_DIM, :] = sk.astype(BF16)
        wk = _headnorm_cols(zt[512 + g * HEAD_DIM:512 + (g + 1) * HEAD_DIM], gk2c_ref[...])
        wint_ref[g * HEAD_DIM:(g + 1) * HEAD_DIM, :] = wk
        nsatb_ref[256 + g * HEAD_DIM:256 + (g + 1) * HEAD_DIM, :] = wk.astype(BF16)

    cmpraw_ref[...] = _dot(h, w_ref[:, C_NKV:C_NKV + 256])

    zt = _dot_nt(wt_ref[R_F:R_END, :], h)
    lft_ref[...] = _log_sigmoid(zt[0:N_HEADS] + bfc_ref[...])
    gate_ref[...] = jax.nn.sigmoid(_dot(h, w_ref[:, C_SMALL:C_END]))

    if sample:
        foxs_ref, nsabs_ref = out_refs[12:]
        z = _dot(h, w_ref[:, 512:C_GM])
        for j in range(4):
            sl = slice(j * LANES, (j + 1) * LANES)
            foxs_ref[:, sl] = _headnorm_pair(z[:, sl], gk_ref[...])
        foxs_ref[:, 512:1024] = z[:, 512:1024]
        z = _dot(h, w_ref[:, C_NKV + 256:C_SMALL])
        nsabs_ref[:, 0:128] = _headnorm_pair(z[:, 0:128], gk1_ref[...]).astype(BF16)
        nsabs_ref[:, 128:256] = z[:, 128:256].astype(BF16)
        nsabs_ref[:, 256:384] = _headnorm_pair(z[:, 256:384], gk2_ref[...]).astype(BF16)
        nsabs_ref[:, 384:512] = z[:, 384:512].astype(BF16)


STATE_OUTS = (1, 6, 7, 10)


def _proj(x, pw, b, s, sample, layer=0, depth=1, prev=None):
    t, d = x.shape
    tm = _token_tile(s)
    nst = s // tm
    tok = lambda n, dt: (pl.BlockSpec((tm, n), lambda i: (i, 0)), jax.ShapeDtypeStruct((t, n), dt))
    feat = lambda n, dt: (pl.BlockSpec((None, n, tm), lambda i: (i // nst, 0, i % nst)),
                          jax.ShapeDtypeStruct((b, n, s), dt))
    state = lambda n: (pl.BlockSpec((None, None, n, tm), lambda i: (layer, i // nst, 0, i % nst)),
                       jax.ShapeDtypeStruct((depth, b, n, s), F32))
    outs = [tok(512, BF16), feat(1024, F32), feat(1024, BF16), tok(512, F32), tok(512, F32),
            tok(1024 if sample else 512, BF16),
            feat(512, F32), feat(256, F32), feat(512, BF16), tok(256, F32), feat(N_HEADS, F32), tok(LANES, F32)]
    if sample:
        outs += [tok(1024, F32), tok(512, BF16)]
    else:
        for k in STATE_OUTS:
            outs[k] = state(outs[k][1].shape[1])
    prev = () if prev is None else tuple(prev)
    return pl.pallas_call(
        functools.partial(_proj_body, sample=sample, n_prev=len(prev)),
        grid=(t // tm,),
        in_specs=[pl.BlockSpec((tm, d), lambda i: (i, 0))] + [_const_spec(a.shape) for a in pw]
                 + [pl.BlockSpec(memory_space=pl.ANY)] * len(prev),
        out_specs=[o[0] for o in outs],
        out_shape=[o[1] for o in outs],
        input_output_aliases={1 + len(pw) + k: STATE_OUTS[k] for k in range(len(prev))},
        compiler_params=_cparams("parallel"),
        name="in_proj_sample" if sample else "in_proj",
    )(x, *pw, *prev)


def _cumsum_body(x_ref, o_ref, *, blk):
    s = x_ref.shape[1]
    r = lax.broadcasted_iota(jnp.int32, (blk, blk), 0)
    c = lax.broadcasted_iota(jnp.int32, (blk, blk), 1)
    tri = (r <= c).astype(F32)
    carry = jnp.zeros((x_ref.shape[0], 1), F32)
    for i in range(s // blk):
        cs = _dot(x_ref[:, i * blk:(i + 1) * blk], tri, precision=HI) + carry
        o_ref[:, i * blk:(i + 1) * blk] = cs
        carry = cs[:, blk - 1:blk]


def _cumsum(x, layer):
    _, b, n, s = x.shape
    return pl.pallas_call(
        functools.partial(_cumsum_body, blk=min(256, s)),
        grid=(b,),
        in_specs=[pl.BlockSpec((None, None, n, s), lambda i: (layer, i, 0, 0))],
        out_specs=pl.BlockSpec((None, n, s), lambda i: (i, 0, 0)),
        out_shape=jax.ShapeDtypeStruct((b, n, s), F32),
        compiler_params=_cparams("parallel"),
        name="logf_cumsum",
    )(x)


def _fox_body(q_ref, k_ref, v_ref, c_ref, o_ref, *, tq, tk):
    qi = pl.program_id(2)
    q = q_ref[...]
    lo = lax.broadcasted_iota(jnp.int32, (tq, LANES), 1) < HEAD_DIM
    zero = jnp.zeros_like(q)
    qs = jnp.concatenate([jnp.where(lo, q, zero), jnp.where(lo, zero, q)], axis=0)
    qpos = qi * tq + lax.broadcasted_iota(jnp.int32, (tq, 1), 0)

    def block(kb, carry, masked):
        t0 = pl.multiple_of(kb * tk, tk)
        m_old, l_old, acc = carry
        bias = -c_ref[:, pl.ds(t0, tk)]
        s = _dot(qs, k_ref[:, pl.ds(t0, tk)]).reshape(2, tq, tk) + bias[:, None, :]
        if masked:
            kpos = t0 + lax.broadcasted_iota(jnp.int32, (1, tk), 1)
            s = jnp.where((kpos <= qpos)[None], s, NEG)
        s = s.reshape(2 * tq, tk)
        m_new = jnp.maximum(m_old, jnp.max(s, axis=-1, keepdims=True))
        alpha = jnp.exp(m_old - m_new)
        p = jnp.exp(s - m_new)
        l_new = alpha * l_old + jnp.sum(p, axis=-1, keepdims=True)
        return m_new, l_new, alpha * acc + _dot_nt(p.astype(BF16), v_ref[:, pl.ds(t0, tk)])

    n_kb = _shr(qi * tq + tq + tk - 1, tk)
    carry = (jnp.full((2 * tq, 1), NEG, F32), jnp.zeros((2 * tq, 1), F32), jnp.zeros((2 * tq, LANES), F32))
    carry = lax.fori_loop(0, n_kb - 1, lambda kb, c: block(kb, c, False), carry)
    _, l, acc = block(n_kb - 1, carry, True)
    o = acc / l
    o_ref[...] = jnp.where(lo, o[0:tq], o[tq:2 * tq]).astype(o_ref.dtype)


def _fox_prompt(fq, foxtb, c):
    b, s, _ = fq.shape
    tq = min(FOX_TQ, s)
    tk = min(ATT_TK, s)
    return pl.pallas_call(
        functools.partial(_fox_body, tq=tq, tk=tk),
        grid=(b, 4, s // tq),
        in_specs=[pl.BlockSpec((None, tq, LANES), lambda bi, p, qi: (bi, qi, p)),
                  pl.BlockSpec((None, LANES, s), lambda bi, p, qi: (bi, p, 0)),
                  pl.BlockSpec((None, LANES, s), lambda bi, p, qi: (bi, 4 + p, 0)),
                  pl.BlockSpec((None, None, 2, s), lambda bi, p, qi: (bi, p, 0, 0))],
        out_specs=pl.BlockSpec((None, tq, LANES), lambda bi, p, qi: (bi, qi, p)),
        out_shape=jax.ShapeDtypeStruct((b, s, 512), BF16),
        compiler_params=_cparams("parallel", "parallel", "arbitrary"),
        name="fox_prompt",
    )(fq, foxtb, foxtb, c)


def _gmlp_body(u_ref, v_ref, ws_ref, b_ref, o_ref):
    n = ws_ref.shape[1]
    r = lax.broadcasted_iota(jnp.int32, (n, n), 0)
    c = lax.broadcasted_iota(jnp.int32, (n, n), 1)
    lo = lax.broadcasted_iota(jnp.int32, (n, LANES), 1) < HEAD_DIM
    for j in range(4):
        sl = slice(j * LANES, (j + 1) * LANES)
        w0 = jnp.where(r >= c, ws_ref[2 * j], 0.0).astype(BF16)
        w1 = jnp.where(r >= c, ws_ref[2 * j + 1], 0.0).astype(BF16)
        for ch in range(u_ref.shape[0] // n):
            rs = slice(ch * n, (ch + 1) * n)
            vp = v_ref[rs, sl].astype(BF16)
            mixed = jnp.where(lo, _dot(w0, vp), _dot(w1, vp)) + b_ref[:, sl]
            o_ref[rs, sl] = (u_ref[rs, sl] * mixed).astype(o_ref.dtype)


def _gmlp_prompt(gmu, gmv, ws, bexp):
    t = gmu.shape[0]
    rows = GMLP_CHUNKS_PER_STEP * CHUNK if t % (GMLP_CHUNKS_PER_STEP * CHUNK) == 0 else CHUNK
    return pl.pallas_call(
        _gmlp_body,
        grid=(t // rows,),
        in_specs=[pl.BlockSpec((rows, 512), lambda i: (i, 0)),
                  pl.BlockSpec((rows, 512), lambda i: (i, 0)),
                  _const_spec(ws.shape), _const_spec(bexp.shape)],
        out_specs=pl.BlockSpec((rows, 512), lambda i: (i, 0)),
        out_shape=jax.ShapeDtypeStruct((t, 512), BF16),
        compiler_params=_cparams("parallel"),
        name="gmlp_prompt",
    )(gmu, gmv, ws, bexp)


def _gmlp_first_body(u_ref, v_ref, w_ref, b_ref, o_ref):
    o_ref[...] = (u_ref[...] * (v_ref[...] * w_ref[...] + b_ref[...])).astype(o_ref.dtype)


def _gmlp_sample(gmu, gmv, wrow, brow):
    n = gmu.shape[0]
    full = pl.BlockSpec((n, 512), lambda i: (0, 0))
    return pl.pallas_call(
        _gmlp_first_body,
        grid=(1,),
        in_specs=[full, full, _const_spec((1, 512)), _const_spec((1, 512))],
        out_specs=full,
        out_shape=jax.ShapeDtypeStruct((n, 512), BF16),
        compiler_params=_cparams("arbitrary"),
        name="gmlp_sample",
    )(gmu, gmv, wrow, brow)


def _compress_finish(xa, xb, wa_ref, wb_ref, w2_ref, kg0_ref):
    nhb = xa.shape[0]
    a = _dot(xa, wa_ref[...])
    b = _dot(xb, wb_ref[...])
    pre = a + pltpu.roll(b, nhb - 1, 0)
    out = _dot(_silu(pre).astype(BF16), w2_ref[...])
    kc = _headnorm_pair(out[:, 0:LANES], kg0_ref[...])
    return kc, out[:, LANES:2 * LANES]


def _compress(xk_ref, xv_ref, wa_ref, wb_ref, pea_ref, peb_ref, w2_ref, kg0_ref):
    nhb = xk_ref.shape[0] // CMP_STRIDE
    xa, xb = [], []
    for l in range(CMP_STRIDE):
        xl = jnp.concatenate([xk_ref[pl.ds(l, nhb, stride=CMP_STRIDE), :],
                              xv_ref[pl.ds(l, nhb, stride=CMP_STRIDE), :]], axis=1)
        xa.append((xl + pea_ref[l]).astype(BF16))
        xb.append((xl + peb_ref[l]).astype(BF16))
    return _compress_finish(jnp.concatenate(xa, axis=1), jnp.concatenate(xb, axis=1), wa_ref, wb_ref, w2_ref, kg0_ref)


def _compress_body(xk_ref, xv_ref, wa_ref, wb_ref, pea_ref, peb_ref, w2_ref, kg0_ref, pet_ref, o_ref):
    kc, vc = _compress(xk_ref, xv_ref, wa_ref, wb_ref, pea_ref, peb_ref, w2_ref, kg0_ref)
    o_ref[:, 0:LANES] = kc.astype(BF16)
    o_ref[:, LANES:2 * LANES] = vc.astype(BF16)


def _compress_prompt(cmpraw, cw):
    b, s, _ = cmpraw.shape
    nhb = s // CMP_STRIDE
    return pl.pallas_call(
        _compress_body,
        grid=(b,),
        in_specs=[pl.BlockSpec((None, s, LANES), lambda i: (i, 0, 0)),
                  pl.BlockSpec((None, s, LANES), lambda i: (i, 0, 1))] + [_const_spec(a.shape) for a in cw],
        out_specs=pl.BlockSpec((None, nhb, 256), lambda i: (i, 0, 0)),
        out_shape=jax.ShapeDtypeStruct((b, nhb, 256), BF16),
        compiler_params=_cparams("parallel"),
        name="nsa_compress",
    )(cmpraw, cmpraw, *cw)


def _slope_col(head):
    out = jnp.zeros(head.shape, F32)
    for hh in range(N_HEADS):
        out = jnp.where(head == hh, 2.0 ** (-(hh + 1)), out)
    return out


def _slope_rows(g):
    return jnp.concatenate([jnp.full((Q_BLOCK, 1), 2.0 ** (-(N_REP * g + r + 1)), F32) for r in range(N_REP)], axis=0)


def _cmp_softmax(s, mask):
    s = jnp.where(mask, s, NEG)
    m = jnp.max(s, axis=-1, keepdims=True)
    e = jnp.where(mask, jnp.exp(s - m), 0.0)
    l = jnp.sum(e, axis=-1, keepdims=True)
    return e / jnp.where(l > 0.0, l, 1.0)


def _select_blocks(imp, cur, n_blk):
    m, n = imp.shape
    pool = (_shr(lax.broadcasted_iota(jnp.int32, (n, LANES), 0), SEL_BLOCK // CMP_STRIDE)
            == lax.broadcasted_iota(jnp.int32, (n, LANES), 1)).astype(F32)
    score = _dot(imp, pool, precision=HI)
    j = lax.broadcasted_iota(jnp.int32, (m, LANES), 1)
    forced = (j == 0) | (j == cur) | (j == cur - 1)
    score = jnp.where(forced, score + FORCE_BONUS, score)
    score = jnp.where(j > cur, NEG, score)
    rank = jnp.zeros((m, LANES), F32)
    for i in range(n_blk):
        ci = score[:, i:i + 1]
        ahead = (ci > score) | ((ci == score) & (j > i))
        rank = rank + ahead.astype(F32)
    return (rank < float(min(N_SEL, n_blk))) & (score > 0.5 * NEG)


def _expand_sel(sel_bf, t0, n):
    jj = lax.broadcasted_iota(jnp.int32, (LANES, n), 0)
    tt = t0 + lax.broadcasted_iota(jnp.int32, (LANES, n), 1)
    e = (_shr(tt, SEL_BLOCK) == jj).astype(BF16)
    return _dot(sel_bf, e)


AUG_SEL = 16
MAX_SEL_BLOCKS = 32


def _aug_rows(pos):
    n = pos.shape[1]
    r = lax.broadcasted_iota(jnp.int32, (HEAD_DIM, n), 0)
    onehot = (r >= AUG_SEL) & (_shr(pos, SEL_BLOCK) == r - AUG_SEL)
    return jnp.where(r == 0, _shr(pos, LANES).astype(F32),
                     jnp.where(r == 1, (pos & (LANES - 1)).astype(F32), onehot.astype(F32)))


def _select_bias_t(imp, qi, off, n_blk):
    n = imp.shape[1]
    base = off + AUG_SEL
    prow = lax.broadcasted_iota(jnp.int32, (LANES, n), 0) - base
    pool_t = (prow == _shr(lax.broadcasted_iota(jnp.int32, (LANES, n), 1), SEL_BLOCK // CMP_STRIDE)).astype(F32)
    score = lax.dot_general(pool_t, imp, (((1,), (1,)), ((), ())), preferred_element_type=F32,
                            precision=HI)[base:base + MAX_SEL_BLOCKS]
    j = lax.broadcasted_iota(jnp.int32, (MAX_SEL_BLOCKS, 1), 0)
    cur = _shr(qi * Q_BLOCK + lax.broadcasted_iota(jnp.int32, (1, Q_BLOCK), 1), SEL_BLOCK)
    forced = (j == 0) | (j == cur) | (j == cur - 1)
    score = jnp.where(forced, score + FORCE_BONUS, score)
    score = jnp.where(j > cur, NEG, score)
    rank = jnp.zeros(score.shape, F32)
    for i in range(n_blk):
        ci = score[i:i + 1, :]
        ahead = (ci > score) | ((ci == score) & (j > i))
        rank = rank + ahead.astype(F32)
    chosen = (rank < float(min(N_SEL, n_blk))) & (score > 0.5 * NEG)
    bias_t = jnp.where(chosen, 0.0, NEG)
    parts = [bias_t, jnp.zeros((LANES - base - MAX_SEL_BLOCKS, Q_BLOCK), F32)]
    if base:
        parts = [jnp.zeros((base, Q_BLOCK), F32)] + parts
    return jnp.concatenate(parts, axis=0).T


def _nsa_body(nq_ref, cmp_ref, kv_ref, aug_ref, gate_ref, o_ref, *, n_blk, tk, wlen):
    qi = pl.program_id(1)
    s_len = kv_ref.shape[1]
    rows = N_REP * Q_BLOCK
    row = lax.broadcasted_iota(jnp.int32, (rows, 1), 0)
    qpos = qi * Q_BLOCK + (row & (Q_BLOCK - 1))
    qpos_q = qi * Q_BLOCK + lax.broadcasted_iota(jnp.int32, (Q_BLOCK, 1), 0)
    lane = lax.broadcasted_iota(jnp.int32, (1, LANES), 1)
    nhb = cmp_ref.shape[0]
    end_col = lax.broadcasted_iota(jnp.int32, (nhb, 1), 0) * CMP_STRIDE + (CMP_LEN - 1)
    end = lax.broadcasted_iota(jnp.int32, (1, nhb), 1) * CMP_STRIDE + (CMP_LEN - 1)

    n_kb = _shr(qi * Q_BLOCK + Q_BLOCK + tk - 1, tk)
    t_last = pl.multiple_of((n_kb - 1) * tk, tk)
    causal = jnp.where(t_last + lax.broadcasted_iota(jnp.int32, (1, tk), 1) <= qpos_q, 0.0, NEG)
    t_win = pl.multiple_of(jnp.clip((qi - WINDOW // Q_BLOCK) * Q_BLOCK, 0, s_len - wlen), Q_BLOCK)
    d_win = qpos_q - (t_win + lax.broadcasted_iota(jnp.int32, (1, wlen), 1))
    band = jnp.where((d_win >= 0) & (d_win < WINDOW), 0.0, NEG)

    def masked(s, bias):
        n = s.shape[1]
        return (s.reshape(N_REP, Q_BLOCK, n) + bias[None]).reshape(rows, n)

    for g in range(N_GROUPS):
        off = HEAD_DIM * (1 - g)
        in_g = _shr(lane, HEAD_DIM) == g
        tiles = []
        for j in range(N_REP // 2):
            pair = nq_ref[:, (2 * g + j) * LANES:(2 * g + j + 1) * LANES].astype(F32)
            moved = pltpu.roll(pair, HEAD_DIM, 1)
            first, second = (pair, moved) if g == 0 else (moved, pair)
            tiles += [jnp.where(in_g, first, 0.0), jnp.where(in_g, second, 0.0)]
        q = jnp.concatenate(tiles, axis=0)
        slope = _slope_rows(g)
        q_alibi = q + jnp.where(lane == off, slope * float(LANES), 0.0) + jnp.where(lane == off + 1, slope, 0.0)
        q_win = q_alibi.astype(BF16)

        def keys(lo, t0, n):
            k = kv_ref[lo + g * HEAD_DIM:lo + (g + 1) * HEAD_DIM, pl.ds(t0, n)]
            a = aug_ref[:, pl.ds(t0, n)]
            return jnp.concatenate([k, a] if g == 0 else [a, k], axis=0)

        kc = cmp_ref[:, 0:LANES].astype(F32)
        kc = jnp.where(in_g, kc, jnp.where(lane == off, _shr(end_col, LANES).astype(F32),
                                           jnp.where(lane == off + 1, (end_col & (LANES - 1)).astype(F32), 0.0)))
        p = _cmp_softmax(_dot_nt(q_win, kc.astype(BF16)), end <= qpos)
        o_cmp = _dot(p.astype(BF16), cmp_ref[:, LANES:2 * LANES])
        imp = p[0:Q_BLOCK]
        for r in range(1, N_REP):
            imp = imp + p[r * Q_BLOCK:(r + 1) * Q_BLOCK]
        sel_bias = _select_bias_t(imp, qi, off, n_blk)
        q_sel = (q_alibi + jnp.concatenate([sel_bias] * N_REP, axis=0)).astype(BF16)

        def sel_tile(t0, carry, last):
            m_old, l_old, acc = carry
            s = _dot(q_sel, keys(0, t0, tk))
            if last:
                s = masked(s, causal)
            v = kv_ref[LANES:2 * LANES, pl.ds(t0, tk)]
            m_new = jnp.maximum(m_old, jnp.max(s, axis=-1, keepdims=True))
            alpha = jnp.exp(m_old - m_new)
            p = jnp.exp(s - m_new)
            l_new = alpha * l_old + jnp.sum(p, axis=-1, keepdims=True)
            return m_new, l_new, alpha * acc + _dot_nt(p.astype(BF16), v)

        carry = (jnp.full((rows, 1), NEG, F32), jnp.zeros((rows, 1), F32), jnp.zeros((rows, LANES), F32))
        carry = lax.fori_loop(0, n_kb - 1, lambda kb, c: sel_tile(pl.multiple_of(kb * tk, tk), c, False), carry)
        _, l_sel, acc_sel = sel_tile(t_last, carry, True)
        o_sel = acc_sel / l_sel

        s = masked(_dot(q_win, keys(2 * LANES, t_win, wlen)), band)
        p = jnp.exp(s - jnp.max(s, axis=-1, keepdims=True))
        v = kv_ref[3 * LANES:4 * LANES, pl.ds(t_win, wlen)]
        o_win = _dot_nt(p.astype(BF16), v) / jnp.sum(p, axis=-1, keepdims=True)

        def gated(r):
            hh = N_REP * g + r
            rs = slice(r * Q_BLOCK, (r + 1) * Q_BLOCK)
            return (gate_ref[:, 8 + hh:9 + hh] * o_cmp[rs] + gate_ref[:, 16 + hh:17 + hh] * o_sel[rs]
                    + gate_ref[:, 24 + hh:25 + hh] * o_win[rs])

        lo = lane < HEAD_DIM
        for j in range(N_REP // 2):
            first, second = gated(2 * j), gated(2 * j + 1)
            if g == 0:
                pair = jnp.where(lo, first, pltpu.roll(second, HEAD_DIM, 1))
            else:
                pair = jnp.where(lo, pltpu.roll(first, HEAD_DIM, 1), second)
            o_ref[:, (2 * g + j) * LANES:(2 * g + j + 1) * LANES] = pair.astype(o_ref.dtype)


def _nsa_aug_body(o_ref):
    o_ref[...] = _aug_rows(lax.broadcasted_iota(jnp.int32, (1, o_ref.shape[1]), 1)).astype(o_ref.dtype)


def _nsa_aug(s):
    return pl.pallas_call(
        _nsa_aug_body,
        out_shape=jax.ShapeDtypeStruct((HEAD_DIM, s), BF16),
        name="nsa_aug",
    )()


def _nsa_prompt(nq, cmpkv, nsatb, aug, gates):
    b, s, w = nq.shape
    nhb = cmpkv.shape[1]
    n_blk = s // SEL_BLOCK
    assert n_blk <= MAX_SEL_BLOCKS and s // LANES <= 256
    return pl.pallas_call(
        functools.partial(_nsa_body, n_blk=n_blk, tk=min(ATT_TK, s), wlen=min(WINDOW + Q_BLOCK, s)),
        grid=(b, s // Q_BLOCK),
        in_specs=[pl.BlockSpec((None, Q_BLOCK, w), lambda bi, qi: (bi, qi, 0)),
                  pl.BlockSpec((None, nhb, 256), lambda bi, qi: (bi, 0, 0)),
                  pl.BlockSpec((None, 512, s), lambda bi, qi: (bi, 0, 0)),
                  pl.BlockSpec((HEAD_DIM, s), lambda bi, qi: (0, 0)),
                  pl.BlockSpec((None, Q_BLOCK, LANES), lambda bi, qi: (bi, qi, 0))],
        out_specs=pl.BlockSpec((None, Q_BLOCK, w), lambda bi, qi: (bi, qi, 0)),
        out_shape=jax.ShapeDtypeStruct((b, s, w), BF16),
        compiler_params=_cparams("parallel", "arbitrary"),
        name="nsa_prompt",
    )(nq, cmpkv, nsatb, aug, gates)


def _merge_body(x_ref, g_ref, of_ref, og_ref, on_ref, wg_ref, wb0_ref, wb1_ref, wb2_ref, wo_ref, o_ref):
    x = x_ref[...]
    d = x.shape[1]
    h = _rms_rows(x, g_ref[...]).astype(BF16)
    y = jax.nn.sigmoid(_dot(h, wg_ref[:, 0:d])) * _dot(of_ref[...], wb0_ref[...])
    y = y + jax.nn.sigmoid(_dot(h, wg_ref[:, d:2 * d])) * _dot(og_ref[...], wb1_ref[...])
    y = y + jax.nn.sigmoid(_dot(h, wg_ref[:, 2 * d:3 * d])) * _dot(on_ref[...], wb2_ref[...])
    o_ref[...] = x + _dot(y.astype(BF16), wo_ref[...])


def _merge(x, g, o_fox, o_gm, o_nsa, wg, wb, wb2, wo, layer):
    t, d = x.shape
    tm = _token_tile(t)
    row = lambda n: pl.BlockSpec((tm, n), lambda i: (i, 0))
    padded = o_nsa.shape[1] != wb.shape[1]
    return pl.pallas_call(
        _merge_body,
        grid=(t // tm,),
        in_specs=[row(d), _const_spec((1, d)), row(512), row(512), row(o_nsa.shape[1]), _const_spec(wg.shape),
                  _slab_spec(wb, 3 * layer), _slab_spec(wb, 3 * layer + 1),
                  _const_spec(wb2.shape) if padded else _slab_spec(wb, 3 * layer + 2), _slab_spec(wo, layer)],
        out_specs=row(d),
        out_shape=jax.ShapeDtypeStruct((t, d), F32),
        compiler_params=_cparams("parallel"),
        name="merge",
    )(x, g, o_fox, o_gm, o_nsa, wg, wb, wb, wb2 if padded else wb, wo)


def _column(x, idx):
    lane = lax.broadcasted_iota(jnp.int32, x.shape, 1)
    return jnp.sum(jnp.where(lane == idx, x, 0.0), axis=1, keepdims=True)


def _logf_suffix_body(x_ref, o_ref):
    n, h, t = x_ref.shape
    tri = (lax.broadcasted_iota(jnp.int32, (t, t), 0) >= lax.broadcasted_iota(jnp.int32, (t, t), 1)).astype(F32)
    o_ref[...] = _dot(x_ref[...].reshape(n * h, t), tri, precision=HI).reshape(n, h, t)


def _logf_suffix(cache_lf, layer):
    n_pool = cache_lf.shape[1]
    blk = next(c for c in (256, 128, 64, 32, 16, 8, 4, 2, 1) if n_pool % c == 0)
    return pl.pallas_call(
        _logf_suffix_body,
        grid=(n_pool // blk,),
        in_specs=[pl.BlockSpec((None, blk, N_HEADS, PAGE), lambda i: (layer, i, 0, 0))],
        out_specs=pl.BlockSpec((blk, N_HEADS, PAGE), lambda i: (i, 0, 0)),
        out_shape=jax.ShapeDtypeStruct((n_pool, N_HEADS, PAGE), F32),
        compiler_params=_cparams("parallel"),
        name="logf_suffix",
    )(cache_lf)


def _fox_decode_body(pt_ref, *refs, n_pages):
    kv = refs[:n_pages]
    q_ref, new_ref, lfn_ref, sfx_ref, o_ref, s_ref = refs[n_pages:]
    b = pl.program_id(0)
    w = N_HEADS * HEAD_DIM
    row = lax.broadcasted_iota(jnp.int32, (N_HEADS, w), 0)
    lane = lax.broadcasted_iota(jnp.int32, (N_HEADS, w), 1)
    diag = _shr(lane, HEAD_DIM) == row
    qbd = jnp.where(diag, jnp.broadcast_to(q_ref[...].astype(F32), (N_HEADS, w)), 0.0).astype(BF16)
    last = lax.broadcasted_iota(jnp.int32, (N_HEADS, PAGE), 1) == PAGE - 1

    later = _column(lfn_ref[...], b)
    for i in reversed(range(n_pages)):
        incl = sfx_ref[pt_ref[b, i]]
        after = jnp.where(last, 0.0, pltpu.roll(incl, PAGE - 1, 1))
        s_ref[:, i * PAGE:(i + 1) * PAGE] = _dot(qbd, kv[i][0].astype(BF16)) + after + later
        later = later + incl[:, 0:1]

    k_new = new_ref[:, 0:w].astype(BF16).astype(F32)
    v_new = new_ref[:, w:2 * w].astype(BF16).astype(F32)
    s_new = jnp.sum(qbd.astype(F32) * k_new, axis=-1, keepdims=True)
    m = jnp.maximum(jnp.max(s_ref[...], axis=-1, keepdims=True), s_new)
    p_new = jnp.exp(s_new - m)
    l = p_new
    o = p_new.astype(BF16).astype(F32) * v_new
    for i in range(n_pages):
        p = jnp.exp(s_ref[:, i * PAGE:(i + 1) * PAGE] - m)
        l = l + jnp.sum(p, axis=-1, keepdims=True)
        o = o + _dot_nt(p.astype(BF16), kv[i][1].astype(BF16))
    o_ref[...] = jnp.sum(jnp.where(diag, o / l, 0.0), axis=0, keepdims=True).astype(o_ref.dtype)


def _fox_decode(page_table, cache_kv, lf_suffix, layer, fq_s, foxkv_s, lft_s):
    db, n_pages = page_table.shape
    w = N_HEADS * HEAD_DIM
    grid_spec = pltpu.PrefetchScalarGridSpec(
        num_scalar_prefetch=1,
        grid=(db,),
        in_specs=[pl.BlockSpec((None, None, 2, w, PAGE), lambda b, pt, i=i: (layer, pt[b, i], 0, 0, 0))
                  for i in range(n_pages)]
                 + [pl.BlockSpec((None, 1, w), lambda b, pt: (b, 0, 0)),
                    pl.BlockSpec((None, 1, 2 * w), lambda b, pt: (b, 0, 0)),
                    pl.BlockSpec(lft_s.shape, lambda b, pt: (0, 0)),
                    pl.BlockSpec(lf_suffix.shape, lambda b, pt: (0, 0, 0), pipeline_mode=pl.Buffered(1))],
        out_specs=pl.BlockSpec((None, 1, w), lambda b, pt: (b, 0, 0)),
        scratch_shapes=[pltpu.VMEM((N_HEADS, n_pages * PAGE), F32)],
    )
    return pl.pallas_call(
        functools.partial(_fox_decode_body, n_pages=n_pages),
        grid_spec=grid_spec,
        out_shape=jax.ShapeDtypeStruct((db, 1, w), BF16),
        compiler_params=_cparams("arbitrary"),
        name="fox_decode",
    )(page_table, *([cache_kv] * n_pages), fq_s, foxkv_s, lft_s, lf_suffix)


def _nsa_decode_body(pt_ref, *refs, n_pages, rps, n_prev):
    pg = [refs[r * n_pages:(r + 1) * n_pages] for r in range(rps)]
    (win_ref, q_ref, new_ref, gate_ref, wnew_ref, wa_ref, wb_ref, pea_ref, peb_ref, w2_ref, kg0_ref,
     pet_ref) = refs[rps * n_pages:rps * n_pages + 12]
    o_ref, wout_ref, xa_ref, xb_ref, s_ref = refs[rps * n_pages + 12 + n_prev:]
    past = n_pages * PAGE
    nhb = past // CMP_STRIDE
    wbuf = win_ref.shape[3]
    cur = past // SEL_BLOCK
    n_blk = cur + 1
    m = rps * N_HEADS

    def per_row(fn):
        return jnp.concatenate([fn(r) for r in range(rps)], axis=0)

    tp = lax.broadcasted_iota(jnp.int32, (2 * PAGE, 2 * PAGE), 0)
    r16 = tp & (CMP_STRIDE - 1)
    src = _shr(r16, 8) * PAGE + (r16 & 7) * CMP_STRIDE + _shr(tp, CMP_STRIDE)
    perm = (lax.broadcasted_iota(jnp.int32, (2 * PAGE, 2 * PAGE), 1) == src).astype(BF16)
    for r in range(rps):
        for j in range(n_pages // 2):
            pkv = jnp.concatenate([jnp.concatenate([pg[r][2 * j + pp][0], pg[r][2 * j + pp][1]], axis=0)
                                   for pp in range(2)], axis=1)
            for half, x_ref in enumerate((xa_ref, xb_ref)):
                rows = _dot_nt(perm, (pkv + pet_ref[half]).astype(BF16)).astype(BF16)
                for l in range(CMP_STRIDE):
                    x_ref[r * nhb + 16 * j:r * nhb + 16 * (j + 1), 256 * l:256 * (l + 1)] = rows[16 * l:16 * (l + 1), :]
    kc, vc = _compress_finish(xa_ref[...], xb_ref[...], wa_ref, wb_ref, w2_ref, kg0_ref)
    kc = kc.astype(BF16)
    vc = vc.astype(BF16)

    qf = q_ref[...].astype(F32)
    q = [qf[r * N_HEADS:(r + 1) * N_HEADS].astype(BF16) for r in range(rps)]
    row = lax.broadcasted_iota(jnp.int32, (m, 1), 0)
    head = row & (N_HEADS - 1)
    slope = _slope_col(head)
    lane = lax.broadcasted_iota(jnp.int32, (1, LANES), 1)

    def rows_of(x, r):
        return x[r * N_HEADS:(r + 1) * N_HEADS]

    def new_rows(lo):
        return per_row(lambda r: jnp.broadcast_to(new_ref[r][:, lo:lo + LANES].astype(F32), (N_HEADS, LANES)))

    end = lax.broadcasted_iota(jnp.int32, (1, nhb), 1) * CMP_STRIDE + (CMP_LEN - 1)
    s = per_row(lambda r: _dot_nt(q[r], kc[r * nhb:(r + 1) * nhb])) - slope * (past - end).astype(F32)
    p = _cmp_softmax(s, jnp.broadcast_to(end <= past, s.shape))
    o_cmp = per_row(lambda r: _dot(rows_of(p, r).astype(BF16), vc[r * nhb:(r + 1) * nhb]))
    same = (_shr(lax.broadcasted_iota(jnp.int32, (m, m), 0), N_REP)
            == _shr(lax.broadcasted_iota(jnp.int32, (m, m), 1), N_REP)).astype(F32)
    imp = _dot(same, p, precision=HI)
    sel = _select_blocks(imp, jnp.full((m, 1), cur, jnp.int32), n_blk)

    for r in range(rps):
        for i in range(n_pages):
            s_ref[r * N_HEADS:(r + 1) * N_HEADS, i * PAGE:(i + 1) * PAGE] = _dot(q[r], pg[r][i][2].astype(BF16))
    kpos = lax.broadcasted_iota(jnp.int32, (1, past), 1)
    s = s_ref[...] - slope * (past - kpos).astype(F32)
    s = jnp.where(_expand_sel(sel.astype(BF16), 0, past) > 0.5, s, NEG)
    new_ok = jnp.sum(jnp.where(lane == cur, sel.astype(F32), 0.0), axis=-1, keepdims=True) > 0.5
    s_new = jnp.where(new_ok, jnp.sum(qf * new_rows(0), axis=-1, keepdims=True), NEG)
    mx = jnp.maximum(jnp.max(s, axis=-1, keepdims=True), s_new)
    p_new = jnp.where(new_ok, jnp.exp(s_new - mx), 0.0)
    p = jnp.where(s > 0.5 * NEG, jnp.exp(s - mx), 0.0)
    l = jnp.sum(p, axis=-1, keepdims=True) + p_new
    s_ref[...] = p

    def sel_pv(r):
        acc = jnp.zeros((N_HEADS, LANES), F32)
        for i in range(n_pages):
            pr = s_ref[r * N_HEADS:(r + 1) * N_HEADS, i * PAGE:(i + 1) * PAGE]
            acc = acc + _dot_nt(pr.astype(BF16), pg[r][i][3].astype(BF16))
        return acc

    o_sel = (per_row(sel_pv) + p_new.astype(BF16).astype(F32) * new_rows(LANES)) / jnp.where(l > 0.0, l, 1.0)

    pos = lax.broadcasted_iota(jnp.int32, (1, wbuf), 1)
    d = wbuf - pos
    s = per_row(lambda r: _dot(q[r], win_ref[r, 0].astype(BF16))) - slope * d.astype(F32)
    s = jnp.where(d < WINDOW, s, NEG)
    s_new = jnp.sum(qf * new_rows(2 * LANES), axis=-1, keepdims=True)
    mx = jnp.maximum(jnp.max(s, axis=-1, keepdims=True), s_new)
    p = jnp.exp(s - mx)
    p_new = jnp.exp(s_new - mx)
    l = jnp.sum(p, axis=-1, keepdims=True) + p_new
    o_win = (per_row(lambda r: _dot_nt(rows_of(p, r).astype(BF16), win_ref[r, 1].astype(BF16)))
             + p_new.astype(BF16).astype(F32) * new_rows(3 * LANES)) / l

    o = gate_ref[:, 0:1] * o_cmp + gate_ref[:, 1:2] * o_sel + gate_ref[:, 2:3] * o_win
    in_group = _shr(lane, HEAD_DIM) == _shr(head, N_REP)
    o_ref[...] = jnp.where(in_group, o, 0.0).astype(o_ref.dtype)

    for r in range(rps):
        bidx = pl.program_id(0) * rps + r
        for kv in range(2):
            wout_ref[r, kv] = jnp.where(pos == wbuf - 1, _column(wnew_ref[kv * LANES:(kv + 1) * LANES, :], bidx),
                                        pltpu.roll(win_ref[r, kv], wbuf - 1, 1))


def _nsa_decode(page_table, cache_nsa, win_state, layer, nq_s, nsab_s, gates_s, wint_s, cw, prev_win=None):
    db, n_pages = page_table.shape
    depth, wbuf = win_state.shape[0], win_state.shape[4]
    prev = () if prev_win is None else (prev_win,)
    rps = DECODE_ROWS
    assert db % rps == 0 and n_pages % 2 == 0
    m = rps * N_HEADS
    nhb = n_pages * PAGE // CMP_STRIDE
    in_specs = [pl.BlockSpec((None, None, 4, LANES, PAGE), lambda b, pt, i=i, r=r: (layer, pt[b * rps + r, i], 0, 0, 0))
                for r in range(rps) for i in range(n_pages)]
    in_specs += [pl.BlockSpec((None, rps, 2, LANES, wbuf), lambda b, pt: (layer, b, 0, 0, 0)),
                 pl.BlockSpec((m, LANES), lambda b, pt: (b, 0)),
                 pl.BlockSpec((rps, 1, 512), lambda b, pt: (b, 0, 0)),
                 pl.BlockSpec((m, 3), lambda b, pt: (b, 0)),
                 pl.BlockSpec(wint_s.shape, lambda b, pt: (0, 0))]
    in_specs += [pl.BlockSpec(a.shape, lambda b, pt, nd=a.ndim: (0,) * nd, pipeline_mode=pl.Buffered(1)) for a in cw]
    in_specs += [pl.BlockSpec(memory_space=pl.ANY)] * len(prev)
    grid_spec = pltpu.PrefetchScalarGridSpec(
        num_scalar_prefetch=1,
        grid=(db // rps,),
        in_specs=in_specs,
        out_specs=[pl.BlockSpec((m, LANES), lambda b, pt: (b, 0)),
                   pl.BlockSpec((None, rps, 2, LANES, wbuf), lambda b, pt: (layer, b, 0, 0, 0))],
        scratch_shapes=[pltpu.VMEM((rps * nhb, CMP_STRIDE * 256), BF16),
                        pltpu.VMEM((rps * nhb, CMP_STRIDE * 256), BF16),
                        pltpu.VMEM((m, n_pages * PAGE), F32)],
    )
    return pl.pallas_call(
        functools.partial(_nsa_decode_body, n_pages=n_pages, rps=rps, n_prev=len(prev)),
        grid_spec=grid_spec,
        out_shape=[jax.ShapeDtypeStruct((db * N_HEADS, LANES), BF16),
                   jax.ShapeDtypeStruct((depth, db, 2, LANES, wbuf), F32)],
        input_output_aliases={1 + rps * n_pages + 5 + len(cw): 1} if prev else {},
        compiler_params=_cparams("arbitrary"),
        name="nsa_decode",
    )(page_table, *([cache_nsa] * (rps * n_pages)), win_state, nq_s, nsab_s, gates_s, wint_s, *cw, *prev)


O_F, O_GM, O_NQ, O_NKV, O_NG, O_MG = 1536, 1544, 2568, 3080, 3848, 3872
_HEAD_PLACE = (np.arange(N_HEADS)[:, None] // N_REP == np.arange(N_GROUPS)[None, :]).astype(np.float32)


def _prep_w_in(w):
    d = w.shape[0]
    wq = w[:, O_NQ:O_NKV].reshape(d, N_HEADS, 1, HEAD_DIM)
    wq = (wq * _HEAD_PLACE[None, :, :, None]).reshape(d, N_HEADS * LANES)
    small = jnp.concatenate([w[:, O_F:O_GM], w[:, O_NG:O_MG], jnp.zeros((d, LANES - 32), w.dtype)], axis=1)
    w_tok = jnp.concatenate([w[:, 0:O_F], w[:, O_GM:O_NQ], wq, w[:, O_NKV:O_NG], small, w[:, O_NQ:O_NKV]],
                            axis=1).astype(BF16)
    wt = w.T
    w_feat = jnp.concatenate([wt[512:O_F], wt[O_NKV:O_NG], wt[O_F:O_GM], jnp.zeros((8, d), w.dtype)],
                             axis=0).astype(BF16)
    return w_tok, w_feat, w[:, O_MG:].astype(BF16)


def _pad_heads(g):
    return (_HEAD_PLACE[:, :, None] * g[None, None, :]).reshape(1, N_HEADS * LANES)


def _prep_compress(w1, w2, pe, kg0):
    w1r = w1.reshape(2, 2, CMP_STRIDE, HEAD_DIM, HEAD_DIM).astype(BF16)

    def block_diag(blocks):
        z = jnp.zeros_like(blocks[0])
        return jnp.concatenate([jnp.concatenate([blk if i == j else z for j in range(4)], axis=-1)
                                for i, blk in enumerate(blocks)], axis=-2)

    wl = block_diag([w1r[c] for c in range(2) for _ in range(N_GROUPS)]).reshape(2, CMP_STRIDE * 256, 256)
    w2bd = block_diag([w2[c] for c in range(2) for _ in range(N_GROUPS)])
    pex = jnp.broadcast_to(pe.transpose(1, 0, 2)[:, :, None, :], (CMP_LEN, 2, N_GROUPS, HEAD_DIM))
    pex = pex.reshape(2, CMP_STRIDE, 1, 256)
    pet = jnp.tile(pex.reshape(2, CMP_STRIDE, 256).transpose(0, 2, 1), (1, 1, 2 * PAGE // CMP_STRIDE))
    return (wl[0], wl[1], pex[0], pex[1], w2bd.astype(BF16), jnp.tile(kg0, 2)[None, :], pet)


def _layer(xp, xs, lw, page_table, caches, layer, depth, prev, dims):
    b, s, db = dims

    xp = _ffn(xp, *lw['ffn1'], layer)
    xs = _ffn(xs, *lw['ffn1'], layer)

    pw = lw['proj']
    (fq, foxt, foxtb, gmu, gmv, nq, nsat, wint, nsatb, cmpraw, lft, gates) = _proj(
        xp, pw, b, s, False, layer, depth, None if prev is None else prev[:4])
    (fq_s, foxt_s, _, gmu_s, gmv_s, nq_s, nsat_s, wint_s, _, _, lft_s, gates_s,
     foxkv_s, nsab_s) = _proj(xs, pw, 1, db, True)

    c = _cumsum(lft, layer).reshape(b, 4, 2, s)
    o_fox_p = _fox_prompt(fq.reshape(b, s, 512), foxtb, c)
    o_gm_p = _gmlp_prompt(gmu, gmv, lw['gmlp_w_s'], lw['gmlp_bexp'])
    cmpkv = _compress_prompt(cmpraw.reshape(b, s, 256), lw['cw'])
    o_nsa_p = _nsa_prompt(nq.reshape(b, s, 512), cmpkv, nsatb, lw['nsa_aug'], gates.reshape(b, s, LANES))
    xp = _merge(xp, lw['mix_norm'], o_fox_p.reshape(b * s, 512), o_gm_p, o_nsa_p.reshape(b * s, 512),
                lw['w_gate'], lw['wb'], lw['wb2'], lw['w_out'], layer)

    cache_fox_kv, cache_fox_lf, cache_nsa, win_state = caches
    o_fox_s = _fox_decode(page_table, cache_fox_kv, _logf_suffix(cache_fox_lf, layer), layer, fq_s.reshape(db, 1, 512),
                          foxkv_s.reshape(db, 1, 1024), lft_s.reshape(N_HEADS, db))
    o_gm_s = _gmlp_sample(gmu_s, gmv_s, lw['gmlp_w00'], lw['gmlp_b0'])
    g3 = gates_s[:, 8:32].reshape(db, 3, N_HEADS).transpose(0, 2, 1).reshape(db * N_HEADS, 3)
    o_nsa_s, win_next = _nsa_decode(page_table, cache_nsa, win_state, layer, nq_s.reshape(db * N_HEADS, LANES),
                                    nsab_s.reshape(db, 1, 512), g3, wint_s.reshape(256, db), lw['cw'],
                                    None if prev is None else prev[4])
    xs = _merge(xs, lw['mix_norm'], o_fox_s.reshape(db, 512), o_gm_s, o_nsa_s.reshape(db, 1024),
                lw['w_gate'], lw['wb'], lw['wb2'], lw['w_out'], layer)

    xp = _ffn(xp, *lw['ffn2'], layer)
    xs = _ffn(xs, *lw['ffn2'], layer)

    state = dict(
        fox_kv_s=foxt_s, fox_logf_s=lft_s, nsa_kv_s=nsat_s,
        gmlp_v_p=gmv.reshape(b, s, 512)[:, s - min(CHUNK, s):], gmlp_v_s=gmv_s.reshape(db, 1, 512),
    )
    return xp, xs, state, (foxt, nsat, wint, lft, win_next)


def kernel(x_prompt, x_sample, page_table, cache_fox_kv, cache_fox_logf, cache_nsa_kv, state_nsa_win, ffn1_norm, ffn1_w_gu, ffn1_w_down, mix_norm, w_in, fox_b_f, fox_qk_gain, gmlp_v_gain, gmlp_w_s, gmlp_b_s, nsa_q_gain, nsa_k_gain, nsa_cmp_pe, nsa_cmp_w1, nsa_cmp_w2, w_branch, w_out, ffn2_norm, ffn2_w_gu, ffn2_w_down):
    b, s, d = x_prompt.shape
    db = x_sample.shape[0]
    depth = w_in.shape[0]
    n_pool = cache_fox_kv.shape[1]
    wb = state_nsa_win.shape[2]
    xp = x_prompt.reshape(b * s, d)
    xs = x_sample.reshape(db, d)

    caches = (cache_fox_kv.transpose(0, 1, 3, 4, 5, 2).reshape(depth, n_pool, 2, N_HEADS * HEAD_DIM, PAGE),
              cache_fox_logf.transpose(0, 1, 3, 2),
              cache_nsa_kv.transpose(0, 1, 3, 4, 5, 2).reshape(depth, n_pool, 4, LANES, PAGE),
              state_nsa_win.transpose(0, 1, 3, 4, 5, 2).reshape(depth, db, 2, LANES, wb))

    nsa_rows = (np.arange(N_HEADS * LANES) % LANES) // HEAD_DIM == (np.arange(N_HEADS * LANES) // LANES) // N_REP
    col = lambda g: g[:, None]
    ffn1 = (ffn1_norm[:, None, :], ffn1_w_gu.astype(BF16), ffn1_w_down.astype(BF16))
    ffn2 = (ffn2_norm[:, None, :], ffn2_w_gu.astype(BF16), ffn2_w_down.astype(BF16))
    wb_all = w_branch.astype(BF16).reshape(depth * 3, w_branch.shape[2], d)
    wo_all = w_out.astype(BF16)
    nsa_aug = _nsa_aug(s)
    states = []
    stacked = None
    for l in range(depth):
        w_tok, w_feat, w_gate = _prep_w_in(w_in[l])
        wb2 = jnp.zeros((N_HEADS * LANES, d), F32).at[np.nonzero(nsa_rows)[0]].set(w_branch[l, 2])
        proj = (mix_norm[l][None, :], w_tok, w_feat,
                jnp.tile(fox_qk_gain[l, 0], 2)[None, :], jnp.tile(fox_qk_gain[l, 1], 2)[None, :],
                col(fox_qk_gain[l, 1]), gmlp_v_gain[l][None, :], _pad_heads(nsa_q_gain[l]),
                jnp.tile(nsa_k_gain[l, 1], 2)[None, :], jnp.tile(nsa_k_gain[l, 2], 2)[None, :],
                col(nsa_k_gain[l, 1]), col(nsa_k_gain[l, 2]), col(fox_b_f[l]), jnp.tile(nsa_q_gain[l], 2)[None, :])
        lw = dict(
            ffn1=ffn1, ffn2=ffn2, wb=wb_all, w_out=wo_all,
            mix_norm=mix_norm[l][None, :], proj=proj, w_gate=w_gate,
            gmlp_w_s=gmlp_w_s[l], gmlp_bexp=jnp.repeat(gmlp_b_s[l].T, HEAD_DIM, axis=1),
            gmlp_w00=jnp.repeat(gmlp_w_s[l, :, 0, 0], HEAD_DIM)[None, :],
            gmlp_b0=jnp.repeat(gmlp_b_s[l, :, 0], HEAD_DIM)[None, :],
            cw=_prep_compress(nsa_cmp_w1[l], nsa_cmp_w2[l], nsa_cmp_pe[l], nsa_k_gain[l, 0]),
            wb2=wb2.astype(BF16), nsa_aug=nsa_aug,
        )
        xp, xs, st, stacked = _layer(xp, xs, lw, page_table, caches, l, depth, stacked, (b, s, db))
        states.append(st)

    def stack(name):
        return jnp.stack([st[name] for st in states])

    def tok_major(a, *feat_dims):
        return jnp.moveaxis(a.reshape(a.shape[:2] + feat_dims + a.shape[3:]), -1, 2)

    def tok_major_s(a, *feat_dims):
        return jnp.swapaxes(tok_major(a, *feat_dims), 1, 2)

    foxt, nsat, wint, lft, win_next = stacked
    return (xp.reshape(b, s, d), xs.reshape(db, 1, d),
            tok_major(foxt, 2, N_HEADS, HEAD_DIM), tok_major_s(stack('fox_kv_s'), 2, N_HEADS, HEAD_DIM),
            tok_major(lft, N_HEADS), tok_major_s(stack('fox_logf_s'), N_HEADS),
            tok_major(nsat, 4, N_GROUPS, HEAD_DIM), tok_major_s(stack('nsa_kv_s'), 4, N_GROUPS, HEAD_DIM),
            tok_major(wint[:, :, :, s - min(WINDOW, s):], 2, N_GROUPS, HEAD_DIM),
            tok_major(win_next.reshape(depth, db, 2 * LANES, wb), 2, N_GROUPS, HEAD_DIM),
            stack('gmlp_v_p'), stack('gmlp_v_s'))
```

```python
import functools

import jax
import jax.numpy as jnp
import numpy as np
from jax import lax
from jax.experimental import pallas as pl
from jax.experimental.pallas import tpu as pltpu

F32 = jnp.float32
BF16 = jnp.bfloat16

HEAD_DIM = 64
LANES = 128
PAGE = 128
CHUNK = 128
GMLP_CHUNKS_PER_STEP = 4
DECODE_ROWS = 2
Q_BLOCK = 256
FOX_TQ = 512
ATT_TK = 512
CMP_LEN = 32
CMP_STRIDE = 16
SEL_BLOCK = 64
N_SEL = 8
WINDOW = 512
N_GROUPS = 2
N_REP = 4
N_HEADS = 8
FORCE_BONUS = 1.0e4
NEG = -1.0e30
EPS = 1e-6
SCALE = HEAD_DIM ** -0.5
VMEM_LIMIT = 56 * 1024 * 1024
HI = lax.Precision.HIGHEST


def _cparams(*sem):
    return pltpu.CompilerParams(dimension_semantics=sem, vmem_limit_bytes=VMEM_LIMIT)


def _dot(a, b, precision=None):
    return jnp.dot(a, b, preferred_element_type=F32, precision=precision)


def _dot_nt(a, b):
    return lax.dot_general(a, b, (((1,), (1,)), ((), ())), preferred_element_type=F32)


def _shr(x, pow2):
    return jnp.right_shift(x, int(pow2).bit_length() - 1)


def _rms_rows(x, g):
    ms = jnp.mean(x * x, axis=-1, keepdims=True)
    return x * lax.rsqrt(ms + EPS) * g


def _headnorm_pair(zb, gain):
    lo = lax.broadcasted_iota(jnp.int32, zb.shape, 1) < HEAD_DIM
    sq = zb * zb
    s_lo = jnp.sum(jnp.where(lo, sq, 0.0), axis=-1, keepdims=True)
    s_hi = jnp.sum(jnp.where(lo, 0.0, sq), axis=-1, keepdims=True)
    ms = jnp.where(lo, s_lo, s_hi) * (1.0 / HEAD_DIM)
    return zb * lax.rsqrt(ms + EPS) * gain


def _headnorm_cols(zt, gain_col):
    ms = jnp.mean(zt * zt, axis=0, keepdims=True)
    return zt * lax.rsqrt(ms + EPS) * gain_col


def _silu(x):
    return x * jax.nn.sigmoid(x)


def _gelu_tanh(x):
    return 0.5 * x * (1.0 + jnp.tanh(np.sqrt(2.0 / np.pi) * (x + 0.044715 * (x * x * x))))


def _log_sigmoid(x):
    return jnp.minimum(x, 0.0) - jnp.log1p(jnp.exp(-jnp.abs(x)))


def _token_tile(t):
    for tm in (512, 384, 256, 128):
        if t % tm == 0:
            return tm
    raise ValueError(f"token count {t} is not a multiple of 128")


def _const_spec(shape):
    nd = len(shape)
    return pl.BlockSpec(shape, lambda *_: (0,) * nd, pipeline_mode=pl.Buffered(1))


def _slab_spec(a, idx):
    nd = a.ndim - 1
    return pl.BlockSpec((None,) + a.shape[1:], lambda *_: (idx,) + (0,) * nd, pipeline_mode=pl.Buffered(1))


def _ffn_body(x_ref, g_ref, wgu_ref, wd_ref, o_ref, *, d_ff, fc):
    x = x_ref[...]
    h = _rms_rows(x, g_ref[...]).astype(BF16)
    acc = jnp.zeros_like(x)
    for c in range(d_ff // fc):
        g = _dot(h, wgu_ref[:, c * fc:(c + 1) * fc])
        u = _dot(h, wgu_ref[:, d_ff + c * fc:d_ff + (c + 1) * fc])
        a = (_silu(g) * u).astype(BF16)
        acc = acc + _dot(a, wd_ref[c * fc:(c + 1) * fc, :])
    o_ref[...] = x + 0.5 * acc


def _ffn(x, g, wgu, wd, layer):
    t, d = x.shape
    d_ff = wd.shape[1]
    tm = _token_tile(t)
    fc = 256 if d_ff % 256 == 0 else 128
    return pl.pallas_call(
        functools.partial(_ffn_body, d_ff=d_ff, fc=fc),
        grid=(t // tm,),
        in_specs=[pl.BlockSpec((tm, d), lambda i: (i, 0)),
                  _slab_spec(g, layer), _slab_spec(wgu, layer), _slab_spec(wd, layer)],
        out_specs=pl.BlockSpec((tm, d), lambda i: (i, 0)),
        out_shape=jax.ShapeDtypeStruct((t, d), F32),
        compiler_params=_cparams("parallel"),
        name="ffn",
    )(x, g, wgu, wd)


C_FOX = 0
C_GM = 1536
C_NQ = 2560
C_NKV = 3584
C_SMALL = 4352
C_END = 4480
C_ROWS = C_END + 512


def _proj_body(x_ref, g_ref, w_ref, gq_ref, gk_ref, gkc_ref, gv_ref, gnq_ref, gk1_ref, gk2_ref,
               gk1c_ref, gk2c_ref, bfc_ref, gnqc_ref, *rest, sample, n_prev):
    out_refs = rest[n_prev:]
    (fq_ref, foxt_ref, foxtb_ref, gmu_ref, gmv_ref, nq_ref, nsat_ref, wint_ref, nsatb_ref, cmpraw_ref,
     lft_ref, gate_ref) = out_refs[:12]
    h = _rms_rows(x_ref[...], g_ref[...]).astype(BF16)

    z = _dot_nt(h, w_ref[0:512, :])
    for j in range(4):
        sl = slice(j * LANES, (j + 1) * LANES)
        fq_ref[:, sl] = (_headnorm_pair(z[:, sl], gq_ref[...]) * SCALE).astype(BF16)
    zt = _dot_nt(w_ref[512:C_GM, :], h)
    for hh in range(N_HEADS):
        rs = slice(hh * HEAD_DIM, (hh + 1) * HEAD_DIM)
        kt = _headnorm_cols(zt[rs], gkc_ref[...])
        foxt_ref[rs, :] = kt
        foxtb_ref[rs, :] = kt.astype(BF16)
    foxt_ref[512:1024, :] = zt[512:1024]
    foxtb_ref[512:1024, :] = zt[512:1024].astype(BF16)

    a = _gelu_tanh(_dot_nt(h, w_ref[C_GM:C_NQ, :]))
    gmu_ref[...] = a[:, 0:512]
    gmv_ref[...] = _rms_rows(a[:, 512:1024], gv_ref[...])

    if sample:
        z = _dot_nt(h, w_ref[C_NQ:C_NKV, :])
        for hh in range(N_HEADS):
            sl = slice(hh * LANES, (hh + 1) * LANES)
            zb = z[:, sl]
            ms = jnp.sum(zb * zb, axis=-1, keepdims=True) * (1.0 / HEAD_DIM)
            nq_ref[:, sl] = (zb * lax.rsqrt(ms + EPS) * (gnq_ref[:, sl] * SCALE)).astype(BF16)
    else:
        z = _dot_nt(h, w_ref[C_END:C_END + 512, :])
        for j in range(4):
            sl = slice(j * LANES, (j + 1) * LANES)
            nq_ref[:, sl] = (_headnorm_pair(z[:, sl], gnqc_ref[...]) * SCALE).astype(BF16)

    zt = _dot_nt(w_ref[C_NKV:C_SMALL, :], h)
    nsat_ref[0:256, :] = zt[0:256]
    nsat_ref[384:512, :] = zt[384:512]
    nsatb_ref[128:256, :] = zt[384:512].astype(BF16)
    wint_ref[128:256, :] = zt[640:768]
    nsatb_ref[384:512, :] = zt[640:768].astype(BF16)
    for g in range(N_GROUPS):
        sk = _headnorm_cols(zt[256 + g * HEAD_DIM:256 + (g + 1) * HEAD_DIM], gk1c_ref[...])
        nsat_ref[256 + g * HEAD_DIM:256 + (g + 1) * HEAD_DIM, :] = sk
        nsatb_ref[g * HEAD_DIM:(g + 1) * HEAD_DIM, :] = sk.astype(BF16)
        wk = _headnorm_cols(zt[512 + g * HEAD_DIM:512 + (g + 1) * HEAD_DIM], gk2c_ref[...])
        wint_ref[g * HEAD_DIM:(g + 1) * HEAD_DIM, :] = wk
        nsatb_ref[256 + g * HEAD_DIM:256 + (g + 1) * HEAD_DIM, :] = wk.astype(BF16)

    cmpraw_ref[...] = _dot_nt(h, w_ref[C_NKV:C_NKV + 256, :])

    zt = _dot_nt(w_ref[C_SMALL:C_SMALL + 16, :], h)
    lft_ref[...] = _log_sigmoid(zt[0:N_HEADS] + bfc_ref[...])
    gate_ref[...] = jax.nn.sigmoid(_dot_nt(h, w_ref[C_SMALL:C_END, :]))

    if sample:
        foxs_ref, nsabs_ref = out_refs[12:]
        z = _dot_nt(h, w_ref[512:C_GM, :])
        for j in range(4):
            sl = slice(j * LANES, (j + 1) * LANES)
            foxs_ref[:, sl] = _headnorm_pair(z[:, sl], gk_ref[...])
        foxs_ref[:, 512:1024] = z[:, 512:1024]
        z = _dot_nt(h, w_ref[C_NKV + 256:C_SMALL, :])
        nsabs_ref[:, 0:128] = _headnorm_pair(z[:, 0:128], gk1_ref[...]).astype(BF16)
        nsabs_ref[:, 128:256] = z[:, 128:256].astype(BF16)
        nsabs_ref[:, 256:384] = _headnorm_pair(z[:, 256:384], gk2_ref[...]).astype(BF16)
        nsabs_ref[:, 384:512] = z[:, 384:512].astype(BF16)


STATE_OUTS = (1, 6, 7, 10)


def _proj(x, pw, b, s, sample, layer=0, depth=1, prev=None):
    t, d = x.shape
    tm = _token_tile(s)
    nst = s // tm
    tok = lambda n, dt: (pl.BlockSpec((tm, n), lambda i: (i, 0)), jax.ShapeDtypeStruct((t, n), dt))
    feat = lambda n, dt: (pl.BlockSpec((None, n, tm), lambda i: (i // nst, 0, i % nst)),
                          jax.ShapeDtypeStruct((b, n, s), dt))
    state = lambda n: (pl.BlockSpec((None, None, n, tm), lambda i: (layer, i // nst, 0, i % nst)),
                       jax.ShapeDtypeStruct((depth, b, n, s), F32))
    outs = [tok(512, BF16), feat(1024, F32), feat(1024, BF16), tok(512, F32), tok(512, F32),
            tok(1024 if sample else 512, BF16),
            feat(512, F32), feat(256, F32), feat(512, BF16), tok(256, F32), feat(N_HEADS, F32), tok(LANES, F32)]
    if sample:
        outs += [tok(1024, F32), tok(512, BF16)]
    else:
        for k in STATE_OUTS:
            outs[k] = state(outs[k][1].shape[1])
    prev = () if prev is None else tuple(prev)
    return pl.pallas_call(
        functools.partial(_proj_body, sample=sample, n_prev=len(prev)),
        grid=(t // tm,),
        in_specs=[pl.BlockSpec((tm, d), lambda i: (i, 0)), _const_spec(pw[0].shape), _slab_spec(pw[1], layer)]
                 + [_const_spec(a.shape) for a in pw[2:]] + [pl.BlockSpec(memory_space=pl.ANY)] * len(prev),
        out_specs=[o[0] for o in outs],
        out_shape=[o[1] for o in outs],
        input_output_aliases={1 + len(pw) + k: STATE_OUTS[k] for k in range(len(prev))},
        compiler_params=_cparams("parallel"),
        name="in_proj_sample" if sample else "in_proj",
    )(x, *pw, *prev)


def _cumsum_body(x_ref, o_ref, *, blk):
    s = x_ref.shape[1]
    r = lax.broadcasted_iota(jnp.int32, (blk, blk), 0)
    c = lax.broadcasted_iota(jnp.int32, (blk, blk), 1)
    tri = (r <= c).astype(F32)
    carry = jnp.zeros((x_ref.shape[0], 1), F32)
    for i in range(s // blk):
        cs = _dot(x_ref[:, i * blk:(i + 1) * blk], tri, precision=HI) + carry
        o_ref[:, i * blk:(i + 1) * blk] = cs
        carry = cs[:, blk - 1:blk]


def _cumsum(x, layer):
    _, b, n, s = x.shape
    return pl.pallas_call(
        functools.partial(_cumsum_body, blk=min(256, s)),
        grid=(b,),
        in_specs=[pl.BlockSpec((None, None, n, s), lambda i: (layer, i, 0, 0))],
        out_specs=pl.BlockSpec((None, n, s), lambda i: (i, 0, 0)),
        out_shape=jax.ShapeDtypeStruct((b, n, s), F32),
        compiler_params=_cparams("parallel"),
        name="logf_cumsum",
    )(x)


def _fox_body(q_ref, k_ref, v_ref, c_ref, o_ref, *, tq, tk):
    qi = pl.program_id(2)
    q = q_ref[...]
    lo = lax.broadcasted_iota(jnp.int32, (tq, LANES), 1) < HEAD_DIM
    zero = jnp.zeros_like(q)
    qs = jnp.concatenate([jnp.where(lo, q, zero), jnp.where(lo, zero, q)], axis=0)
    qpos = qi * tq + lax.broadcasted_iota(jnp.int32, (tq, 1), 0)

    def block(kb, carry, masked):
        t0 = pl.multiple_of(kb * tk, tk)
        m_old, l_old, acc = carry
        bias = -c_ref[:, pl.ds(t0, tk)]
        s = _dot(qs, k_ref[:, pl.ds(t0, tk)]).reshape(2, tq, tk) + bias[:, None, :]
        if masked:
            kpos = t0 + lax.broadcasted_iota(jnp.int32, (1, tk), 1)
            s = jnp.where((kpos <= qpos)[None], s, NEG)
        s = s.reshape(2 * tq, tk)
        m_new = jnp.maximum(m_old, jnp.max(s, axis=-1, keepdims=True))
        alpha = jnp.exp(m_old - m_new)
        p = jnp.exp(s - m_new)
        l_new = alpha * l_old + jnp.sum(p, axis=-1, keepdims=True)
        return m_new, l_new, alpha * acc + _dot_nt(p.astype(BF16), v_ref[:, pl.ds(t0, tk)])

    n_kb = _shr(qi * tq + tq + tk - 1, tk)
    carry = (jnp.full((2 * tq, 1), NEG, F32), jnp.zeros((2 * tq, 1), F32), jnp.zeros((2 * tq, LANES), F32))
    carry = lax.fori_loop(0, n_kb - 1, lambda kb, c: block(kb, c, False), carry)
    _, l, acc = block(n_kb - 1, carry, True)
    o = acc / l
    o_ref[...] = jnp.where(lo, o[0:tq], o[tq:2 * tq]).astype(o_ref.dtype)


def _fox_prompt(fq, foxtb, c):
    b, s, _ = fq.shape
    tq = min(FOX_TQ, s)
    tk = min(ATT_TK, s)
    return pl.pallas_call(
        functools.partial(_fox_body, tq=tq, tk=tk),
        grid=(b, 4, s // tq),
        in_specs=[pl.BlockSpec((None, tq, LANES), lambda bi, p, qi: (bi, qi, p)),
                  pl.BlockSpec((None, LANES, s), lambda bi, p, qi: (bi, p, 0)),
                  pl.BlockSpec((None, LANES, s), lambda bi, p, qi: (bi, 4 + p, 0)),
                  pl.BlockSpec((None, None, 2, s), lambda bi, p, qi: (bi, p, 0, 0))],
        out_specs=pl.BlockSpec((None, tq, LANES), lambda bi, p, qi: (bi, qi, p)),
        out_shape=jax.ShapeDtypeStruct((b, s, 512), BF16),
        compiler_params=_cparams("parallel", "parallel", "arbitrary"),
        name="fox_prompt",
    )(fq, foxtb, foxtb, c)


def _gmlp_body(u_ref, v_ref, ws_ref, b_ref, o_ref):
    n = ws_ref.shape[1]
    r = lax.broadcasted_iota(jnp.int32, (n, n), 0)
    c = lax.broadcasted_iota(jnp.int32, (n, n), 1)
    lo = lax.broadcasted_iota(jnp.int32, (n, LANES), 1) < HEAD_DIM
    for j in range(4):
        sl = slice(j * LANES, (j + 1) * LANES)
        w0 = jnp.where(r >= c, ws_ref[2 * j], 0.0).astype(BF16)
        w1 = jnp.where(r >= c, ws_ref[2 * j + 1], 0.0).astype(BF16)
        for ch in range(u_ref.shape[0] // n):
            rs = slice(ch * n, (ch + 1) * n)
            vp = v_ref[rs, sl].astype(BF16)
            mixed = jnp.where(lo, _dot(w0, vp), _dot(w1, vp)) + b_ref[:, sl]
            o_ref[rs, sl] = (u_ref[rs, sl] * mixed).astype(o_ref.dtype)


def _gmlp_prompt(gmu, gmv, ws, bexp):
    t = gmu.shape[0]
    rows = GMLP_CHUNKS_PER_STEP * CHUNK if t % (GMLP_CHUNKS_PER_STEP * CHUNK) == 0 else CHUNK
    return pl.pallas_call(
        _gmlp_body,
        grid=(t // rows,),
        in_specs=[pl.BlockSpec((rows, 512), lambda i: (i, 0)),
                  pl.BlockSpec((rows, 512), lambda i: (i, 0)),
                  _const_spec(ws.shape), _const_spec(bexp.shape)],
        out_specs=pl.BlockSpec((rows, 512), lambda i: (i, 0)),
        out_shape=jax.ShapeDtypeStruct((t, 512), BF16),
        compiler_params=_cparams("parallel"),
        name="gmlp_prompt",
    )(gmu, gmv, ws, bexp)


def _gmlp_first_body(u_ref, v_ref, w_ref, b_ref, o_ref):
    o_ref[...] = (u_ref[...] * (v_ref[...] * w_ref[...] + b_ref[...])).astype(o_ref.dtype)


def _gmlp_sample(gmu, gmv, wrow, brow):
    n = gmu.shape[0]
    full = pl.BlockSpec((n, 512), lambda i: (0, 0))
    return pl.pallas_call(
        _gmlp_first_body,
        grid=(1,),
        in_specs=[full, full, _const_spec((1, 512)), _const_spec((1, 512))],
        out_specs=full,
        out_shape=jax.ShapeDtypeStruct((n, 512), BF16),
        compiler_params=_cparams("arbitrary"),
        name="gmlp_sample",
    )(gmu, gmv, wrow, brow)


def _compress_finish(xa, xb, wa_ref, wb_ref, w2_ref, kg0_ref):
    nhb = xa.shape[0]
    a = _dot(xa, wa_ref[...])
    b = _dot(xb, wb_ref[...])
    pre = a + pltpu.roll(b, nhb - 1, 0)
    out = _dot(_silu(pre).astype(BF16), w2_ref[...])
    kc = _headnorm_pair(out[:, 0:LANES], kg0_ref[...])
    return kc, out[:, LANES:2 * LANES]


def _compress(xk_ref, xv_ref, wa_ref, wb_ref, pea_ref, peb_ref, w2_ref, kg0_ref):
    nhb = xk_ref.shape[0] // CMP_STRIDE
    xa, xb = [], []
    for l in range(CMP_STRIDE):
        xl = jnp.concatenate([xk_ref[pl.ds(l, nhb, stride=CMP_STRIDE), :],
                              xv_ref[pl.ds(l, nhb, stride=CMP_STRIDE), :]], axis=1)
        xa.append((xl + pea_ref[l]).astype(BF16))
        xb.append((xl + peb_ref[l]).astype(BF16))
    return _compress_finish(jnp.concatenate(xa, axis=1), jnp.concatenate(xb, axis=1), wa_ref, wb_ref, w2_ref, kg0_ref)


def _compress_body(xk_ref, xv_ref, wa_ref, wb_ref, pea_ref, peb_ref, w2_ref, kg0_ref, pet_ref, o_ref):
    kc, vc = _compress(xk_ref, xv_ref, wa_ref, wb_ref, pea_ref, peb_ref, w2_ref, kg0_ref)
    o_ref[:, 0:LANES] = kc.astype(BF16)
    o_ref[:, LANES:2 * LANES] = vc.astype(BF16)


def _compress_prompt(cmpraw, cw):
    b, s, _ = cmpraw.shape
    nhb = s // CMP_STRIDE
    return pl.pallas_call(
        _compress_body,
        grid=(b,),
        in_specs=[pl.BlockSpec((None, s, LANES), lambda i: (i, 0, 0)),
                  pl.BlockSpec((None, s, LANES), lambda i: (i, 0, 1))] + [_const_spec(a.shape) for a in cw],
        out_specs=pl.BlockSpec((None, nhb, 256), lambda i: (i, 0, 0)),
        out_shape=jax.ShapeDtypeStruct((b, nhb, 256), BF16),
        compiler_params=_cparams("parallel"),
        name="nsa_compress",
    )(cmpraw, cmpraw, *cw)


def _slope_col(head):
    out = jnp.zeros(head.shape, F32)
    for hh in range(N_HEADS):
        out = jnp.where(head == hh, 2.0 ** (-(hh + 1)), out)
    return out


def _slope_rows(g):
    return jnp.concatenate([jnp.full((Q_BLOCK, 1), 2.0 ** (-(N_REP * g + r + 1)), F32) for r in range(N_REP)], axis=0)


def _cmp_softmax(s, mask):
    s = jnp.where(mask, s, NEG)
    m = jnp.max(s, axis=-1, keepdims=True)
    e = jnp.where(mask, jnp.exp(s - m), 0.0)
    l = jnp.sum(e, axis=-1, keepdims=True)
    return e / jnp.where(l > 0.0, l, 1.0)


def _select_blocks(imp, cur, n_blk):
    m, n = imp.shape
    pool = (_shr(lax.broadcasted_iota(jnp.int32, (n, LANES), 0), SEL_BLOCK // CMP_STRIDE)
            == lax.broadcasted_iota(jnp.int32, (n, LANES), 1)).astype(F32)
    score = _dot(imp, pool, precision=HI)
    j = lax.broadcasted_iota(jnp.int32, (m, LANES), 1)
    forced = (j == 0) | (j == cur) | (j == cur - 1)
    score = jnp.where(forced, score + FORCE_BONUS, score)
    score = jnp.where(j > cur, NEG, score)
    rank = jnp.zeros((m, LANES), F32)
    for i in range(n_blk):
        ci = score[:, i:i + 1]
        ahead = (ci > score) | ((ci == score) & (j > i))
        rank = rank + ahead.astype(F32)
    return (rank < float(min(N_SEL, n_blk))) & (score > 0.5 * NEG)


def _expand_sel(sel_bf, t0, n):
    jj = lax.broadcasted_iota(jnp.int32, (LANES, n), 0)
    tt = t0 + lax.broadcasted_iota(jnp.int32, (LANES, n), 1)
    e = (_shr(tt, SEL_BLOCK) == jj).astype(BF16)
    return _dot(sel_bf, e)


AUG_SEL = 16
MAX_SEL_BLOCKS = 32


def _aug_rows(pos):
    n = pos.shape[1]
    r = lax.broadcasted_iota(jnp.int32, (HEAD_DIM, n), 0)
    onehot = (r >= AUG_SEL) & (_shr(pos, SEL_BLOCK) == r - AUG_SEL)
    return jnp.where(r == 0, _shr(pos, LANES).astype(F32),
                     jnp.where(r == 1, (pos & (LANES - 1)).astype(F32), onehot.astype(F32)))


def _select_bias_t(imp, qi, off, n_blk):
    n = imp.shape[1]
    base = off + AUG_SEL
    prow = lax.broadcasted_iota(jnp.int32, (LANES, n), 0) - base
    pool_t = (prow == _shr(lax.broadcasted_iota(jnp.int32, (LANES, n), 1), SEL_BLOCK // CMP_STRIDE)).astype(F32)
    score = lax.dot_general(pool_t, imp, (((1,), (1,)), ((), ())), preferred_element_type=F32,
                            precision=HI)[base:base + MAX_SEL_BLOCKS]
    j = lax.broadcasted_iota(jnp.int32, (MAX_SEL_BLOCKS, 1), 0)
    cur = _shr(qi * Q_BLOCK + lax.broadcasted_iota(jnp.int32, (1, Q_BLOCK), 1), SEL_BLOCK)
    forced = (j == 0) | (j == cur) | (j == cur - 1)
    score = jnp.where(forced, score + FORCE_BONUS, score)
    score = jnp.where(j > cur, NEG, score)
    rank = jnp.zeros(score.shape, F32)
    for i in range(n_blk):
        ci = score[i:i + 1, :]
        ahead = (ci > score) | ((ci == score) & (j > i))
        rank = rank + ahead.astype(F32)
    chosen = (rank < float(min(N_SEL, n_blk))) & (score > 0.5 * NEG)
    bias_t = jnp.where(chosen, 0.0, NEG)
    parts = [bias_t, jnp.zeros((LANES - base - MAX_SEL_BLOCKS, Q_BLOCK), F32)]
    if base:
        parts = [jnp.zeros((base, Q_BLOCK), F32)] + parts
    return jnp.concatenate(parts, axis=0).T


def _nsa_body(nq_ref, cmp_ref, kv_ref, aug_ref, gate_ref, o_ref, *, n_blk, tk, wlen):
    qi = pl.program_id(1)
    s_len = kv_ref.shape[1]
    rows = N_REP * Q_BLOCK
    row = lax.broadcasted_iota(jnp.int32, (rows, 1), 0)
    qpos = qi * Q_BLOCK + (row & (Q_BLOCK - 1))
    qpos_q = qi * Q_BLOCK + lax.broadcasted_iota(jnp.int32, (Q_BLOCK, 1), 0)
    lane = lax.broadcasted_iota(jnp.int32, (1, LANES), 1)
    nhb = cmp_ref.shape[0]
    end_col = lax.broadcasted_iota(jnp.int32, (nhb, 1), 0) * CMP_STRIDE + (CMP_LEN - 1)
    end = lax.broadcasted_iota(jnp.int32, (1, nhb), 1) * CMP_STRIDE + (CMP_LEN - 1)

    n_kb = _shr(qi * Q_BLOCK + Q_BLOCK + tk - 1, tk)
    t_last = pl.multiple_of((n_kb - 1) * tk, tk)
    causal = jnp.where(t_last + lax.broadcasted_iota(jnp.int32, (1, tk), 1) <= qpos_q, 0.0, NEG)
    t_win = pl.multiple_of(jnp.clip((qi - WINDOW // Q_BLOCK) * Q_BLOCK, 0, s_len - wlen), Q_BLOCK)
    d_win = qpos_q - (t_win + lax.broadcasted_iota(jnp.int32, (1, wlen), 1))
    band = jnp.where((d_win >= 0) & (d_win < WINDOW), 0.0, NEG)

    def masked(s, bias):
        n = s.shape[1]
        return (s.reshape(N_REP, Q_BLOCK, n) + bias[None]).reshape(rows, n)

    for g in range(N_GROUPS):
        off = HEAD_DIM * (1 - g)
        in_g = _shr(lane, HEAD_DIM) == g
        tiles = []
        for j in range(N_REP // 2):
            pair = nq_ref[:, (2 * g + j) * LANES:(2 * g + j + 1) * LANES].astype(F32)
            moved = pltpu.roll(pair, HEAD_DIM, 1)
            first, second = (pair, moved) if g == 0 else (moved, pair)
            tiles += [jnp.where(in_g, first, 0.0), jnp.where(in_g, second, 0.0)]
        q = jnp.concatenate(tiles, axis=0)
        slope = _slope_rows(g)
        q_alibi = q + jnp.where(lane == off, slope * float(LANES), 0.0) + jnp.where(lane == off + 1, slope, 0.0)
        q_win = q_alibi.astype(BF16)

        def keys(lo, t0, n):
            k = kv_ref[lo + g * HEAD_DIM:lo + (g + 1) * HEAD_DIM, pl.ds(t0, n)]
            a = aug_ref[:, pl.ds(t0, n)]
            return jnp.concatenate([k, a] if g == 0 else [a, k], axis=0)

        kc = cmp_ref[:, 0:LANES].astype(F32)
        kc = jnp.where(in_g, kc, jnp.where(lane == off, _shr(end_col, LANES).astype(F32),
                                           jnp.where(lane == off + 1, (end_col & (LANES - 1)).astype(F32), 0.0)))
        p = _cmp_softmax(_dot_nt(q_win, kc.astype(BF16)), end <= qpos)
        o_cmp = _dot(p.astype(BF16), cmp_ref[:, LANES:2 * LANES])
        imp = p[0:Q_BLOCK]
        for r in range(1, N_REP):
            imp = imp + p[r * Q_BLOCK:(r + 1) * Q_BLOCK]
        sel_bias = _select_bias_t(imp, qi, off, n_blk)
        q_sel = (q_alibi + jnp.concatenate([sel_bias] * N_REP, axis=0)).astype(BF16)

        def sel_tile(t0, carry, last):
            m_old, l_old, acc = carry
            s = _dot(q_sel, keys(0, t0, tk))
            if last:
                s = masked(s, causal)
            v = kv_ref[LANES:2 * LANES, pl.ds(t0, tk)]
            m_new = jnp.maximum(m_old, jnp.max(s, axis=-1, keepdims=True))
            alpha = jnp.exp(m_old - m_new)
            p = jnp.exp(s - m_new)
            l_new = alpha * l_old + jnp.sum(p, axis=-1, keepdims=True)
            return m_new, l_new, alpha * acc + _dot_nt(p.astype(BF16), v)

        carry = (jnp.full((rows, 1), NEG, F32), jnp.zeros((rows, 1), F32), jnp.zeros((rows, LANES), F32))
        carry = lax.fori_loop(0, n_kb - 1, lambda kb, c: sel_tile(pl.multiple_of(kb * tk, tk), c, False), carry)
        _, l_sel, acc_sel = sel_tile(t_last, carry, True)
        o_sel = acc_sel / l_sel

        s = masked(_dot(q_win, keys(2 * LANES, t_win, wlen)), band)
        p = jnp.exp(s - jnp.max(s, axis=-1, keepdims=True))
        v = kv_ref[3 * LANES:4 * LANES, pl.ds(t_win, wlen)]
        o_win = _dot_nt(p.astype(BF16), v) / jnp.sum(p, axis=-1, keepdims=True)

        def gated(r):
            hh = N_REP * g + r
            rs = slice(r * Q_BLOCK, (r + 1) * Q_BLOCK)
            return (gate_ref[:, 8 + hh:9 + hh] * o_cmp[rs] + gate_ref[:, 16 + hh:17 + hh] * o_sel[rs]
                    + gate_ref[:, 24 + hh:25 + hh] * o_win[rs])

        lo = lane < HEAD_DIM
        for j in range(N_REP // 2):
            first, second = gated(2 * j), gated(2 * j + 1)
            if g == 0:
                pair = jnp.where(lo, first, pltpu.roll(second, HEAD_DIM, 1))
            else:
                pair = jnp.where(lo, pltpu.roll(first, HEAD_DIM, 1), second)
            o_ref[:, (2 * g + j) * LANES:(2 * g + j + 1) * LANES] = pair.astype(o_ref.dtype)


def _nsa_aug_body(o_ref):
    o_ref[...] = _aug_rows(lax.broadcasted_iota(jnp.int32, (1, o_ref.shape[1]), 1)).astype(o_ref.dtype)


def _nsa_aug(s):
    return pl.pallas_call(
        _nsa_aug_body,
        out_shape=jax.ShapeDtypeStruct((HEAD_DIM, s), BF16),
        name="nsa_aug",
    )()


def _nsa_prompt(nq, cmpkv, nsatb, aug, gates):
    b, s, w = nq.shape
    nhb = cmpkv.shape[1]
    n_blk = s // SEL_BLOCK
    assert n_blk <= MAX_SEL_BLOCKS and s // LANES <= 256
    return pl.pallas_call(
        functools.partial(_nsa_body, n_blk=n_blk, tk=min(ATT_TK, s), wlen=min(WINDOW + Q_BLOCK, s)),
        grid=(b, s // Q_BLOCK),
        in_specs=[pl.BlockSpec((None, Q_BLOCK, w), lambda bi, qi: (bi, qi, 0)),
                  pl.BlockSpec((None, nhb, 256), lambda bi, qi: (bi, 0, 0)),
                  pl.BlockSpec((None, 512, s), lambda bi, qi: (bi, 0, 0)),
                  pl.BlockSpec((HEAD_DIM, s), lambda bi, qi: (0, 0)),
                  pl.BlockSpec((None, Q_BLOCK, LANES), lambda bi, qi: (bi, qi, 0))],
        out_specs=pl.BlockSpec((None, Q_BLOCK, w), lambda bi, qi: (bi, qi, 0)),
        out_shape=jax.ShapeDtypeStruct((b, s, w), BF16),
        compiler_params=_cparams("parallel", "arbitrary"),
        name="nsa_prompt",
    )(nq, cmpkv, nsatb, aug, gates)


def _merge_body(x_ref, g_ref, of_ref, og_ref, on_ref, wg_ref, wb0_ref, wb1_ref, wb2_ref, wo_ref, o_ref):
    x = x_ref[...]
    d = x.shape[1]
    h = _rms_rows(x, g_ref[...]).astype(BF16)
    y = jax.nn.sigmoid(_dot_nt(h, wg_ref[0:d, :])) * _dot(of_ref[...], wb0_ref[...])
    y = y + jax.nn.sigmoid(_dot_nt(h, wg_ref[d:2 * d, :])) * _dot(og_ref[...], wb1_ref[...])
    y = y + jax.nn.sigmoid(_dot_nt(h, wg_ref[2 * d:3 * d, :])) * _dot(on_ref[...], wb2_ref[...])
    o_ref[...] = x + _dot(y.astype(BF16), wo_ref[...])


def _merge(x, g, o_fox, o_gm, o_nsa, wg, wb, wb2, wo, layer):
    t, d = x.shape
    tm = _token_tile(t)
    row = lambda n: pl.BlockSpec((tm, n), lambda i: (i, 0))
    padded = o_nsa.shape[1] != wb.shape[1]
    return pl.pallas_call(
        _merge_body,
        grid=(t // tm,),
        in_specs=[row(d), _const_spec((1, d)), row(512), row(512), row(o_nsa.shape[1]), _slab_spec(wg, layer),
                  _slab_spec(wb, 3 * layer), _slab_spec(wb, 3 * layer + 1),
                  _const_spec(wb2.shape) if padded else _slab_spec(wb, 3 * layer + 2), _slab_spec(wo, layer)],
        out_specs=row(d),
        out_shape=jax.ShapeDtypeStruct((t, d), F32),
        compiler_params=_cparams("parallel"),
        name="merge",
    )(x, g, o_fox, o_gm, o_nsa, wg, wb, wb, wb2 if padded else wb, wo)


def _column(x, idx):
    lane = lax.broadcasted_iota(jnp.int32, x.shape, 1)
    return jnp.sum(jnp.where(lane == idx, x, 0.0), axis=1, keepdims=True)


def _logf_suffix_body(x_ref, o_ref):
    n, h, t = x_ref.shape
    tri = (lax.broadcasted_iota(jnp.int32, (t, t), 0) >= lax.broadcasted_iota(jnp.int32, (t, t), 1)).astype(F32)
    o_ref[...] = _dot(x_ref[...].reshape(n * h, t), tri, precision=HI).reshape(n, h, t)


def _logf_suffix(cache_lf, layer):
    n_pool = cache_lf.shape[1]
    blk = next(c for c in (256, 128, 64, 32, 16, 8, 4, 2, 1) if n_pool % c == 0)
    return pl.pallas_call(
        _logf_suffix_body,
        grid=(n_pool // blk,),
        in_specs=[pl.BlockSpec((None, blk, N_HEADS, PAGE), lambda i: (layer, i, 0, 0))],
        out_specs=pl.BlockSpec((blk, N_HEADS, PAGE), lambda i: (i, 0, 0)),
        out_shape=jax.ShapeDtypeStruct((n_pool, N_HEADS, PAGE), F32),
        compiler_params=_cparams("parallel"),
        name="logf_suffix",
    )(cache_lf)


def _fox_decode_body(pt_ref, *refs, n_pages):
    kv = refs[:n_pages]
    q_ref, new_ref, lfn_ref, sfx_ref, o_ref, s_ref = refs[n_pages:]
    b = pl.program_id(0)
    w = N_HEADS * HEAD_DIM
    row = lax.broadcasted_iota(jnp.int32, (N_HEADS, w), 0)
    lane = lax.broadcasted_iota(jnp.int32, (N_HEADS, w), 1)
    diag = _shr(lane, HEAD_DIM) == row
    qbd = jnp.where(diag, jnp.broadcast_to(q_ref[...].astype(F32), (N_HEADS, w)), 0.0).astype(BF16)
    last = lax.broadcasted_iota(jnp.int32, (N_HEADS, PAGE), 1) == PAGE - 1

    later = _column(lfn_ref[...], b)
    for i in reversed(range(n_pages)):
        incl = sfx_ref[pt_ref[b, i]]
        after = jnp.where(last, 0.0, pltpu.roll(incl, PAGE - 1, 1))
        s_ref[:, i * PAGE:(i + 1) * PAGE] = _dot(qbd, kv[i][0].astype(BF16)) + after + later
        later = later + incl[:, 0:1]

    k_new = new_ref[:, 0:w].astype(BF16).astype(F32)
    v_new = new_ref[:, w:2 * w].astype(BF16).astype(F32)
    s_new = jnp.sum(qbd.astype(F32) * k_new, axis=-1, keepdims=True)
    m = jnp.maximum(jnp.max(s_ref[...], axis=-1, keepdims=True), s_new)
    p_new = jnp.exp(s_new - m)
    l = p_new
    o = p_new.astype(BF16).astype(F32) * v_new
    for i in range(n_pages):
        p = jnp.exp(s_ref[:, i * PAGE:(i + 1) * PAGE] - m)
        l = l + jnp.sum(p, axis=-1, keepdims=True)
        o = o + _dot_nt(p.astype(BF16), kv[i][1].astype(BF16))
    o_ref[...] = jnp.sum(jnp.where(diag, o / l, 0.0), axis=0, keepdims=True).astype(o_ref.dtype)


def _fox_decode(page_table, cache_kv, lf_suffix, layer, fq_s, foxkv_s, lft_s):
    db, n_pages = page_table.shape
    w = N_HEADS * HEAD_DIM
    grid_spec = pltpu.PrefetchScalarGridSpec(
        num_scalar_prefetch=1,
        grid=(db,),
        in_specs=[pl.BlockSpec((None, None, 2, w, PAGE), lambda b, pt, i=i: (layer, pt[b, i], 0, 0, 0))
                  for i in range(n_pages)]
                 + [pl.BlockSpec((None, 1, w), lambda b, pt: (b, 0, 0)),
                    pl.BlockSpec((None, 1, 2 * w), lambda b, pt: (b, 0, 0)),
                    pl.BlockSpec(lft_s.shape, lambda b, pt: (0, 0)),
                    pl.BlockSpec(lf_suffix.shape, lambda b, pt: (0, 0, 0), pipeline_mode=pl.Buffered(1))],
        out_specs=pl.BlockSpec((None, 1, w), lambda b, pt: (b, 0, 0)),
        scratch_shapes=[pltpu.VMEM((N_HEADS, n_pages * PAGE), F32)],
    )
    return pl.pallas_call(
        functools.partial(_fox_decode_body, n_pages=n_pages),
        grid_spec=grid_spec,
        out_shape=jax.ShapeDtypeStruct((db, 1, w), BF16),
        compiler_params=_cparams("arbitrary"),
        name="fox_decode",
    )(page_table, *([cache_kv] * n_pages), fq_s, foxkv_s, lft_s, lf_suffix)


def _nsa_decode_body(pt_ref, *refs, n_pages, rps, n_prev):
    pg = [refs[r * n_pages:(r + 1) * n_pages] for r in range(rps)]
    (win_ref, q_ref, new_ref, gate_ref, wnew_ref, wa_ref, wb_ref, pea_ref, peb_ref, w2_ref, kg0_ref,
     pet_ref) = refs[rps * n_pages:rps * n_pages + 12]
    o_ref, wout_ref, xa_ref, xb_ref, s_ref = refs[rps * n_pages + 12 + n_prev:]
    past = n_pages * PAGE
    nhb = past // CMP_STRIDE
    wbuf = win_ref.shape[3]
    cur = past // SEL_BLOCK
    n_blk = cur + 1
    m = rps * N_HEADS

    def per_row(fn):
        return jnp.concatenate([fn(r) for r in range(rps)], axis=0)

    tp = lax.broadcasted_iota(jnp.int32, (2 * PAGE, 2 * PAGE), 0)
    r16 = tp & (CMP_STRIDE - 1)
    src = _shr(r16, 8) * PAGE + (r16 & 7) * CMP_STRIDE + _shr(tp, CMP_STRIDE)
    perm = (lax.broadcasted_iota(jnp.int32, (2 * PAGE, 2 * PAGE), 1) == src).astype(BF16)
    for r in range(rps):
        for j in range(n_pages // 2):
            pkv = jnp.concatenate([jnp.concatenate([pg[r][2 * j + pp][0], pg[r][2 * j + pp][1]], axis=0)
                                   for pp in range(2)], axis=1)
            for half, x_ref in enumerate((xa_ref, xb_ref)):
                rows = _dot_nt(perm, (pkv + pet_ref[half]).astype(BF16)).astype(BF16)
                for l in range(CMP_STRIDE):
                    x_ref[r * nhb + 16 * j:r * nhb + 16 * (j + 1), 256 * l:256 * (l + 1)] = rows[16 * l:16 * (l + 1), :]
    kc, vc = _compress_finish(xa_ref[...], xb_ref[...], wa_ref, wb_ref, w2_ref, kg0_ref)
    kc = kc.astype(BF16)
    vc = vc.astype(BF16)

    qf = q_ref[...].astype(F32)
    q = [qf[r * N_HEADS:(r + 1) * N_HEADS].astype(BF16) for r in range(rps)]
    row = lax.broadcasted_iota(jnp.int32, (m, 1), 0)
    head = row & (N_HEADS - 1)
    slope = _slope_col(head)
    lane = lax.broadcasted_iota(jnp.int32, (1, LANES), 1)

    def rows_of(x, r):
        return x[r * N_HEADS:(r + 1) * N_HEADS]

    def new_rows(lo):
        return per_row(lambda r: jnp.broadcast_to(new_ref[r][:, lo:lo + LANES].astype(F32), (N_HEADS, LANES)))

    end = lax.broadcasted_iota(jnp.int32, (1, nhb), 1) * CMP_STRIDE + (CMP_LEN - 1)
    s = per_row(lambda r: _dot_nt(q[r], kc[r * nhb:(r + 1) * nhb])) - slope * (past - end).astype(F32)
    p = _cmp_softmax(s, jnp.broadcast_to(end <= past, s.shape))
    o_cmp = per_row(lambda r: _dot(rows_of(p, r).astype(BF16), vc[r * nhb:(r + 1) * nhb]))
    same = (_shr(lax.broadcasted_iota(jnp.int32, (m, m), 0), N_REP)
            == _shr(lax.broadcasted_iota(jnp.int32, (m, m), 1), N_REP)).astype(F32)
    imp = _dot(same, p, precision=HI)
    sel = _select_blocks(imp, jnp.full((m, 1), cur, jnp.int32), n_blk)

    for r in range(rps):
        for i in range(n_pages):
            s_ref[r * N_HEADS:(r + 1) * N_HEADS, i * PAGE:(i + 1) * PAGE] = _dot(q[r], pg[r][i][2].astype(BF16))
    kpos = lax.broadcasted_iota(jnp.int32, (1, past), 1)
    s = s_ref[...] - slope * (past - kpos).astype(F32)
    s = jnp.where(_expand_sel(sel.astype(BF16), 0, past) > 0.5, s, NEG)
    new_ok = jnp.sum(jnp.where(lane == cur, sel.astype(F32), 0.0), axis=-1, keepdims=True) > 0.5
    s_new = jnp.where(new_ok, jnp.sum(qf * new_rows(0), axis=-1, keepdims=True), NEG)
    mx = jnp.maximum(jnp.max(s, axis=-1, keepdims=True), s_new)
    p_new = jnp.where(new_ok, jnp.exp(s_new - mx), 0.0)
    p = jnp.where(s > 0.5 * NEG, jnp.exp(s - mx), 0.0)
    l = jnp.sum(p, axis=-1, keepdims=True) + p_new
    s_ref[...] = p

    def sel_pv(r):
        acc = jnp.zeros((N_HEADS, LANES), F32)
        for i in range(n_pages):
            pr = s_ref[r * N_HEADS:(r + 1) * N_HEADS, i * PAGE:(i + 1) * PAGE]
            acc = acc + _dot_nt(pr.astype(BF16), pg[r][i][3].astype(BF16))
        return acc

    o_sel = (per_row(sel_pv) + p_new.astype(BF16).astype(F32) * new_rows(LANES)) / jnp.where(l > 0.0, l, 1.0)

    pos = lax.broadcasted_iota(jnp.int32, (1, wbuf), 1)
    d = wbuf - pos
    s = per_row(lambda r: _dot(q[r], win_ref[r, 0].astype(BF16))) - slope * d.astype(F32)
    s = jnp.where(d < WINDOW, s, NEG)
    s_new = jnp.sum(qf * new_rows(2 * LANES), axis=-1, keepdims=True)
    mx = jnp.maximum(jnp.max(s, axis=-1, keepdims=True), s_new)
    p = jnp.exp(s - mx)
    p_new = jnp.exp(s_new - mx)
    l = jnp.sum(p, axis=-1, keepdims=True) + p_new
    o_win = (per_row(lambda r: _dot_nt(rows_of(p, r).astype(BF16), win_ref[r, 1].astype(BF16)))
             + p_new.astype(BF16).astype(F32) * new_rows(3 * LANES)) / l

    o = gate_ref[:, 0:1] * o_cmp + gate_ref[:, 1:2] * o_sel + gate_ref[:, 2:3] * o_win
    in_group = _shr(lane, HEAD_DIM) == _shr(head, N_REP)
    o_ref[...] = jnp.where(in_group, o, 0.0).astype(o_ref.dtype)

    for r in range(rps):
        bidx = pl.program_id(0) * rps + r
        for kv in range(2):
            wout_ref[r, kv] = jnp.where(pos == wbuf - 1, _column(wnew_ref[kv * LANES:(kv + 1) * LANES, :], bidx),
                                        pltpu.roll(win_ref[r, kv], wbuf - 1, 1))


def _nsa_decode(page_table, cache_nsa, win_state, layer, nq_s, nsab_s, gates_s, wint_s, cw, prev_win=None):
    db, n_pages = page_table.shape
    depth, wbuf = win_state.shape[0], win_state.shape[4]
    prev = () if prev_win is None else (prev_win,)
    rps = DECODE_ROWS
    assert db % rps == 0 and n_pages % 2 == 0
    m = rps * N_HEADS
    nhb = n_pages * PAGE // CMP_STRIDE
    in_specs = [pl.BlockSpec((None, None, 4, LANES, PAGE), lambda b, pt, i=i, r=r: (layer, pt[b * rps + r, i], 0, 0, 0))
                for r in range(rps) for i in range(n_pages)]
    in_specs += [pl.BlockSpec((None, rps, 2, LANES, wbuf), lambda b, pt: (layer, b, 0, 0, 0)),
                 pl.BlockSpec((m, LANES), lambda b, pt: (b, 0)),
                 pl.BlockSpec((rps, 1, 512), lambda b, pt: (b, 0, 0)),
                 pl.BlockSpec((m, 3), lambda b, pt: (b, 0)),
                 pl.BlockSpec(wint_s.shape, lambda b, pt: (0, 0))]
    in_specs += [pl.BlockSpec(a.shape, lambda b, pt, nd=a.ndim: (0,) * nd, pipeline_mode=pl.Buffered(1)) for a in cw]
    in_specs += [pl.BlockSpec(memory_space=pl.ANY)] * len(prev)
    grid_spec = pltpu.PrefetchScalarGridSpec(
        num_scalar_prefetch=1,
        grid=(db // rps,),
        in_specs=in_specs,
        out_specs=[pl.BlockSpec((m, LANES), lambda b, pt: (b, 0)),
                   pl.BlockSpec((None, rps, 2, LANES, wbuf), lambda b, pt: (layer, b, 0, 0, 0))],
        scratch_shapes=[pltpu.VMEM((rps * nhb, CMP_STRIDE * 256), BF16),
                        pltpu.VMEM((rps * nhb, CMP_STRIDE * 256), BF16),
                        pltpu.VMEM((m, n_pages * PAGE), F32)],
    )
    return pl.pallas_call(
        functools.partial(_nsa_decode_body, n_pages=n_pages, rps=rps, n_prev=len(prev)),
        grid_spec=grid_spec,
        out_shape=[jax.ShapeDtypeStruct((db * N_HEADS, LANES), BF16),
                   jax.ShapeDtypeStruct((depth, db, 2, LANES, wbuf), F32)],
        input_output_aliases={1 + rps * n_pages + 5 + len(cw): 1} if prev else {},
        compiler_params=_cparams("arbitrary"),
        name="nsa_decode",
    )(page_table, *([cache_nsa] * (rps * n_pages)), win_state, nq_s, nsab_s, gates_s, wint_s, *cw, *prev)


O_F, O_GM, O_NQ, O_NKV, O_NG, O_MG = 1536, 1544, 2568, 3080, 3848, 3872
_HEAD_PLACE = (np.arange(N_HEADS)[:, None] // N_REP == np.arange(N_GROUPS)[None, :]).astype(np.float32)


def _prep_w_in(w_in):
    wt = jnp.swapaxes(w_in, 1, 2).astype(BF16)
    depth, _, d = wt.shape
    nq = wt[:, O_NQ:O_NKV]
    nq_pad = (nq.reshape(depth, N_HEADS, 1, HEAD_DIM, d) * _HEAD_PLACE.astype(BF16)[None, :, :, None, None])
    small = jnp.concatenate([wt[:, O_F:O_GM], wt[:, O_NG:O_MG], jnp.zeros((depth, LANES - 32, d), BF16)], axis=1)
    rows = jnp.concatenate([wt[:, 0:O_F], wt[:, O_GM:O_NQ], nq_pad.reshape(depth, N_HEADS * LANES, d),
                            wt[:, O_NKV:O_NG], small, nq], axis=1)
    return rows, wt[:, O_MG:]


def _pad_heads(g):
    return (_HEAD_PLACE[:, :, None] * g[None, None, :]).reshape(1, N_HEADS * LANES)


def _prep_compress(w1, w2, pe, kg0):
    w1r = w1.reshape(2, 2, CMP_STRIDE, HEAD_DIM, HEAD_DIM).astype(BF16)

    def block_diag(blocks):
        z = jnp.zeros_like(blocks[0])
        return jnp.concatenate([jnp.concatenate([blk if i == j else z for j in range(4)], axis=-1)
                                for i, blk in enumerate(blocks)], axis=-2)

    wl = block_diag([w1r[c] for c in range(2) for _ in range(N_GROUPS)]).reshape(2, CMP_STRIDE * 256, 256)
    w2bd = block_diag([w2[c] for c in range(2) for _ in range(N_GROUPS)])
    pex = jnp.broadcast_to(pe.transpose(1, 0, 2)[:, :, None, :], (CMP_LEN, 2, N_GROUPS, HEAD_DIM))
    pex = pex.reshape(2, CMP_STRIDE, 1, 256)
    pet = jnp.tile(pex.reshape(2, CMP_STRIDE, 256).transpose(0, 2, 1), (1, 1, 2 * PAGE // CMP_STRIDE))
    return (wl[0], wl[1], pex[0], pex[1], w2bd.astype(BF16), jnp.tile(kg0, 2)[None, :], pet)


def _layer(xp, xs, lw, page_table, caches, layer, depth, prev, dims):
    b, s, db = dims

    xp = _ffn(xp, *lw['ffn1'], layer)
    xs = _ffn(xs, *lw['ffn1'], layer)

    pw = lw['proj']
    (fq, foxt, foxtb, gmu, gmv, nq, nsat, wint, nsatb, cmpraw, lft, gates) = _proj(
        xp, pw, b, s, False, layer, depth, None if prev is None else prev[:4])
    (fq_s, foxt_s, _, gmu_s, gmv_s, nq_s, nsat_s, wint_s, _, _, lft_s, gates_s,
     foxkv_s, nsab_s) = _proj(xs, pw, 1, db, True, layer)

    c = _cumsum(lft, layer).reshape(b, 4, 2, s)
    o_fox_p = _fox_prompt(fq.reshape(b, s, 512), foxtb, c)
    o_gm_p = _gmlp_prompt(gmu, gmv, lw['gmlp_w_s'], lw['gmlp_bexp'])
    cmpkv = _compress_prompt(cmpraw.reshape(b, s, 256), lw['cw'])
    o_nsa_p = _nsa_prompt(nq.reshape(b, s, 512), cmpkv, nsatb, lw['nsa_aug'], gates.reshape(b, s, LANES))
    xp = _merge(xp, lw['mix_norm'], o_fox_p.reshape(b * s, 512), o_gm_p, o_nsa_p.reshape(b * s, 512),
                lw['w_gate'], lw['wb'], lw['wb2'], lw['w_out'], layer)

    cache_fox_kv, cache_fox_lf, cache_nsa, win_state = caches
    o_fox_s = _fox_decode(page_table, cache_fox_kv, _logf_suffix(cache_fox_lf, layer), layer, fq_s.reshape(db, 1, 512),
                          foxkv_s.reshape(db, 1, 1024), lft_s.reshape(N_HEADS, db))
    o_gm_s = _gmlp_sample(gmu_s, gmv_s, lw['gmlp_w00'], lw['gmlp_b0'])
    g3 = gates_s[:, 8:32].reshape(db, 3, N_HEADS).transpose(0, 2, 1).reshape(db * N_HEADS, 3)
    o_nsa_s, win_next = _nsa_decode(page_table, cache_nsa, win_state, layer, nq_s.reshape(db * N_HEADS, LANES),
                                    nsab_s.reshape(db, 1, 512), g3, wint_s.reshape(256, db), lw['cw'],
                                    None if prev is None else prev[4])
    xs = _merge(xs, lw['mix_norm'], o_fox_s.reshape(db, 512), o_gm_s, o_nsa_s.reshape(db, 1024),
                lw['w_gate'], lw['wb'], lw['wb2'], lw['w_out'], layer)

    xp = _ffn(xp, *lw['ffn2'], layer)
    xs = _ffn(xs, *lw['ffn2'], layer)

    state = dict(
        fox_kv_s=foxt_s, fox_logf_s=lft_s, nsa_kv_s=nsat_s,
        gmlp_v_p=gmv.reshape(b, s, 512)[:, s - min(CHUNK, s):], gmlp_v_s=gmv_s.reshape(db, 1, 512),
    )
    return xp, xs, state, (foxt, nsat, wint, lft, win_next)


def kernel(x_prompt, x_sample, page_table, cache_fox_kv, cache_fox_logf, cache_nsa_kv, state_nsa_win, ffn1_norm, ffn1_w_gu, ffn1_w_down, mix_norm, w_in, fox_b_f, fox_qk_gain, gmlp_v_gain, gmlp_w_s, gmlp_b_s, nsa_q_gain, nsa_k_gain, nsa_cmp_pe, nsa_cmp_w1, nsa_cmp_w2, w_branch, w_out, ffn2_norm, ffn2_w_gu, ffn2_w_down):
    b, s, d = x_prompt.shape
    db = x_sample.shape[0]
    depth = w_in.shape[0]
    n_pool = cache_fox_kv.shape[1]
    wb = state_nsa_win.shape[2]
    xp = x_prompt.reshape(b * s, d)
    xs = x_sample.reshape(db, d)

    caches = (cache_fox_kv.transpose(0, 1, 3, 4, 5, 2).reshape(depth, n_pool, 2, N_HEADS * HEAD_DIM, PAGE),
              cache_fox_logf.transpose(0, 1, 3, 2),
              cache_nsa_kv.transpose(0, 1, 3, 4, 5, 2).reshape(depth, n_pool, 4, LANES, PAGE),
              state_nsa_win.transpose(0, 1, 3, 4, 5, 2).reshape(depth, db, 2, LANES, wb))

    nsa_rows = (np.arange(N_HEADS * LANES) % LANES) // HEAD_DIM == (np.arange(N_HEADS * LANES) // LANES) // N_REP
    col = lambda g: g[:, None]
    ffn1 = (ffn1_norm[:, None, :], ffn1_w_gu.astype(BF16), ffn1_w_down.astype(BF16))
    ffn2 = (ffn2_norm[:, None, :], ffn2_w_gu.astype(BF16), ffn2_w_down.astype(BF16))
    wb_all = w_branch.astype(BF16).reshape(depth * 3, w_branch.shape[2], d)
    wo_all = w_out.astype(BF16)
    w_rows, w_gate = _prep_w_in(w_in)
    nsa_aug = _nsa_aug(s)
    states = []
    stacked = None
    for l in range(depth):
        wb2 = jnp.zeros((N_HEADS * LANES, d), F32).at[np.nonzero(nsa_rows)[0]].set(w_branch[l, 2])
        proj = (mix_norm[l][None, :], w_rows,
                jnp.tile(fox_qk_gain[l, 0], 2)[None, :], jnp.tile(fox_qk_gain[l, 1], 2)[None, :],
                col(fox_qk_gain[l, 1]), gmlp_v_gain[l][None, :], _pad_heads(nsa_q_gain[l]),
                jnp.tile(nsa_k_gain[l, 1], 2)[None, :], jnp.tile(nsa_k_gain[l, 2], 2)[None, :],
                col(nsa_k_gain[l, 1]), col(nsa_k_gain[l, 2]), col(fox_b_f[l]), jnp.tile(nsa_q_gain[l], 2)[None, :])
        lw = dict(
            ffn1=ffn1, ffn2=ffn2, wb=wb_all, w_out=wo_all,
            mix_norm=mix_norm[l][None, :], proj=proj, w_gate=w_gate,
            gmlp_w_s=gmlp_w_s[l], gmlp_bexp=jnp.repeat(gmlp_b_s[l].T, HEAD_DIM, axis=1),
            gmlp_w00=jnp.repeat(gmlp_w_s[l, :, 0, 0], HEAD_DIM)[None, :],
            gmlp_b0=jnp.repeat(gmlp_b_s[l, :, 0], HEAD_DIM)[None, :],
            cw=_prep_compress(nsa_cmp_w1[l], nsa_cmp_w2[l], nsa_cmp_pe[l], nsa_k_gain[l, 0]),
            wb2=wb2.astype(BF16), nsa_aug=nsa_aug,
        )
        xp, xs, st, stacked = _layer(xp, xs, lw, page_table, caches, l, depth, stacked, (b, s, db))
        states.append(st)

    def stack(name):
        return jnp.stack([st[name] for st in states])

    def tok_major(a, *feat_dims):
        return jnp.moveaxis(a.reshape(a.shape[:2] + feat_dims + a.shape[3:]), -1, 2)

    def tok_major_s(a, *feat_dims):
        return jnp.swapaxes(tok_major(a, *feat_dims), 1, 2)

    foxt, nsat, wint, lft, win_next = stacked
    return (xp.reshape(b, s, d), xs.reshape(db, 1, d),
            tok_major(foxt, 2, N_HEADS, HEAD_DIM), tok_major_s(stack('fox_kv_s'), 2, N_HEADS, HEAD_DIM),
            tok_major(lft, N_HEADS), tok_major_s(stack('fox_logf_s'), N_HEADS),
            tok_major(nsat, 4, N_GROUPS, HEAD_DIM), tok_major_s(stack('nsa_kv_s'), 4, N_GROUPS, HEAD_DIM),
            tok_major(wint[:, :, :, s - min(WINDOW, s):], 2, N_GROUPS, HEAD_DIM),
            tok_major(win_next.reshape(depth, db, 2 * LANES, wb), 2, N_GROUPS, HEAD_DIM),
            stack('gmlp_v_p'), stack('gmlp_v_s'))
```
